```python
import math
import jax, jax.numpy as jnp
from jax import lax
import numpy as np

D_MODEL = 1024
BATCH = 16
SEQ = 2048
DEPTH = 1

D_RNN = D_MODEL // 2
N_RNN_BLOCKS = 8
RNN_BLOCK = D_RNN // N_RNN_BLOCKS
CONV_WIDTH = 4
LRU_C = 8.0
DIFF_HEAD_DIM = 64
N_DIFF_HEADS = (D_MODEL // 2) // (2 * DIFF_HEAD_DIM)
D_ATTN = N_DIFF_HEADS * 2 * DIFF_HEAD_DIM
D_MIX = D_RNN + D_ATTN
D_IN = 2 * D_RNN + 3 * D_ATTN
Q_BLOCK = 128
N_EXPERTS = 32
TOP_K = 4
D_FF = D_MODEL
SWIGLU_LIMIT = 7.0
SWIGLU_ALPHA = 1.702
MOE_BLOCK = 128
EPS = 1e-5

kernel_name = "hybrid_rglru_diffattn_moe"


def rms_norm(x, g):
    xf = x.astype(jnp.float32)
    y = xf * lax.rsqrt(jnp.mean(xf * xf, axis=-1, keepdims=True) + EPS)
    return (y * g.astype(jnp.float32)).astype(x.dtype)


def causal_depthwise_conv(x, w, b):
    c = x.shape[-1]
    y = lax.conv_general_dilated(
        x, w[:, None, :].astype(x.dtype), window_strides=(1,),
        padding=[(CONV_WIDTH - 1, 0)], dimension_numbers=("NWC", "WIO", "NWC"),
        feature_group_count=c)
    return y + b


def _lru_combine(left, right):
    a_l, h_l = left
    a_r, h_r = right
    return a_l * a_r, a_r * h_l + h_r


def rg_lru(x, w_a, b_a, w_x, b_x, lam):
    bsz, seq, _ = x.shape
    xb = x.reshape(bsz, seq, N_RNN_BLOCKS, RNN_BLOCK)
    r = jax.nn.sigmoid(jnp.einsum("bsni,nij->bsnj", xb, w_a).reshape(bsz, seq, D_RNN) + b_a)
    i = jax.nn.sigmoid(jnp.einsum("bsni,nij->bsnj", xb, w_x).reshape(bsz, seq, D_RNN) + b_x)
    log_a = -LRU_C * r.astype(jnp.float32) * jax.nn.softplus(-lam.astype(jnp.float32))
    a = jnp.exp(log_a)
    u = jnp.sqrt(-jnp.expm1(2.0 * log_a)) * (i * x).astype(jnp.float32)
    _, h = lax.associative_scan(_lru_combine, (a, u), axis=1)
    return h.astype(x.dtype)


def diff_attention(q, k, v, lam):
    bsz, seq = q.shape[:2]
    nb = seq // Q_BLOCK
    q = jnp.transpose(q, (0, 2, 3, 1, 4))
    k = jnp.transpose(k, (0, 2, 3, 1, 4))
    v = jnp.transpose(v, (0, 2, 1, 3))
    qb = jnp.moveaxis(q.reshape(bsz, N_DIFF_HEADS, 2, nb, Q_BLOCK, DIFF_HEAD_DIM), 3, 0)
    kpos = jnp.arange(seq)

    def one_block(args):
        q_blk, blk = args
        s = jnp.einsum("bhcqd,bhckd->bhcqk", q_blk, k).astype(jnp.float32)
        qpos = blk * Q_BLOCK + jnp.arange(Q_BLOCK)
        s = jnp.where(kpos[None, :] <= qpos[:, None], s, -jnp.inf)
        p = jax.nn.softmax(s, axis=-1)
        w = p[:, :, 0] - lam * p[:, :, 1]
        return jnp.einsum("bhqk,bhke->bhqe", w.astype(v.dtype), v)

    o = lax.map(one_block, (qb, jnp.arange(nb)))
    return jnp.transpose(o, (1, 0, 3, 2, 4)).reshape(bsz, seq, N_DIFF_HEADS, 2 * DIFF_HEAD_DIM)


def moe_ffn(h, router_w, router_b, w1, b1, w2, b2):
    bsz, seq, d = h.shape
    n_tok = bsz * seq
    xt = h.reshape(n_tok, d)
    logits = jnp.dot(xt.astype(jnp.float32), router_w.astype(jnp.float32)) + router_b.astype(jnp.float32)
    top_val, top_idx = lax.top_k(logits, TOP_K)
    gates = jax.nn.softmax(top_val, axis=-1).reshape(-1)
    flat_e = top_idx.reshape(-1)
    n_asg = n_tok * TOP_K
    order = jnp.argsort(flat_e)
    sorted_e = flat_e[order]
    tok = order // TOP_K
    counts = jnp.bincount(flat_e, length=N_EXPERTS)
    padded = (counts + MOE_BLOCK - 1) // MOE_BLOCK * MOE_BLOCK
    pad_end = jnp.cumsum(padded)
    pad_start = pad_end - padded
    grp_start = jnp.cumsum(counts) - counts
    dest = pad_start[sorted_e] + jnp.arange(n_asg) - grp_start[sorted_e]
    n_blocks = (n_asg + MOE_BLOCK - 1) // MOE_BLOCK + N_EXPERTS
    src = jnp.full((n_blocks * MOE_BLOCK,), n_asg, dtype=jnp.int32).at[dest].set(
        jnp.arange(n_asg, dtype=jnp.int32))
    x_sorted = jnp.concatenate([xt[tok], jnp.zeros((1, d), xt.dtype)], axis=0)
    x_blocks = x_sorted[src].reshape(n_blocks, MOE_BLOCK, d)
    blk_e = jnp.minimum(jnp.searchsorted(pad_end, jnp.arange(n_blocks) * MOE_BLOCK, side="right"),
                        N_EXPERTS - 1)

    def expert_block(args):
        xb, e = args
        hcat = xb @ w1[e] + b1[e]
        gate = jnp.minimum(hcat[:, :D_FF], SWIGLU_LIMIT)
        up = jnp.clip(hcat[:, D_FF:], -SWIGLU_LIMIT, SWIGLU_LIMIT)
        act = gate * jax.nn.sigmoid(SWIGLU_ALPHA * gate) * (up + 1.0)
        return act @ w2[e] + b2[e]

    y_blocks = lax.map(expert_block, (x_blocks, blk_e)).reshape(-1, d)
    y_sorted = y_blocks[dest] * gates[order][:, None].astype(y_blocks.dtype)
    y = jax.ops.segment_sum(y_sorted, tok, num_segments=n_tok)
    return y.reshape(bsz, seq, d)


def setup_inputs(seed: int = 0) -> dict:
    key = jax.random.key(seed)
    ks = jax.random.split(key, 25)

    def nrm(k, shape, scale):
        return jax.random.normal(k, shape, jnp.float32) * scale

    a_c = jax.random.uniform(ks[9], (DEPTH, D_RNN), jnp.float32, minval=0.9, maxval=0.999)
    s = a_c ** (1.0 / LRU_C)
    return {
        "x": nrm(ks[0], (BATCH, SEQ, D_MODEL), 1.0),
        "norm1_g": 1.0 + nrm(ks[1], (DEPTH, D_MODEL), 0.02),
        "w_in": nrm(ks[2], (DEPTH, D_MODEL, D_IN), D_MODEL ** -0.5),
        "conv_w": nrm(ks[3], (DEPTH, CONV_WIDTH, D_RNN), CONV_WIDTH ** -0.5),
        "conv_b": nrm(ks[4], (DEPTH, D_RNN), 0.02),
        "lru_wa": nrm(ks[5], (DEPTH, N_RNN_BLOCKS, RNN_BLOCK, RNN_BLOCK), RNN_BLOCK ** -0.5),
        "lru_ba": nrm(ks[6], (DEPTH, D_RNN), 0.02),
        "lru_wx": nrm(ks[7], (DEPTH, N_RNN_BLOCKS, RNN_BLOCK, RNN_BLOCK), RNN_BLOCK ** -0.5),
        "lru_bx": nrm(ks[8], (DEPTH, D_RNN), 0.02),
        "lru_lambda": jnp.log(s) - jnp.log1p(-s),
        "q_norm_g": 1.0 + nrm(ks[10], (DEPTH, DIFF_HEAD_DIM), 0.02),
        "k_norm_g": 1.0 + nrm(ks[11], (DEPTH, DIFF_HEAD_DIM), 0.02),
        "lambda_q1": nrm(ks[12], (DEPTH, DIFF_HEAD_DIM), 0.1),
        "lambda_k1": nrm(ks[13], (DEPTH, DIFF_HEAD_DIM), 0.1),
        "lambda_q2": nrm(ks[14], (DEPTH, DIFF_HEAD_DIM), 0.1),
        "lambda_k2": nrm(ks[15], (DEPTH, DIFF_HEAD_DIM), 0.1),
        "subln_g": 1.0 + nrm(ks[16], (DEPTH, 2 * DIFF_HEAD_DIM), 0.02),
        "w_out": nrm(ks[17], (DEPTH, D_MIX, D_MODEL), D_MIX ** -0.5),
        "norm2_g": 1.0 + nrm(ks[18], (DEPTH, D_MODEL), 0.02),
        "router_w": nrm(ks[19], (DEPTH, D_MODEL, N_EXPERTS), D_MODEL ** -0.5),
        "router_b": nrm(ks[20], (DEPTH, N_EXPERTS), 0.01),
        "w1": nrm(ks[21], (DEPTH, N_EXPERTS, D_MODEL, 2 * D_FF), D_MODEL ** -0.5),
        "b1": nrm(ks[22], (DEPTH, N_EXPERTS, 2 * D_FF), 0.02),
        "w2": nrm(ks[23], (DEPTH, N_EXPERTS, D_FF, D_MODEL), D_FF ** -0.5),
        "b2": nrm(ks[24], (DEPTH, N_EXPERTS, D_MODEL), 0.02),
    }


def reference(x, norm1_g, w_in, conv_w, conv_b, lru_wa, lru_ba, lru_wx, lru_bx, lru_lambda,
              q_norm_g, k_norm_g, lambda_q1, lambda_k1, lambda_q2, lambda_k2, subln_g, w_out,
              norm2_g, router_w, router_b, w1, b1, w2, b2):
    bsz, seq, _ = x.shape
    splits = [D_RNN, 2 * D_RNN, 2 * D_RNN + D_ATTN, 2 * D_RNN + 2 * D_ATTN]
    for l in range(DEPTH):
        h = rms_norm(x, norm1_g[l])
        z = h @ w_in[l]
        xr, gr, q, k, v = jnp.split(z, splits, axis=-1)
        xr = causal_depthwise_conv(xr, conv_w[l], conv_b[l])
        y_rnn = rg_lru(xr, lru_wa[l], lru_ba[l], lru_wx[l], lru_bx[l], lru_lambda[l]) \
            * jax.nn.gelu(gr, approximate=True)
        q = rms_norm(q.reshape(bsz, seq, N_DIFF_HEADS, 2, DIFF_HEAD_DIM), q_norm_g[l]) \
            * (DIFF_HEAD_DIM ** -0.5)
        k = rms_norm(k.reshape(bsz, seq, N_DIFF_HEADS, 2, DIFF_HEAD_DIM), k_norm_g[l])
        v = v.reshape(bsz, seq, N_DIFF_HEADS, 2 * DIFF_HEAD_DIM)
        lam_init = 0.8 - 0.6 * math.exp(-0.3 * l)
        lam = (jnp.exp(jnp.sum(lambda_q1[l].astype(jnp.float32) * lambda_k1[l].astype(jnp.float32)))
               - jnp.exp(jnp.sum(lambda_q2[l].astype(jnp.float32) * lambda_k2[l].astype(jnp.float32)))
               + lam_init)
        o = diff_attention(q, k, v, lam)
        y_attn = (rms_norm(o, subln_g[l]) * (1.0 - lam_init)).reshape(bsz, seq, D_ATTN)
        x = x + jnp.concatenate([y_rnn, y_attn], axis=-1) @ w_out[l]
        x = x + moe_ffn(rms_norm(x, norm2_g[l]), router_w[l], router_b[l], w1[l], b1[l], w2[l], b2[l])
    return x
```

```python
import functools
import math

import jax
import jax.numpy as jnp
from jax import lax
from jax.experimental import pallas as pl
from jax.experimental.pallas import tpu as pltpu

F32 = jnp.float32
BF16 = jnp.bfloat16

D_MODEL = 1024
D_RNN = 512
N_RNN_BLOCKS = 8
RNN_BLOCK = 64
CONV_WIDTH = 4
LRU_C = 8.0
HEAD_DIM = 64
N_HEADS = 4
D_ATTN = 512
D_IN = 2 * D_RNN + 3 * D_ATTN
N_EXPERTS = 32
TOP_K = 4
D_FF = 1024
SWIGLU_LIMIT = 7.0
SWIGLU_ALPHA = 1.702
EPS = 1e-5
LAM_INIT = 0.8 - 0.6 * math.exp(0.0)

LANES = 128
VMEM_LIMIT = 52 * 1024 * 1024

TM_PROJ = 512
T_SCAN = 256
TQ = 256
BM = 256
TD = 256


def _cparams(sem):
    return pltpu.CompilerParams(dimension_semantics=sem, vmem_limit_bytes=VMEM_LIMIT)


def _in_proj_kernel(x_ref, g_ref, w_ref, z_ref):
    x = x_ref[...]
    ms = jnp.mean(x * x, axis=-1, keepdims=True)
    h = x * lax.rsqrt(ms + EPS) * g_ref[...]
    z_ref[...] = jnp.dot(h.astype(BF16), w_ref[...], preferred_element_type=F32)


def _in_proj(x2, g, w_bf):
    n = x2.shape[0]
    return pl.pallas_call(
        _in_proj_kernel,
        grid=(n // TM_PROJ,),
        in_specs=[
            pl.BlockSpec((TM_PROJ, D_MODEL), lambda i: (i, 0)),
            pl.BlockSpec((1, D_MODEL), lambda i: (0, 0)),
            pl.BlockSpec((D_MODEL, D_IN), lambda i: (0, 0)),
        ],
        out_specs=pl.BlockSpec((TM_PROJ, D_IN), lambda i: (i, 0)),
        out_shape=jax.ShapeDtypeStruct((n, D_IN), F32),
        compiler_params=_cparams(("arbitrary",)),
        name="in_proj",
    )(x2, g, w_bf)


def _rnn_kernel(xr_ref, gr_ref, cw_ref, cb_ref, wg_ref, bg_ref, lam_ref, y_ref):
    seq = xr_ref.shape[1]
    n_chunks = seq // T_SCAN
    cw = cw_ref[...]
    cb = cb_ref[...]
    lam = lam_ref[...]
    nl = -lam
    softplus_neg_lam = jnp.maximum(nl, 0.0) + jnp.log(1.0 + jnp.exp(-jnp.abs(nl)))
    row = lax.broadcasted_iota(jnp.int32, (T_SCAN, D_RNN), 0)

    def chunk(c, h_prev):
        t0 = pl.multiple_of(c * T_SCAN, T_SCAN)
        cur = xr_ref[0, pl.ds(t0, T_SCAN), :]
        p0 = pl.multiple_of(jnp.maximum(t0 - 8, 0), 8)
        prev = xr_ref[0, pl.ds(p0, 8), :]
        prev = jnp.where(c > 0, prev, 0.0)
        xc = jnp.concatenate([prev, cur], axis=0)
        conv = cb + cw[3:4, :] * cur
        for j in range(1, CONV_WIDTH):
            sh = pltpu.roll(xc, j, axis=0)[8:, :]
            conv = conv + cw[3 - j:4 - j, :] * sh
        gates = jnp.dot(conv.astype(BF16), wg_ref[...], preferred_element_type=F32) + bg_ref[...]
        r = jax.nn.sigmoid(gates[:, :D_RNN])
        i = jax.nn.sigmoid(gates[:, D_RNN:])
        log_a = -LRU_C * r * softplus_neg_lam
        a = jnp.exp(log_a)
        u = jnp.sqrt(1.0 - a * a) * (i * conv)
        d = 1
        while d < T_SCAN:
            a_sh = pltpu.roll(a, d, axis=0)
            u_sh = pltpu.roll(u, d, axis=0)
            keep = row >= d
            u = jnp.where(keep, a * u_sh + u, u)
            a = jnp.where(keep, a * a_sh, a)
            d *= 2
        h = a * h_prev + u
        y_ref[0, pl.ds(t0, T_SCAN), :] = h * jax.nn.gelu(gr_ref[0, pl.ds(t0, T_SCAN), :], approximate=True)
        return h[T_SCAN - 1:T_SCAN, :]

    lax.fori_loop(0, n_chunks, chunk, jnp.zeros((1, D_RNN), F32))


def _rnn(z3, conv_w, conv_b, wg_bf, bg, lam):
    bsz, seq, _ = z3.shape
    const = lambda shape: pl.BlockSpec(shape, lambda b: (0,) * len(shape))
    return pl.pallas_call(
        _rnn_kernel,
        grid=(bsz,),
        in_specs=[
            pl.BlockSpec((1, seq, D_RNN), lambda b: (b, 0, 0)),
            pl.BlockSpec((1, seq, D_RNN), lambda b: (b, 0, 1)),
            const((CONV_WIDTH, D_RNN)),
            const((1, D_RNN)),
            const((D_RNN, 2 * D_RNN)),
            const((1, 2 * D_RNN)),
            const((1, D_RNN)),
        ],
        out_specs=pl.BlockSpec((1, seq, D_RNN), lambda b: (b, 0, 0)),
        out_shape=jax.ShapeDtypeStruct((bsz, seq, D_RNN), F32),
        compiler_params=_cparams(("arbitrary",)),
        name="rnn",
    )(z3, z3, conv_w, conv_b, wg_bf, bg, lam)


def _group_rms(x, ones_bd):
    x2 = x * x
    hi = x2.astype(BF16)
    lo = (x2 - hi.astype(F32)).astype(BF16)
    ssq = jnp.dot(hi, ones_bd, preferred_element_type=F32) + jnp.dot(lo, ones_bd, preferred_element_type=F32)
    return x * lax.rsqrt(ssq * (1.0 / HEAD_DIM) + EPS)


def _attn_kernel(q_ref, k_ref, v_ref, qg_ref, kg_ref, lq1_ref, lk1_ref, lq2_ref, lk2_ref, sg_ref, ones_ref, o_ref):
    seq = q_ref.shape[1]
    ones_bd = ones_ref[...]
    lam = (jnp.exp(jnp.sum(lq1_ref[...] * lk1_ref[...], axis=-1, keepdims=True))
           - jnp.exp(jnp.sum(lq2_ref[...] * lk2_ref[...], axis=-1, keepdims=True)) + LAM_INIT)
    qn = _group_rms(q_ref[0], ones_bd) * qg_ref[...] * (HEAD_DIM ** -0.5)
    kn = _group_rms(k_ref[0], ones_bd) * kg_ref[...]
    lane = lax.broadcasted_iota(jnp.int32, (seq, LANES), 1)
    q1 = jnp.where(lane < HEAD_DIM, qn, 0.0).astype(BF16)
    q2 = jnp.where(lane >= HEAD_DIM, qn, 0.0).astype(BF16)
    kb = kn.astype(BF16)
    vb = v_ref[0].astype(BF16)
    sg = sg_ref[...]
    dn = (((1,), (1,)), ((), ()))
    for qi in range(seq // TQ):
        kv = (qi + 1) * TQ
        rows = slice(qi * TQ, kv)
        qpos = qi * TQ + lax.broadcasted_iota(jnp.int32, (TQ, kv), 0)
        kpos = lax.broadcasted_iota(jnp.int32, (TQ, kv), 1)
        causal = kpos <= qpos

        def probs(qm):
            s = lax.dot_general(qm[rows], kb[:kv], dn, preferred_element_type=F32)
            s = jnp.where(causal, s, -jnp.inf)
            e = jnp.exp(s - jnp.max(s, axis=-1, keepdims=True))
            return e, jnp.sum(e, axis=-1, keepdims=True)

        e1, l1 = probs(q1)
        e2, l2 = probs(q2)
        w = e1 * (1.0 / l1) - e2 * (lam / l2)
        o = jnp.dot(w.astype(BF16), vb[:kv], preferred_element_type=F32)
        o = o * lax.rsqrt(jnp.mean(o * o, axis=-1, keepdims=True) + EPS) * sg * (1.0 - LAM_INIT)
        o_ref[0, rows, :] = o


def _attn(z3, qg2, kg2, lq1, lk1, lq2, lk2, sg, ones_bd):
    bsz, seq, _ = z3.shape
    qoff = 2 * D_RNN // LANES
    koff = qoff + D_ATTN // LANES
    voff = koff + D_ATTN // LANES
    const = lambda shape: pl.BlockSpec(shape, lambda b, h: (0,) * len(shape))
    return pl.pallas_call(
        _attn_kernel,
        grid=(bsz, N_HEADS),
        in_specs=[
            pl.BlockSpec((1, seq, LANES), lambda b, h: (b, 0, qoff + h)),
            pl.BlockSpec((1, seq, LANES), lambda b, h: (b, 0, koff + h)),
            pl.BlockSpec((1, seq, LANES), lambda b, h: (b, 0, voff + h)),
            const((1, LANES)), const((1, LANES)),
            const((1, HEAD_DIM)), const((1, HEAD_DIM)), const((1, HEAD_DIM)), const((1, HEAD_DIM)),
            const((1, LANES)), const((LANES, LANES)),
        ],
        out_specs=pl.BlockSpec((1, seq, LANES), lambda b, h: (b, 0, h)),
        out_shape=jax.ShapeDtypeStruct((bsz, seq, D_ATTN), F32),
        compiler_params=_cparams(("arbitrary", "arbitrary")),
        name="attn",
    )(z3, z3, z3, qg2, kg2, lq1, lk1, lq2, lk2, sg, ones_bd)


def _out_proj_kernel(yr_ref, ya_ref, x_ref, wo_ref, g_ref, rwh_ref, rwl_ref, rb_ref,
                     x1_ref, h2_ref, ids_ref, gates_ref):
    acc = jnp.dot(yr_ref[...].astype(BF16), wo_ref[0], preferred_element_type=F32)
    acc = acc + jnp.dot(ya_ref[...].astype(BF16), wo_ref[1], preferred_element_type=F32)
    x1 = x_ref[...] + acc
    x1_ref[...] = x1
    h2 = x1 * lax.rsqrt(jnp.mean(x1 * x1, axis=-1, keepdims=True) + EPS) * g_ref[...]
    h2_ref[...] = h2
    hh = h2.astype(BF16)
    hl = (h2 - hh.astype(F32)).astype(BF16)
    logits = (jnp.dot(hh, rwh_ref[...], preferred_element_type=F32)
              + jnp.dot(hl, rwh_ref[...], preferred_element_type=F32)
              + jnp.dot(hh, rwl_ref[...], preferred_element_type=F32)) + rb_ref[...]
    tm = logits.shape[0]
    lane = lax.broadcasted_iota(jnp.int32, (tm, LANES), 1)
    l = jnp.where(lane < N_EXPERTS, logits, -jnp.inf)
    vals, idxs = [], []
    for _ in range(TOP_K):
        m = jnp.max(l, axis=-1, keepdims=True)
        idx = jnp.min(jnp.where(l == m, lane, LANES), axis=-1, keepdims=True)
        vals.append(m)
        idxs.append(idx)
        l = jnp.where(lane == idx, -jnp.inf, l)
    es = [jnp.exp(v - vals[0]) for v in vals]
    inv = 1.0 / (es[0] + es[1] + es[2] + es[3])
    ids = jnp.zeros((tm, LANES), jnp.int32)
    gates = jnp.zeros((tm, LANES), F32)
    for k in range(TOP_K):
        ids = jnp.where(lane == k, idxs[k], ids)
        gates = jnp.where(lane == k, es[k] * inv, gates)
    ids_ref[...] = ids
    gates_ref[...] = gates


def _out_proj(y_rnn, y_attn, x2, wo_bf, g2, rw_hi, rw_lo, rb):
    n = x2.shape[0]
    row = lambda w: pl.BlockSpec((TM_PROJ, w), lambda i: (i, 0))
    const = lambda shape: pl.BlockSpec(shape, lambda i: (0,) * len(shape))
    return pl.pallas_call(
        _out_proj_kernel,
        grid=(n // TM_PROJ,),
        in_specs=[row(D_RNN), row(D_ATTN), row(D_MODEL),
                  const((2, D_RNN, D_MODEL)), const((1, D_MODEL)),
                  const((D_MODEL, LANES)), const((D_MODEL, LANES)), const((1, LANES))],
        out_specs=[row(D_MODEL), row(D_MODEL), row(LANES), row(LANES)],
        out_shape=[jax.ShapeDtypeStruct((n, D_MODEL), F32), jax.ShapeDtypeStruct((n, D_MODEL), F32),
                   jax.ShapeDtypeStruct((n, LANES), jnp.int32), jax.ShapeDtypeStruct((n, LANES), F32)],
        compiler_params=_cparams(("arbitrary",)),
        name="out_proj",
    )(y_rnn, y_attn, x2, wo_bf, g2, rw_hi, rw_lo, rb)


def _route(ids, n_blocks):
    flat_e = ids.reshape(-1)
    onehot = (flat_e[:, None] == jnp.arange(N_EXPERTS, dtype=jnp.int32)[None, :]).astype(jnp.int32)
    csum = jnp.cumsum(onehot, axis=0)
    counts = csum[-1]
    padded = (counts + BM - 1) // BM * BM
    pad_end = jnp.cumsum(padded)
    pad_start = pad_end - padded
    pos = jnp.sum(onehot * (csum - 1 + pad_start[None, :]), axis=1).astype(jnp.int32)
    blk_e = jnp.minimum(jnp.searchsorted(pad_end, jnp.arange(n_blocks, dtype=jnp.int32) * BM, side="right"),
                        N_EXPERTS - 1).astype(jnp.int32)
    n_valid = (pad_end[-1] // BM).astype(jnp.int32).reshape(1)
    return pos, blk_e, n_valid, pad_end.astype(jnp.int32), padded.astype(jnp.int32)


def _dispatch_kernel(pad_end_ref, padded_ref, n_valid_ref, pos_ref, h_ref, xs_ref, zeros_ref, sem, zsem):
    i = pl.program_id(0)
    n_blocks = xs_ref.shape[0] // BM

    @pl.when(i == 0)
    def _():
        zeros_ref[...] = jnp.zeros_like(zeros_ref)

        def zero_block(start):
            cp = pltpu.make_async_copy(zeros_ref, xs_ref.at[pl.ds(pl.multiple_of(start, BM), BM), :], zsem)
            cp.start()
            cp.wait()

        def zero_tail(e, carry):
            @pl.when(padded_ref[e] > 0)
            def _():
                zero_block(pad_end_ref[e] - BM)
            return carry

        def zero_dead(b, carry):
            zero_block(b * BM)
            return carry

        lax.fori_loop(0, N_EXPERTS, zero_tail, 0)
        lax.fori_loop(n_valid_ref[0], n_blocks, zero_dead, 0)

    def row_copy(a):
        r = a // TOP_K
        return pltpu.make_async_copy(h_ref.at[pl.ds(r, 1), :], xs_ref.at[pl.ds(pos_ref[a], 1), :], sem)

    def issue(a, carry):
        row_copy(a).start()
        return carry

    def drain(a, carry):
        row_copy(a).wait()
        return carry

    lax.fori_loop(0, TD * TOP_K, issue, 0)
    lax.fori_loop(0, TD * TOP_K, drain, 0)


def _dispatch(h2, pos, pad_end, padded, n_valid, n_rows):
    n = h2.shape[0]
    grid_spec = pltpu.PrefetchScalarGridSpec(
        num_scalar_prefetch=3,
        grid=(n // TD,),
        in_specs=[
            pl.BlockSpec((TD * TOP_K,), lambda i, *_: (i,), memory_space=pltpu.SMEM),
            pl.BlockSpec((TD, D_MODEL), lambda i, *_: (i, 0)),
        ],
        out_specs=pl.BlockSpec(memory_space=pl.ANY),
        scratch_shapes=[pltpu.VMEM((BM, D_MODEL), F32), pltpu.SemaphoreType.DMA(()), pltpu.SemaphoreType.DMA(())],
    )
    return pl.pallas_call(
        _dispatch_kernel,
        grid_spec=grid_spec,
        out_shape=jax.ShapeDtypeStruct((n_rows, D_MODEL), F32),
        compiler_params=_cparams(("arbitrary",)),
        name="dispatch",
    )(pad_end, padded, n_valid, pos, h2)


def _experts_kernel(blk_e_ref, n_valid_ref, xs_ref, w1_ref, b1_ref, w2_ref, b2_ref, y_ref, w1b_ref, w2b_ref):
    i = pl.program_id(0)

    @pl.when(i < n_valid_ref[0])
    def _():
        prev_e = blk_e_ref[jnp.maximum(i - 1, 0)]

        @pl.when((i == 0) | (blk_e_ref[i] != prev_e))
        def _():
            w1b_ref[...] = w1_ref[0].astype(BF16)
            w2b_ref[...] = w2_ref[0].astype(BF16)

        x = xs_ref[...].astype(BF16)
        hcat = jnp.dot(x, w1b_ref[...], preferred_element_type=F32) + b1_ref[0]
        gate = jnp.minimum(hcat[:, :D_FF], SWIGLU_LIMIT)
        up = jnp.clip(hcat[:, D_FF:], -SWIGLU_LIMIT, SWIGLU_LIMIT)
        act = gate * jax.nn.sigmoid(SWIGLU_ALPHA * gate) * (up + 1.0)
        y_ref[...] = jnp.dot(act.astype(BF16), w2b_ref[...], preferred_element_type=F32) + b2_ref[0]

    @pl.when(i >= n_valid_ref[0])
    def _():
        y_ref[...] = jnp.zeros_like(y_ref)


def _experts(xs, blk_e, n_valid, w1, b1, w2, b2):
    n_rows = xs.shape[0]
    n_blocks = n_rows // BM
    blk = lambda i, be, nv: (jnp.minimum(i, nv[0] - 1), 0)
    exp3 = lambda i, be, nv: (be[i], 0, 0)
    grid_spec = pltpu.PrefetchScalarGridSpec(
        num_scalar_prefetch=2,
        grid=(n_blocks,),
        in_specs=[
            pl.BlockSpec((BM, D_MODEL), blk),
            pl.BlockSpec((1, D_MODEL, 2 * D_FF), exp3),
            pl.BlockSpec((1, 1, 2 * D_FF), exp3),
            pl.BlockSpec((1, D_FF, D_MODEL), exp3),
            pl.BlockSpec((1, 1, D_MODEL), exp3),
        ],
        out_specs=pl.BlockSpec((BM, D_MODEL), lambda i, be, nv: (i, 0)),
        scratch_shapes=[pltpu.VMEM((D_MODEL, 2 * D_FF), BF16), pltpu.VMEM((D_FF, D_MODEL), BF16)],
    )
    return pl.pallas_call(
        _experts_kernel,
        grid_spec=grid_spec,
        out_shape=jax.ShapeDtypeStruct((n_rows, D_MODEL), F32),
        compiler_params=_cparams(("arbitrary",)),
        name="experts",
    )(blk_e, n_valid, xs, w1, b1, w2, b2)


def _combine_kernel(pos_ref, x1_ref, gates_ref, ys_ref, o_ref, buf_ref, sem):
    def row_copy(a):
        r = a // TOP_K
        k = a % TOP_K
        return pltpu.make_async_copy(ys_ref.at[pl.ds(pos_ref[a], 1), :], buf_ref.at[k, pl.ds(r, 1), :], sem)

    def issue(a, carry):
        row_copy(a).start()
        return carry

    def drain(a, carry):
        row_copy(a).wait()
        return carry

    lax.fori_loop(0, TD * TOP_K, issue, 0)
    lax.fori_loop(0, TD * TOP_K, drain, 0)
    g = gates_ref[...]
    acc = x1_ref[...]
    for k in range(TOP_K):
        acc = acc + g[:, k:k + 1] * buf_ref[k]
    o_ref[...] = acc


def _combine(x1, gates, pos, ys):
    n = x1.shape[0]
    return pl.pallas_call(
        _combine_kernel,
        grid=(n // TD,),
        in_specs=[
            pl.BlockSpec((TD * TOP_K,), lambda i: (i,), memory_space=pltpu.SMEM),
            pl.BlockSpec((TD, D_MODEL), lambda i: (i, 0)),
            pl.BlockSpec((TD, LANES), lambda i: (i, 0)),
            pl.BlockSpec(memory_space=pl.ANY),
        ],
        out_specs=pl.BlockSpec((TD, D_MODEL), lambda i: (i, 0)),
        out_shape=jax.ShapeDtypeStruct((n, D_MODEL), F32),
        scratch_shapes=[pltpu.VMEM((TOP_K, TD, D_MODEL), F32), pltpu.SemaphoreType.DMA(())],
        compiler_params=_cparams(("arbitrary",)),
        name="combine",
    )(pos, x1, gates, ys)


def _block_diag(w):
    n, r, _ = w.shape
    eye = jnp.eye(n, dtype=w.dtype)
    return (eye[:, None, :, None] * w[:, :, None, :]).reshape(n * r, n * r)


def kernel(x, norm1_g, w_in, conv_w, conv_b, lru_wa, lru_ba, lru_wx, lru_bx, lru_lambda, q_norm_g, k_norm_g,
           lambda_q1, lambda_k1, lambda_q2, lambda_k2, subln_g, w_out, norm2_g, router_w, router_b, w1, b1, w2, b2):
    bsz, seq, d = x.shape
    n_tok = bsz * seq
    assert d == D_MODEL and n_tok % TM_PROJ == 0 and seq % T_SCAN == 0 and seq % TQ == 0 and n_tok % TD == 0
    assert norm1_g.shape[0] == 1, "single-layer stack"
    x2 = x.reshape(n_tok, d)

    z = _in_proj(x2, norm1_g[0][None, :], w_in[0].astype(BF16))
    z3 = z.reshape(bsz, seq, D_IN)

    wg = jnp.concatenate([_block_diag(lru_wa[0]), _block_diag(lru_wx[0])], axis=1).astype(BF16)
    bg = jnp.concatenate([lru_ba[0], lru_bx[0]])[None, :]
    y_rnn = _rnn(z3, conv_w[0], conv_b[0][None, :], wg, bg, lru_lambda[0][None, :])

    half = jnp.arange(LANES) // HEAD_DIM
    ones_bd = (half[:, None] == half[None, :]).astype(BF16)
    y_attn = _attn(z3, jnp.tile(q_norm_g[0], 2)[None, :], jnp.tile(k_norm_g[0], 2)[None, :],
                   lambda_q1[0][None, :], lambda_k1[0][None, :], lambda_q2[0][None, :], lambda_k2[0][None, :],
                   subln_g[0][None, :], ones_bd)

    rw = jnp.pad(router_w[0], ((0, 0), (0, LANES - N_EXPERTS)))
    rw_hi = rw.astype(BF16)
    rw_lo = (rw - rw_hi.astype(F32)).astype(BF16)
    rb = jnp.pad(router_b[0], (0, LANES - N_EXPERTS))[None, :]
    x1, h2, ids, gates = _out_proj(y_rnn.reshape(n_tok, D_RNN), y_attn.reshape(n_tok, D_ATTN), x2,
                                   w_out[0].astype(BF16).reshape(2, D_RNN, D_MODEL), norm2_g[0][None, :],
                                   rw_hi, rw_lo, rb)

    n_blocks = (n_tok * TOP_K) // BM + N_EXPERTS
    pos, blk_e, n_valid, pad_end, padded = _route(ids[:, :TOP_K], n_blocks)
    xs = _dispatch(h2, pos, pad_end, padded, n_valid, n_blocks * BM)
    ys = _experts(xs, blk_e, n_valid, w1[0], b1[0][:, None, :], w2[0], b2[0][:, None, :])
    out = _combine(x1, gates, pos, ys)
    return out.reshape(bsz, seq, d)
```

```python
import functools
import math

import jax
import jax.numpy as jnp
from jax import lax
from jax.experimental import pallas as pl
from jax.experimental.pallas import tpu as pltpu

F32 = jnp.float32
BF16 = jnp.bfloat16

D_MODEL = 1024
D_RNN = 512
N_RNN_BLOCKS = 8
RNN_BLOCK = 64
CONV_WIDTH = 4
LRU_C = 8.0
HEAD_DIM = 64
N_HEADS = 4
D_ATTN = 512
D_IN = 2 * D_RNN + 3 * D_ATTN
N_EXPERTS = 32
TOP_K = 4
D_FF = 1024
SWIGLU_LIMIT = 7.0
SWIGLU_ALPHA = 1.702
EPS = 1e-5
LAM_INIT = 0.8 - 0.6 * math.exp(0.0)

LANES = 128
SUBLANES = 8
ROW_TILES = D_MODEL // LANES
VMEM_LIMIT = 52 * 1024 * 1024

TM_PROJ = 512
T_SCAN = 256
TQ = 256
BM = 256
TD = 256


def _cparams(sem):
    return pltpu.CompilerParams(dimension_semantics=sem, vmem_limit_bytes=VMEM_LIMIT)


def _to_tile_rows(ref, x):
    rows = x.shape[0]
    for s in range(ROW_TILES):
        ref[pl.ds(s, rows, stride=ROW_TILES), :] = x[:, s * LANES:(s + 1) * LANES]


def _from_tile_rows(ref, rows):
    return jnp.concatenate([ref[pl.ds(s, rows, stride=ROW_TILES), :] for s in range(ROW_TILES)], axis=1)


def _in_proj_kernel(x_ref, g_ref, w_ref, z_ref):
    x = x_ref[...]
    ms = jnp.mean(x * x, axis=-1, keepdims=True)
    h = x * lax.rsqrt(ms + EPS) * g_ref[...]
    z_ref[...] = jnp.dot(h.astype(BF16), w_ref[...], preferred_element_type=F32)


def _in_proj(x2, g, w_bf):
    n = x2.shape[0]
    return pl.pallas_call(
        _in_proj_kernel,
        grid=(n // TM_PROJ,),
        in_specs=[
            pl.BlockSpec((TM_PROJ, D_MODEL), lambda i: (i, 0)),
            pl.BlockSpec((1, D_MODEL), lambda i: (0, 0)),
            pl.BlockSpec((D_MODEL, D_IN), lambda i: (0, 0)),
        ],
        out_specs=pl.BlockSpec((TM_PROJ, D_IN), lambda i: (i, 0)),
        out_shape=jax.ShapeDtypeStruct((n, D_IN), F32),
        compiler_params=_cparams(("arbitrary",)),
        name="in_proj",
    )(x2, g, w_bf)


def _rnn_kernel(xr_ref, gr_ref, cw_ref, cb_ref, wg_ref, bg_ref, lam_ref, y_ref):
    seq = xr_ref.shape[1]
    n_chunks = seq // T_SCAN
    cw = cw_ref[...]
    cb = cb_ref[...]
    lam = lam_ref[...]
    nl = -lam
    softplus_neg_lam = jnp.maximum(nl, 0.0) + jnp.log(1.0 + jnp.exp(-jnp.abs(nl)))
    row = lax.broadcasted_iota(jnp.int32, (T_SCAN, D_RNN), 0)

    def chunk(c, h_prev):
        t0 = pl.multiple_of(c * T_SCAN, T_SCAN)
        cur = xr_ref[0, pl.ds(t0, T_SCAN), :]
        p0 = pl.multiple_of(jnp.maximum(t0 - 8, 0), 8)
        prev = xr_ref[0, pl.ds(p0, 8), :]
        prev = jnp.where(c > 0, prev, 0.0)
        xc = jnp.concatenate([prev, cur], axis=0)
        conv = cb + cw[3:4, :] * cur
        for j in range(1, CONV_WIDTH):
            sh = pltpu.roll(xc, j, axis=0)[8:, :]
            conv = conv + cw[3 - j:4 - j, :] * sh
        gates = jnp.dot(conv.astype(BF16), wg_ref[...], preferred_element_type=F32) + bg_ref[...]
        r = jax.nn.sigmoid(gates[:, :D_RNN])
        i = jax.nn.sigmoid(gates[:, D_RNN:])
        log_a = -LRU_C * r * softplus_neg_lam
        a = jnp.exp(log_a)
        u = jnp.sqrt(1.0 - a * a) * (i * conv)
        d = 1
        while d < T_SCAN:
            a_sh = pltpu.roll(a, d, axis=0)
            u_sh = pltpu.roll(u, d, axis=0)
            keep = row >= d
            u = jnp.where(keep, a * u_sh + u, u)
            a = jnp.where(keep, a * a_sh, a)
            d *= 2
        h = a * h_prev + u
        y_ref[0, pl.ds(t0, T_SCAN), :] = h * jax.nn.gelu(gr_ref[0, pl.ds(t0, T_SCAN), :], approximate=True)
        return h[T_SCAN - 1:T_SCAN, :]

    lax.fori_loop(0, n_chunks, chunk, jnp.zeros((1, D_RNN), F32))


def _rnn(z3, conv_w, conv_b, wg_bf, bg, lam):
    bsz, seq, _ = z3.shape
    const = lambda shape: pl.BlockSpec(shape, lambda b: (0,) * len(shape))
    return pl.pallas_call(
        _rnn_kernel,
        grid=(bsz,),
        in_specs=[
            pl.BlockSpec((1, seq, D_RNN), lambda b: (b, 0, 0)),
            pl.BlockSpec((1, seq, D_RNN), lambda b: (b, 0, 1)),
            const((CONV_WIDTH, D_RNN)),
            const((1, D_RNN)),
            const((D_RNN, 2 * D_RNN)),
            const((1, 2 * D_RNN)),
            const((1, D_RNN)),
        ],
        out_specs=pl.BlockSpec((1, seq, D_RNN), lambda b: (b, 0, 0)),
        out_shape=jax.ShapeDtypeStruct((bsz, seq, D_RNN), F32),
        compiler_params=_cparams(("arbitrary",)),
        name="rnn",
    )(z3, z3, conv_w, conv_b, wg_bf, bg, lam)


def _group_rms(x, ones_bd):
    x2 = x * x
    hi = x2.astype(BF16)
    lo = (x2 - hi.astype(F32)).astype(BF16)
    ssq = jnp.dot(hi, ones_bd, preferred_element_type=F32) + jnp.dot(lo, ones_bd, preferred_element_type=F32)
    return x * lax.rsqrt(ssq * (1.0 / HEAD_DIM) + EPS)


def _attn_kernel(q_ref, k_ref, v_ref, qg_ref, kg_ref, lq1_ref, lk1_ref, lq2_ref, lk2_ref, sg_ref, ones_ref, o_ref):
    seq = q_ref.shape[1]
    ones_bd = ones_ref[...]
    lam = (jnp.exp(jnp.sum(lq1_ref[...] * lk1_ref[...], axis=-1, keepdims=True))
           - jnp.exp(jnp.sum(lq2_ref[...] * lk2_ref[...], axis=-1, keepdims=True)) + LAM_INIT)
    qn = _group_rms(q_ref[0], ones_bd) * qg_ref[...] * (HEAD_DIM ** -0.5)
    kn = _group_rms(k_ref[0], ones_bd) * kg_ref[...]
    lane = lax.broadcasted_iota(jnp.int32, (seq, LANES), 1)
    q1 = jnp.where(lane < HEAD_DIM, qn, 0.0).astype(BF16)
    q2 = jnp.where(lane >= HEAD_DIM, qn, 0.0).astype(BF16)
    kb = kn.astype(BF16)
    vb = v_ref[0].astype(BF16)
    sg = sg_ref[...]
    dn = (((1,), (1,)), ((), ()))
    for qi in range(seq // TQ):
        kv = (qi + 1) * TQ
        rows = slice(qi * TQ, kv)
        qpos = qi * TQ + lax.broadcasted_iota(jnp.int32, (TQ, kv), 0)
        kpos = lax.broadcasted_iota(jnp.int32, (TQ, kv), 1)
        causal = kpos <= qpos

        def probs(qm):
            s = lax.dot_general(qm[rows], kb[:kv], dn, preferred_element_type=F32)
            s = jnp.where(causal, s, -jnp.inf)
            e = jnp.exp(s - jnp.max(s, axis=-1, keepdims=True))
            return e, jnp.sum(e, axis=-1, keepdims=True)

        e1, l1 = probs(q1)
        e2, l2 = probs(q2)
        w = e1 * (1.0 / l1) - e2 * (lam / l2)
        o = jnp.dot(w.astype(BF16), vb[:kv], preferred_element_type=F32)
        o = o * lax.rsqrt(jnp.mean(o * o, axis=-1, keepdims=True) + EPS) * sg * (1.0 - LAM_INIT)
        o_ref[0, rows, :] = o


def _attn(z3, qg2, kg2, lq1, lk1, lq2, lk2, sg, ones_bd):
    bsz, seq, _ = z3.shape
    qoff = 2 * D_RNN // LANES
    koff = qoff + D_ATTN // LANES
    voff = koff + D_ATTN // LANES
    const = lambda shape: pl.BlockSpec(shape, lambda b, h: (0,) * len(shape))
    return pl.pallas_call(
        _attn_kernel,
        grid=(bsz, N_HEADS),
        in_specs=[
            pl.BlockSpec((1, seq, LANES), lambda b, h: (b, 0, qoff + h)),
            pl.BlockSpec((1, seq, LANES), lambda b, h: (b, 0, koff + h)),
            pl.BlockSpec((1, seq, LANES), lambda b, h: (b, 0, voff + h)),
            const((1, LANES)), const((1, LANES)),
            const((1, HEAD_DIM)), const((1, HEAD_DIM)), const((1, HEAD_DIM)), const((1, HEAD_DIM)),
            const((1, LANES)), const((LANES, LANES)),
        ],
        out_specs=pl.BlockSpec((1, seq, LANES), lambda b, h: (b, 0, h)),
        out_shape=jax.ShapeDtypeStruct((bsz, seq, D_ATTN), F32),
        compiler_params=_cparams(("arbitrary", "arbitrary")),
        name="attn",
    )(z3, z3, z3, qg2, kg2, lq1, lk1, lq2, lk2, sg, ones_bd)


def _out_proj_kernel(yr_ref, ya_ref, x_ref, wo_ref, g_ref, rwh_ref, rwl_ref, rb_ref, tri_ref,
                     x1_ref, h2_ref, route_ref, gates_ref, counts_ref, run_ref):
    step = pl.program_id(0)

    @pl.when(step == 0)
    def _():
        run_ref[...] = jnp.zeros_like(run_ref)

    acc = jnp.dot(yr_ref[...].astype(BF16), wo_ref[0], preferred_element_type=F32)
    acc = acc + jnp.dot(ya_ref[...].astype(BF16), wo_ref[1], preferred_element_type=F32)
    x1 = x_ref[...] + acc
    x1_ref[...] = x1
    h2 = x1 * lax.rsqrt(jnp.mean(x1 * x1, axis=-1, keepdims=True) + EPS) * g_ref[...]
    _to_tile_rows(h2_ref, h2)
    hh = h2.astype(BF16)
    hl = (h2 - hh.astype(F32)).astype(BF16)
    logits = (jnp.dot(hh, rwh_ref[...], preferred_element_type=F32)
              + jnp.dot(hl, rwh_ref[...], preferred_element_type=F32)
              + jnp.dot(hh, rwl_ref[...], preferred_element_type=F32)) + rb_ref[...]
    tm = logits.shape[0]
    l = logits.T[:N_EXPERTS, :]
    eid = lax.broadcasted_iota(jnp.int32, (N_EXPERTS, tm), 0)
    vals, idxs = [], []
    for _ in range(TOP_K):
        m = jnp.max(l, axis=0, keepdims=True)
        idx = jnp.min(jnp.where(l == m, eid, N_EXPERTS), axis=0, keepdims=True)
        vals.append(m)
        idxs.append(idx)
        l = jnp.where(eid == idx, -jnp.inf, l)
    es = [jnp.exp(v - vals[0]) for v in vals]
    inv = 1.0 / (es[0] + es[1] + es[2] + es[3])
    chosen = jnp.zeros((N_EXPERTS, tm), F32)
    for k in range(TOP_K):
        chosen = chosen + (eid == idxs[k]).astype(F32)
    before = jnp.dot(chosen.astype(BF16), tri_ref[...], preferred_element_type=F32) + run_ref[:, 0:1]
    sub = lax.broadcasted_iota(jnp.int32, (SUBLANES, tm), 0)
    route = jnp.zeros((SUBLANES, tm), jnp.int32)
    gates = jnp.zeros((SUBLANES, tm), F32)
    for k in range(TOP_K):
        rank = jnp.sum(jnp.where(eid == idxs[k], before, 0.0), axis=0, keepdims=True).astype(jnp.int32)
        route = jnp.where(sub == k, idxs[k], route)
        route = jnp.where(sub == TOP_K + k, rank, route)
        gates = jnp.where(sub == k, es[k] * inv, gates)
    route_ref[...] = route
    gates_ref[...] = gates
    run = run_ref[...] + jnp.sum(chosen, axis=1, keepdims=True)
    run_ref[...] = run
    counts_ref[...] = run.astype(jnp.int32)


def _out_proj(y_rnn, y_attn, x2, wo_bf, g2, rw_hi, rw_lo, rb, tri):
    n = x2.shape[0]
    row = lambda w: pl.BlockSpec((TM_PROJ, w), lambda i: (i, 0))
    col = pl.BlockSpec((SUBLANES, TM_PROJ), lambda i: (0, i))
    const = lambda shape: pl.BlockSpec(shape, lambda i: (0,) * len(shape))
    return pl.pallas_call(
        _out_proj_kernel,
        grid=(n // TM_PROJ,),
        in_specs=[row(D_RNN), row(D_ATTN), row(D_MODEL),
                  const((2, D_RNN, D_MODEL)), const((1, D_MODEL)),
                  const((D_MODEL, LANES)), const((D_MODEL, LANES)), const((1, LANES)),
                  const((TM_PROJ, TM_PROJ))],
        out_specs=[row(D_MODEL), pl.BlockSpec((TM_PROJ * ROW_TILES, LANES), lambda i: (i, 0)), col, col,
                   const((N_EXPERTS, LANES))],
        out_shape=[jax.ShapeDtypeStruct((n, D_MODEL), F32),
                   jax.ShapeDtypeStruct((n * ROW_TILES, LANES), F32),
                   jax.ShapeDtypeStruct((SUBLANES, n), jnp.int32),
                   jax.ShapeDtypeStruct((SUBLANES, n), F32),
                   jax.ShapeDtypeStruct((N_EXPERTS, LANES), jnp.int32)],
        scratch_shapes=[pltpu.VMEM((N_EXPERTS, LANES), F32)],
        compiler_params=_cparams(("arbitrary",)),
        name="out_proj",
    )(y_rnn, y_attn, x2, wo_bf, g2, rw_hi, rw_lo, rb, tri)


def _plan(route, counts, n_blocks):
    padded = (counts + BM - 1) // BM * BM
    pad_end = jnp.cumsum(padded).astype(jnp.int32)
    pad_start = pad_end - padded
    ids, rank = route[:TOP_K], route[TOP_K:]
    experts = jnp.arange(N_EXPERTS, dtype=jnp.int32)
    start_of = jnp.sum(jnp.where(ids[..., None] == experts, pad_start, 0), axis=-1)
    pos = (start_of + rank).astype(jnp.int32)
    blk_start = jnp.arange(n_blocks, dtype=jnp.int32) * BM
    blk_e = jnp.minimum(jnp.sum((pad_end[None, :] <= blk_start[:, None]).astype(jnp.int32), axis=1), N_EXPERTS - 1)
    n_valid = (pad_end[-1] // BM).reshape(1)
    return pos, blk_e, n_valid, pad_end, padded


def _dispatch_kernel(pad_end_ref, padded_ref, n_valid_ref, pos_ref, h_ref, xs_ref, zeros_ref, sem, zsem):
    i = pl.program_id(0)
    blk_rows = BM * ROW_TILES
    n_blocks = xs_ref.shape[0] // blk_rows

    @pl.when(i == 0)
    def _():
        zeros_ref[...] = jnp.zeros_like(zeros_ref)

        def zero_block(blk):
            start = pl.multiple_of(blk * blk_rows, blk_rows)
            cp = pltpu.make_async_copy(zeros_ref, xs_ref.at[pl.ds(start, blk_rows), :], zsem)
            cp.start()
            cp.wait()

        def zero_tail(e, carry):
            @pl.when(padded_ref[e] > 0)
            def _():
                zero_block(pad_end_ref[e] // BM - 1)
            return carry

        def zero_dead(b, carry):
            zero_block(b)
            return carry

        lax.fori_loop(0, N_EXPERTS, zero_tail, 0)
        lax.fori_loop(n_valid_ref[0], n_blocks, zero_dead, 0)

    def row_copy(r, k):
        src = h_ref.at[pl.ds(pl.multiple_of(r * ROW_TILES, ROW_TILES), ROW_TILES), :]
        dst = xs_ref.at[pl.ds(pl.multiple_of(pos_ref[k, r] * ROW_TILES, ROW_TILES), ROW_TILES), :]
        return pltpu.make_async_copy(src, dst, sem)

    def issue(r, carry):
        for k in range(TOP_K):
            row_copy(r, k).start()
        return carry

    def drain(r, carry):
        for k in range(TOP_K):
            row_copy(r, k).wait()
        return carry

    lax.fori_loop(0, TD, issue, 0)
    lax.fori_loop(0, TD, drain, 0)


def _dispatch(h2t, pos, pad_end, padded, n_valid, n_rows):
    n = h2t.shape[0] // ROW_TILES
    grid_spec = pltpu.PrefetchScalarGridSpec(
        num_scalar_prefetch=3,
        grid=(n // TD,),
        in_specs=[
            pl.BlockSpec((TOP_K, TD), lambda i, *_: (0, i), memory_space=pltpu.SMEM),
            pl.BlockSpec((TD * ROW_TILES, LANES), lambda i, *_: (i, 0)),
        ],
        out_specs=pl.BlockSpec(memory_space=pl.ANY),
        scratch_shapes=[pltpu.VMEM((BM * ROW_TILES, LANES), F32),
                        pltpu.SemaphoreType.DMA(()), pltpu.SemaphoreType.DMA(())],
    )
    return pl.pallas_call(
        _dispatch_kernel,
        grid_spec=grid_spec,
        out_shape=jax.ShapeDtypeStruct((n_rows * ROW_TILES, LANES), F32),
        compiler_params=_cparams(("arbitrary",)),
        name="dispatch",
    )(pad_end, padded, n_valid, pos, h2t)


def _experts_kernel(blk_e_ref, n_valid_ref, xs_ref, w1_ref, b1_ref, w2_ref, b2_ref, y_ref, w1b_ref, w2b_ref):
    i = pl.program_id(0)

    @pl.when(i < n_valid_ref[0])
    def _():
        prev_e = blk_e_ref[jnp.maximum(i - 1, 0)]

        @pl.when((i == 0) | (blk_e_ref[i] != prev_e))
        def _():
            w1b_ref[...] = w1_ref[0].astype(BF16)
            w2b_ref[...] = w2_ref[0].astype(BF16)

        x = _from_tile_rows(xs_ref, BM).astype(BF16)
        hcat = jnp.dot(x, w1b_ref[...], preferred_element_type=F32) + b1_ref[0]
        gate = jnp.minimum(hcat[:, :D_FF], SWIGLU_LIMIT)
        up = jnp.clip(hcat[:, D_FF:], -SWIGLU_LIMIT, SWIGLU_LIMIT)
        act = gate * jax.nn.sigmoid(SWIGLU_ALPHA * gate) * (up + 1.0)
        y = jnp.dot(act.astype(BF16), w2b_ref[...], preferred_element_type=F32) + b2_ref[0]
        _to_tile_rows(y_ref, y)

    @pl.when(i >= n_valid_ref[0])
    def _():
        y_ref[...] = jnp.zeros_like(y_ref)


def _experts(xs, blk_e, n_valid, w1, b1, w2, b2):
    blk_rows = BM * ROW_TILES
    n_blocks = xs.shape[0] // blk_rows
    exp3 = lambda i, be, nv: (be[i], 0, 0)
    grid_spec = pltpu.PrefetchScalarGridSpec(
        num_scalar_prefetch=2,
        grid=(n_blocks,),
        in_specs=[
            pl.BlockSpec((blk_rows, LANES), lambda i, be, nv: (jnp.minimum(i, nv[0] - 1), 0)),
            pl.BlockSpec((1, D_MODEL, 2 * D_FF), exp3),
            pl.BlockSpec((1, 1, 2 * D_FF), exp3),
            pl.BlockSpec((1, D_FF, D_MODEL), exp3),
            pl.BlockSpec((1, 1, D_MODEL), exp3),
        ],
        out_specs=pl.BlockSpec((blk_rows, LANES), lambda i, be, nv: (i, 0)),
        scratch_shapes=[pltpu.VMEM((D_MODEL, 2 * D_FF), BF16), pltpu.VMEM((D_FF, D_MODEL), BF16)],
    )
    return pl.pallas_call(
        _experts_kernel,
        grid_spec=grid_spec,
        out_shape=jax.ShapeDtypeStruct(xs.shape, F32),
        compiler_params=_cparams(("arbitrary",)),
        name="experts",
    )(blk_e, n_valid, xs, w1, b1, w2, b2)


def _combine_kernel(pos_ref, x1_ref, gates_ref, ys_ref, o_ref, buf_ref, sem):
    def row_copy(r, k):
        src = ys_ref.at[pl.ds(pl.multiple_of(pos_ref[k, r] * ROW_TILES, ROW_TILES), ROW_TILES), :]
        dst = buf_ref.at[k, pl.ds(pl.multiple_of(r * ROW_TILES, ROW_TILES), ROW_TILES), :]
        return pltpu.make_async_copy(src, dst, sem)

    def issue(r, carry):
        for k in range(TOP_K):
            row_copy(r, k).start()
        return carry

    def drain(r, carry):
        for k in range(TOP_K):
            row_copy(r, k).wait()
        return carry

    lax.fori_loop(0, TD, issue, 0)
    lax.fori_loop(0, TD, drain, 0)
    g = gates_ref[...].T
    acc = x1_ref[...]
    for k in range(TOP_K):
        acc = acc + g[:, k:k + 1] * _from_tile_rows(buf_ref.at[k], TD)
    o_ref[...] = acc


def _combine(x1, gates, pos, ys):
    n = x1.shape[0]
    return pl.pallas_call(
        _combine_kernel,
        grid=(n // TD,),
        in_specs=[
            pl.BlockSpec((TOP_K, TD), lambda i: (0, i), memory_space=pltpu.SMEM),
            pl.BlockSpec((TD, D_MODEL), lambda i: (i, 0)),
            pl.BlockSpec((SUBLANES, TD), lambda i: (0, i)),
            pl.BlockSpec(memory_space=pl.ANY),
        ],
        out_specs=pl.BlockSpec((TD, D_MODEL), lambda i: (i, 0)),
        out_shape=jax.ShapeDtypeStruct((n, D_MODEL), F32),
        scratch_shapes=[pltpu.VMEM((TOP_K, TD * ROW_TILES, LANES), F32), pltpu.SemaphoreType.DMA(())],
        compiler_params=_cparams(("arbitrary",)),
        name="combine",
    )(pos, x1, gates, ys)


def _block_diag(w):
    n, r, _ = w.shape
    eye = jnp.eye(n, dtype=w.dtype)
    return (eye[:, None, :, None] * w[:, :, None, :]).reshape(n * r, n * r)


def kernel(x, norm1_g, w_in, conv_w, conv_b, lru_wa, lru_ba, lru_wx, lru_bx, lru_lambda, q_norm_g, k_norm_g,
           lambda_q1, lambda_k1, lambda_q2, lambda_k2, subln_g, w_out, norm2_g, router_w, router_b, w1, b1, w2, b2):
    bsz, seq, d = x.shape
    n_tok = bsz * seq
    assert d == D_MODEL and n_tok % TM_PROJ == 0 and seq % T_SCAN == 0 and seq % TQ == 0 and n_tok % TD == 0
    assert (n_tok * TOP_K) % BM == 0
    assert norm1_g.shape[0] == 1, "single-layer stack"
    x2 = x.reshape(n_tok, d)

    z = _in_proj(x2, norm1_g[0][None, :], w_in[0].astype(BF16))
    z3 = z.reshape(bsz, seq, D_IN)

    wg = jnp.concatenate([_block_diag(lru_wa[0]), _block_diag(lru_wx[0])], axis=1).astype(BF16)
    bg = jnp.concatenate([lru_ba[0], lru_bx[0]])[None, :]
    y_rnn = _rnn(z3, conv_w[0], conv_b[0][None, :], wg, bg, lru_lambda[0][None, :])

    half = jnp.arange(LANES) // HEAD_DIM
    ones_bd = (half[:, None] == half[None, :]).astype(BF16)
    y_attn = _attn(z3, jnp.tile(q_norm_g[0], 2)[None, :], jnp.tile(k_norm_g[0], 2)[None, :],
                   lambda_q1[0][None, :], lambda_k1[0][None, :], lambda_q2[0][None, :], lambda_k2[0][None, :],
                   subln_g[0][None, :], ones_bd)

    rw = jnp.pad(router_w[0], ((0, 0), (0, LANES - N_EXPERTS)))
    rw_hi = rw.astype(BF16)
    rw_lo = (rw - rw_hi.astype(F32)).astype(BF16)
    rb = jnp.pad(router_b[0], (0, LANES - N_EXPERTS))[None, :]
    tok = jnp.arange(TM_PROJ)
    tri = (tok[:, None] < tok[None, :]).astype(BF16)
    x1, h2t, route, gates, counts = _out_proj(
        y_rnn.reshape(n_tok, D_RNN), y_attn.reshape(n_tok, D_ATTN), x2,
        w_out[0].astype(BF16).reshape(2, D_RNN, D_MODEL), norm2_g[0][None, :], rw_hi, rw_lo, rb, tri)

    n_blocks = (n_tok * TOP_K) // BM + N_EXPERTS
    pos, blk_e, n_valid, pad_end, padded = _plan(route, counts[:, 0], n_blocks)
    xs = _dispatch(h2t, pos, pad_end, padded, n_valid, n_blocks * BM)
    ys = _experts(xs, blk_e, n_valid, w1[0], b1[0][:, None, :], w2[0], b2[0][:, None, :])
    out = _combine(x1, gates, pos, ys)
    return out.reshape(bsz, seq, d)
```

```python
import functools
import math

import jax
import jax.numpy as jnp
from jax import lax
from jax.experimental import pallas as pl
from jax.experimental.pallas import tpu as pltpu

F32 = jnp.float32
BF16 = jnp.bfloat16

D_MODEL = 1024
D_RNN = 512
N_RNN_BLOCKS = 8
RNN_BLOCK = 64
CONV_WIDTH = 4
LRU_C = 8.0
HEAD_DIM = 64
N_HEADS = 4
D_ATTN = 512
D_IN = 2 * D_RNN + 3 * D_ATTN
N_EXPERTS = 32
TOP_K = 4
D_FF = 1024
SWIGLU_LIMIT = 7.0
SWIGLU_ALPHA = 1.702
EPS = 1e-5
LAM_INIT = 0.8 - 0.6 * math.exp(0.0)

LANES = 128
SUBLANES = 8
ROW_TILES = D_MODEL // LANES
N_DMA_PRIORITIES = 2
VMEM_LIMIT = 52 * 1024 * 1024

TM_PROJ = 512
T_SCAN = 256
TQ = 256
BM = 256
TD = 256


def _cparams(sem):
    return pltpu.CompilerParams(dimension_semantics=sem, vmem_limit_bytes=VMEM_LIMIT)


def _to_tile_rows(ref, x):
    rows = x.shape[0]
    for s in range(ROW_TILES):
        ref[pl.ds(s, rows, stride=ROW_TILES), :] = x[:, s * LANES:(s + 1) * LANES]


def _from_tile_rows(ref, rows):
    return jnp.concatenate([ref[pl.ds(s, rows, stride=ROW_TILES), :] for s in range(ROW_TILES)], axis=1)


def _in_proj_kernel(x_ref, g_ref, w_ref, z_ref):
    x = x_ref[...]
    ms = jnp.mean(x * x, axis=-1, keepdims=True)
    h = x * lax.rsqrt(ms + EPS) * g_ref[...]
    z_ref[...] = jnp.dot(h.astype(BF16), w_ref[...], preferred_element_type=F32)


def _in_proj(x2, g, w_bf):
    n = x2.shape[0]
    return pl.pallas_call(
        _in_proj_kernel,
        grid=(n // TM_PROJ,),
        in_specs=[
            pl.BlockSpec((TM_PROJ, D_MODEL), lambda i: (i, 0)),
            pl.BlockSpec((1, D_MODEL), lambda i: (0, 0)),
            pl.BlockSpec((D_MODEL, D_IN), lambda i: (0, 0)),
        ],
        out_specs=pl.BlockSpec((TM_PROJ, D_IN), lambda i: (i, 0)),
        out_shape=jax.ShapeDtypeStruct((n, D_IN), F32),
        compiler_params=_cparams(("arbitrary",)),
        name="in_proj",
    )(x2, g, w_bf)


def _rnn_kernel(xr_ref, gr_ref, cw_ref, cb_ref, wg_ref, bg_ref, lam_ref, y_ref, a_ref, u_ref, hin_ref):
    seq = xr_ref.shape[1]
    n_chunks = seq // T_SCAN
    n_groups = T_SCAN // SUBLANES
    cw = cw_ref[...]
    cb = cb_ref[...]
    lam = lam_ref[...]
    nl = -lam
    softplus_neg_lam = jnp.maximum(nl, 0.0) + jnp.log(1.0 + jnp.exp(-jnp.abs(nl)))
    in_group = lax.broadcasted_iota(jnp.int32, (T_SCAN, D_RNN), 0) % SUBLANES
    group = lax.broadcasted_iota(jnp.int32, (n_groups, D_RNN), 0)
    n_slabs = D_RNN // LANES

    def last_row_of_groups(ref, x):
        for c in range(n_slabs):
            ref[c] = x[:, c * LANES:(c + 1) * LANES]
        return jnp.concatenate([ref[c, pl.ds(SUBLANES - 1, n_groups, stride=SUBLANES), :] for c in range(n_slabs)],
                               axis=1)

    def chunk(c, h_prev):
        t0 = pl.multiple_of(c * T_SCAN, T_SCAN)
        cur = xr_ref[0, pl.ds(t0, T_SCAN), :]
        p0 = pl.multiple_of(jnp.maximum(t0 - 8, 0), 8)
        prev = xr_ref[0, pl.ds(p0, 8), :]
        prev = jnp.where(c > 0, prev, 0.0)
        xc = jnp.concatenate([prev, cur], axis=0)
        conv = cb + cw[3:4, :] * cur
        for j in range(1, CONV_WIDTH):
            sh = pltpu.roll(xc, j, axis=0)[8:, :]
            conv = conv + cw[3 - j:4 - j, :] * sh
        gates = jnp.dot(conv.astype(BF16), wg_ref[...], preferred_element_type=F32) + bg_ref[...]
        r = jax.nn.sigmoid(gates[:, :D_RNN])
        i = jax.nn.sigmoid(gates[:, D_RNN:])
        log_a = -LRU_C * r * softplus_neg_lam
        a = jnp.exp(log_a)
        var = 1.0 - a * a
        u = jnp.where(var > 0.0, var * lax.rsqrt(var), 0.0) * (i * conv)
        d = 1
        while d < SUBLANES:
            a_sh = pltpu.roll(a, d, axis=0)
            u_sh = pltpu.roll(u, d, axis=0)
            keep = in_group >= d
            u = jnp.where(keep, a * u_sh + u, u)
            a = jnp.where(keep, a * a_sh, a)
            d *= 2
        ga = last_row_of_groups(a_ref, a)
        gu = last_row_of_groups(u_ref, u)
        d = 1
        while d < n_groups:
            ga_sh = pltpu.roll(ga, d, axis=0)
            gu_sh = pltpu.roll(gu, d, axis=0)
            keep = group >= d
            gu = jnp.where(keep, ga * gu_sh + gu, gu)
            ga = jnp.where(keep, ga * ga_sh, ga)
            d *= 2
        h_after = ga * h_prev + gu
        h_before = jnp.where(group >= 1, pltpu.roll(h_after, 1, axis=0), h_prev)
        for c in range(n_slabs):
            for s in range(SUBLANES):
                hin_ref[c, pl.ds(s, n_groups, stride=SUBLANES), :] = h_before[:, c * LANES:(c + 1) * LANES]
        h = a * jnp.concatenate([hin_ref[c] for c in range(n_slabs)], axis=1) + u
        y_ref[0, pl.ds(t0, T_SCAN), :] = h * jax.nn.gelu(gr_ref[0, pl.ds(t0, T_SCAN), :], approximate=True)
        return h_after[n_groups - 1:n_groups, :]

    lax.fori_loop(0, n_chunks, chunk, jnp.zeros((1, D_RNN), F32))


def _rnn(z3, conv_w, conv_b, wg_bf, bg, lam):
    bsz, seq, _ = z3.shape
    const = lambda shape: pl.BlockSpec(shape, lambda b: (0,) * len(shape))
    return pl.pallas_call(
        _rnn_kernel,
        grid=(bsz,),
        in_specs=[
            pl.BlockSpec((1, seq, D_RNN), lambda b: (b, 0, 0)),
            pl.BlockSpec((1, seq, D_RNN), lambda b: (b, 0, 1)),
            const((CONV_WIDTH, D_RNN)),
            const((1, D_RNN)),
            const((D_RNN, 2 * D_RNN)),
            const((1, 2 * D_RNN)),
            const((1, D_RNN)),
        ],
        out_specs=pl.BlockSpec((1, seq, D_RNN), lambda b: (b, 0, 0)),
        out_shape=jax.ShapeDtypeStruct((bsz, seq, D_RNN), F32),
        scratch_shapes=[pltpu.VMEM((D_RNN // LANES, T_SCAN, LANES), F32)] * 3,
        compiler_params=_cparams(("arbitrary",)),
        name="rnn",
    )(z3, z3, conv_w, conv_b, wg_bf, bg, lam)


def _group_rms(x, ones_bd):
    x2 = x * x
    hi = x2.astype(BF16)
    lo = (x2 - hi.astype(F32)).astype(BF16)
    ssq = jnp.dot(hi, ones_bd, preferred_element_type=F32) + jnp.dot(lo, ones_bd, preferred_element_type=F32)
    return x * lax.rsqrt(ssq * (1.0 / HEAD_DIM) + EPS)


def _attn_kernel(q_ref, k_ref, v_ref, qg_ref, kg_ref, lq1_ref, lk1_ref, lq2_ref, lk2_ref, sg_ref, ones_ref, o_ref):
    seq = q_ref.shape[1]
    ones_bd = ones_ref[...]
    lam = (jnp.exp(jnp.sum(lq1_ref[...] * lk1_ref[...], axis=-1, keepdims=True))
           - jnp.exp(jnp.sum(lq2_ref[...] * lk2_ref[...], axis=-1, keepdims=True)) + LAM_INIT)
    qn = _group_rms(q_ref[0], ones_bd) * qg_ref[...] * (HEAD_DIM ** -0.5)
    kn = _group_rms(k_ref[0], ones_bd) * kg_ref[...]
    lane = lax.broadcasted_iota(jnp.int32, (seq, LANES), 1)
    q1 = jnp.where(lane < HEAD_DIM, qn, 0.0).astype(BF16)
    q2 = jnp.where(lane >= HEAD_DIM, qn, 0.0).astype(BF16)
    kb = kn.astype(BF16)
    vb = v_ref[0].astype(BF16)
    sg = sg_ref[...]
    dn = (((1,), (1,)), ((), ()))
    for qi in range(seq // TQ):
        kv = (qi + 1) * TQ
        rows = slice(qi * TQ, kv)
        qpos = qi * TQ + lax.broadcasted_iota(jnp.int32, (TQ, kv), 0)
        kpos = lax.broadcasted_iota(jnp.int32, (TQ, kv), 1)
        causal = kpos <= qpos

        def probs(qm):
            s = lax.dot_general(qm[rows], kb[:kv], dn, preferred_element_type=F32)
            s = jnp.where(causal, s, -jnp.inf)
            e = jnp.exp(s - jnp.max(s, axis=-1, keepdims=True))
            return e, jnp.sum(e, axis=-1, keepdims=True)

        e1, l1 = probs(q1)
        e2, l2 = probs(q2)
        w = e1 * (1.0 / l1) - e2 * (lam / l2)
        o = jnp.dot(w.astype(BF16), vb[:kv], preferred_element_type=F32)
        o = o * lax.rsqrt(jnp.mean(o * o, axis=-1, keepdims=True) + EPS) * sg * (1.0 - LAM_INIT)
        o_ref[0, rows, :] = o


def _attn(z3, qg2, kg2, lq1, lk1, lq2, lk2, sg, ones_bd):
    bsz, seq, _ = z3.shape
    qoff = 2 * D_RNN // LANES
    koff = qoff + D_ATTN // LANES
    voff = koff + D_ATTN // LANES
    const = lambda shape: pl.BlockSpec(shape, lambda b, h: (0,) * len(shape))
    return pl.pallas_call(
        _attn_kernel,
        grid=(bsz, N_HEADS),
        in_specs=[
            pl.BlockSpec((1, seq, LANES), lambda b, h: (b, 0, qoff + h)),
            pl.BlockSpec((1, seq, LANES), lambda b, h: (b, 0, koff + h)),
            pl.BlockSpec((1, seq, LANES), lambda b, h: (b, 0, voff + h)),
            const((1, LANES)), const((1, LANES)),
            const((1, HEAD_DIM)), const((1, HEAD_DIM)), const((1, HEAD_DIM)), const((1, HEAD_DIM)),
            const((1, LANES)), const((LANES, LANES)),
        ],
        out_specs=pl.BlockSpec((1, seq, LANES), lambda b, h: (b, 0, h)),
        out_shape=jax.ShapeDtypeStruct((bsz, seq, D_ATTN), F32),
        compiler_params=_cparams(("arbitrary", "arbitrary")),
        name="attn",
    )(z3, z3, z3, qg2, kg2, lq1, lk1, lq2, lk2, sg, ones_bd)


def _out_proj_kernel(yr_ref, ya_ref, x_ref, wo_ref, g_ref, rwh_ref, rwl_ref, rb_ref, tri_ref,
                     x1_ref, h2_ref, route_ref, gates_ref, counts_ref, run_ref):
    step = pl.program_id(0)

    @pl.when(step == 0)
    def _():
        run_ref[...] = jnp.zeros_like(run_ref)

    acc = jnp.dot(yr_ref[...].astype(BF16), wo_ref[0], preferred_element_type=F32)
    acc = acc + jnp.dot(ya_ref[...].astype(BF16), wo_ref[1], preferred_element_type=F32)
    x1 = x_ref[...] + acc
    x1_ref[...] = x1
    h2 = x1 * lax.rsqrt(jnp.mean(x1 * x1, axis=-1, keepdims=True) + EPS) * g_ref[...]
    _to_tile_rows(h2_ref, h2)
    hh = h2.astype(BF16)
    hl = (h2 - hh.astype(F32)).astype(BF16)
    logits = (jnp.dot(hh, rwh_ref[...], preferred_element_type=F32)
              + jnp.dot(hl, rwh_ref[...], preferred_element_type=F32)
              + jnp.dot(hh, rwl_ref[...], preferred_element_type=F32)) + rb_ref[...]
    tm = logits.shape[0]
    l = logits.T[:N_EXPERTS, :]
    eid = lax.broadcasted_iota(jnp.int32, (N_EXPERTS, tm), 0)
    vals, idxs = [], []
    for _ in range(TOP_K):
        m = jnp.max(l, axis=0, keepdims=True)
        idx = jnp.min(jnp.where(l == m, eid, N_EXPERTS), axis=0, keepdims=True)
        vals.append(m)
        idxs.append(idx)
        l = jnp.where(eid == idx, -jnp.inf, l)
    es = [jnp.exp(v - vals[0]) for v in vals]
    inv = 1.0 / (es[0] + es[1] + es[2] + es[3])
    chosen = jnp.zeros((N_EXPERTS, tm), F32)
    for k in range(TOP_K):
        chosen = chosen + (eid == idxs[k]).astype(F32)
    before = jnp.dot(chosen.astype(BF16), tri_ref[...], preferred_element_type=F32) + run_ref[:, 0:1]
    sub = lax.broadcasted_iota(jnp.int32, (SUBLANES, tm), 0)
    route = jnp.zeros((SUBLANES, tm), jnp.int32)
    gates = jnp.zeros((SUBLANES, tm), F32)
    for k in range(TOP_K):
        rank = jnp.sum(jnp.where(eid == idxs[k], before, 0.0), axis=0, keepdims=True).astype(jnp.int32)
        route = jnp.where(sub == k, idxs[k], route)
        route = jnp.where(sub == TOP_K + k, rank, route)
        gates = jnp.where(sub == k, es[k] * inv, gates)
    route_ref[...] = route
    gates_ref[...] = gates
    run = run_ref[...] + jnp.sum(chosen, axis=1, keepdims=True)
    run_ref[...] = run
    counts_ref[...] = run.astype(jnp.int32)


def _out_proj(y_rnn, y_attn, x2, wo_bf, g2, rw_hi, rw_lo, rb, tri):
    n = x2.shape[0]
    row = lambda w: pl.BlockSpec((TM_PROJ, w), lambda i: (i, 0))
    col = pl.BlockSpec((SUBLANES, TM_PROJ), lambda i: (0, i))
    const = lambda shape: pl.BlockSpec(shape, lambda i: (0,) * len(shape))
    return pl.pallas_call(
        _out_proj_kernel,
        grid=(n // TM_PROJ,),
        in_specs=[row(D_RNN), row(D_ATTN), row(D_MODEL),
                  const((2, D_RNN, D_MODEL)), const((1, D_MODEL)),
                  const((D_MODEL, LANES)), const((D_MODEL, LANES)), const((1, LANES)),
                  const((TM_PROJ, TM_PROJ))],
        out_specs=[row(D_MODEL), pl.BlockSpec((TM_PROJ * ROW_TILES, LANES), lambda i: (i, 0)), col, col,
                   const((N_EXPERTS, LANES))],
        out_shape=[jax.ShapeDtypeStruct((n, D_MODEL), F32),
                   jax.ShapeDtypeStruct((n * ROW_TILES, LANES), F32),
                   jax.ShapeDtypeStruct((SUBLANES, n), jnp.int32),
                   jax.ShapeDtypeStruct((SUBLANES, n), F32),
                   jax.ShapeDtypeStruct((N_EXPERTS, LANES), jnp.int32)],
        scratch_shapes=[pltpu.VMEM((N_EXPERTS, LANES), F32)],
        compiler_params=_cparams(("arbitrary",)),
        name="out_proj",
    )(y_rnn, y_attn, x2, wo_bf, g2, rw_hi, rw_lo, rb, tri)


def _plan(route, counts, n_blocks):
    padded = (counts + BM - 1) // BM * BM
    pad_end = jnp.cumsum(padded).astype(jnp.int32)
    pad_start = pad_end - padded
    ids, rank = route[:TOP_K], route[TOP_K:]
    experts = jnp.arange(N_EXPERTS, dtype=jnp.int32)
    start_of = jnp.sum(jnp.where(ids[..., None] == experts, pad_start, 0), axis=-1)
    pos = (start_of + rank).astype(jnp.int32)
    blk_start = jnp.arange(n_blocks, dtype=jnp.int32) * BM
    blk_e = jnp.minimum(jnp.sum((pad_end[None, :] <= blk_start[:, None]).astype(jnp.int32), axis=1), N_EXPERTS - 1)
    n_valid = (pad_end[-1] // BM).reshape(1)
    return pos, blk_e, n_valid, pad_end, padded


def _dispatch_kernel(pad_end_ref, padded_ref, n_valid_ref, pos_ref, h_ref, xs_ref, zeros_ref, sem, zsem):
    i = pl.program_id(0)
    blk_rows = BM * ROW_TILES
    n_blocks = xs_ref.shape[0] // blk_rows

    @pl.when(i == 0)
    def _():
        zeros_ref[...] = jnp.zeros_like(zeros_ref)

        def zero_block(blk):
            start = pl.multiple_of(blk * blk_rows, blk_rows)
            cp = pltpu.make_async_copy(zeros_ref, xs_ref.at[pl.ds(start, blk_rows), :], zsem)
            cp.start()
            cp.wait()

        def zero_tail(e, carry):
            @pl.when(padded_ref[e] > 0)
            def _():
                zero_block(pad_end_ref[e] // BM - 1)
            return carry

        def zero_dead(b, carry):
            zero_block(b)
            return carry

        lax.fori_loop(0, N_EXPERTS, zero_tail, 0)
        lax.fori_loop(n_valid_ref[0], n_blocks, zero_dead, 0)

    def row_copy(r, k):
        src = h_ref.at[pl.ds(pl.multiple_of(r * ROW_TILES, ROW_TILES), ROW_TILES), :]
        dst = xs_ref.at[pl.ds(pl.multiple_of(pos_ref[k, r] * ROW_TILES, ROW_TILES), ROW_TILES), :]
        return pltpu.make_async_copy(src, dst, sem)

    def issue(r, carry):
        for k in range(TOP_K):
            row_copy(r, k).start(priority=k % N_DMA_PRIORITIES)
        return carry

    def drain(r, carry):
        for k in range(TOP_K):
            row_copy(r, k).wait()
        return carry

    lax.fori_loop(0, TD, issue, 0)
    lax.fori_loop(0, TD, drain, 0)


def _dispatch(h2t, pos, pad_end, padded, n_valid, n_rows):
    n = h2t.shape[0] // ROW_TILES
    grid_spec = pltpu.PrefetchScalarGridSpec(
        num_scalar_prefetch=3,
        grid=(n // TD,),
        in_specs=[
            pl.BlockSpec((TOP_K, TD), lambda i, *_: (0, i), memory_space=pltpu.SMEM),
            pl.BlockSpec((TD * ROW_TILES, LANES), lambda i, *_: (i, 0)),
        ],
        out_specs=pl.BlockSpec(memory_space=pl.ANY),
        scratch_shapes=[pltpu.VMEM((BM * ROW_TILES, LANES), F32),
                        pltpu.SemaphoreType.DMA(()), pltpu.SemaphoreType.DMA(())],
    )
    return pl.pallas_call(
        _dispatch_kernel,
        grid_spec=grid_spec,
        out_shape=jax.ShapeDtypeStruct((n_rows * ROW_TILES, LANES), F32),
        compiler_params=_cparams(("arbitrary",)),
        name="dispatch",
    )(pad_end, padded, n_valid, pos, h2t)


def _experts_kernel(blk_e_ref, n_valid_ref, xs_ref, w1_ref, b1_ref, w2_ref, b2_ref, y_ref, w1b_ref, w2b_ref):
    i = pl.program_id(0)

    @pl.when(i < n_valid_ref[0])
    def _():
        prev_e = blk_e_ref[jnp.maximum(i - 1, 0)]

        @pl.when((i == 0) | (blk_e_ref[i] != prev_e))
        def _():
            w1b_ref[...] = w1_ref[0].astype(BF16)
            w2b_ref[...] = w2_ref[0].astype(BF16)

        x = _from_tile_rows(xs_ref, BM).astype(BF16)
        hcat = jnp.dot(x, w1b_ref[...], preferred_element_type=F32) + b1_ref[0]
        gate = jnp.minimum(hcat[:, :D_FF], SWIGLU_LIMIT)
        up = jnp.clip(hcat[:, D_FF:], -SWIGLU_LIMIT, SWIGLU_LIMIT)
        act = gate * jax.nn.sigmoid(SWIGLU_ALPHA * gate) * (up + 1.0)
        y = jnp.dot(act.astype(BF16), w2b_ref[...], preferred_element_type=F32) + b2_ref[0]
        _to_tile_rows(y_ref, y)

    @pl.when(i >= n_valid_ref[0])
    def _():
        y_ref[...] = jnp.zeros_like(y_ref)


def _experts(xs, blk_e, n_valid, w1, b1, w2, b2):
    blk_rows = BM * ROW_TILES
    n_blocks = xs.shape[0] // blk_rows
    exp3 = lambda i, be, nv: (be[i], 0, 0)
    grid_spec = pltpu.PrefetchScalarGridSpec(
        num_scalar_prefetch=2,
        grid=(n_blocks,),
        in_specs=[
            pl.BlockSpec((blk_rows, LANES), lambda i, be, nv: (jnp.minimum(i, nv[0] - 1), 0)),
            pl.BlockSpec((1, D_MODEL, 2 * D_FF), exp3),
            pl.BlockSpec((1, 1, 2 * D_FF), exp3),
            pl.BlockSpec((1, D_FF, D_MODEL), exp3),
            pl.BlockSpec((1, 1, D_MODEL), exp3),
        ],
        out_specs=pl.BlockSpec((blk_rows, LANES), lambda i, be, nv: (i, 0)),
        scratch_shapes=[pltpu.VMEM((D_MODEL, 2 * D_FF), BF16), pltpu.VMEM((D_FF, D_MODEL), BF16)],
    )
    return pl.pallas_call(
        _experts_kernel,
        grid_spec=grid_spec,
        out_shape=jax.ShapeDtypeStruct(xs.shape, F32),
        compiler_params=_cparams(("arbitrary",)),
        name="experts",
    )(blk_e, n_valid, xs, w1, b1, w2, b2)


def _combine_kernel(pos_ref, pos_next_ref, x1_ref, gates_ref, ys_ref, o_ref, buf_ref, sem):
    i = pl.program_id(0)
    n_tiles = pl.num_programs(0)
    slot = i % 2

    def row_copy(p_ref, s, r, k):
        src = ys_ref.at[pl.ds(pl.multiple_of(p_ref[k, r] * ROW_TILES, ROW_TILES), ROW_TILES), :]
        dst = buf_ref.at[s, k, pl.ds(pl.multiple_of(r * ROW_TILES, ROW_TILES), ROW_TILES), :]
        return pltpu.make_async_copy(src, dst, sem.at[s])

    def gather(p_ref, s):
        def issue(r, carry):
            for k in range(TOP_K):
                row_copy(p_ref, s, r, k).start(priority=k % N_DMA_PRIORITIES)
            return carry
        lax.fori_loop(0, TD, issue, 0)

    @pl.when(i == 0)
    def _():
        gather(pos_ref, slot)

    @pl.when(i + 1 < n_tiles)
    def _():
        gather(pos_next_ref, 1 - slot)

    def drain(r, carry):
        for k in range(TOP_K):
            row_copy(pos_ref, slot, r, k).wait()
        return carry

    lax.fori_loop(0, TD, drain, 0)
    g = gates_ref[...].T
    acc = x1_ref[...]
    for k in range(TOP_K):
        acc = acc + g[:, k:k + 1] * _from_tile_rows(buf_ref.at[slot, k], TD)
    o_ref[...] = acc


def _combine(x1, gates, pos, ys):
    n = x1.shape[0]
    n_tiles = n // TD
    return pl.pallas_call(
        _combine_kernel,
        grid=(n_tiles,),
        in_specs=[
            pl.BlockSpec((TOP_K, TD), lambda i: (0, i), memory_space=pltpu.SMEM),
            pl.BlockSpec((TOP_K, TD), lambda i: (0, jnp.minimum(i + 1, n_tiles - 1)), memory_space=pltpu.SMEM),
            pl.BlockSpec((TD, D_MODEL), lambda i: (i, 0)),
            pl.BlockSpec((SUBLANES, TD), lambda i: (0, i)),
            pl.BlockSpec(memory_space=pl.ANY),
        ],
        out_specs=pl.BlockSpec((TD, D_MODEL), lambda i: (i, 0)),
        out_shape=jax.ShapeDtypeStruct((n, D_MODEL), F32),
        scratch_shapes=[pltpu.VMEM((2, TOP_K, TD * ROW_TILES, LANES), F32), pltpu.SemaphoreType.DMA((2,))],
        compiler_params=_cparams(("arbitrary",)),
        name="combine",
    )(pos, pos, x1, gates, ys)


def _block_diag(w):
    n, r, _ = w.shape
    eye = jnp.eye(n, dtype=w.dtype)
    return (eye[:, None, :, None] * w[:, :, None, :]).reshape(n * r, n * r)


def kernel(x, norm1_g, w_in, conv_w, conv_b, lru_wa, lru_ba, lru_wx, lru_bx, lru_lambda, q_norm_g, k_norm_g,
           lambda_q1, lambda_k1, lambda_q2, lambda_k2, subln_g, w_out, norm2_g, router_w, router_b, w1, b1, w2, b2):
    bsz, seq, d = x.shape
    n_tok = bsz * seq
    assert d == D_MODEL and n_tok % TM_PROJ == 0 and seq % T_SCAN == 0 and seq % TQ == 0 and n_tok % TD == 0
    assert (n_tok * TOP_K) % BM == 0
    assert norm1_g.shape[0] == 1, "single-layer stack"
    x2 = x.reshape(n_tok, d)

    z = _in_proj(x2, norm1_g[0][None, :], w_in[0].astype(BF16))
    z3 = z.reshape(bsz, seq, D_IN)

    wg = jnp.concatenate([_block_diag(lru_wa[0]), _block_diag(lru_wx[0])], axis=1).astype(BF16)
    bg = jnp.concatenate([lru_ba[0], lru_bx[0]])[None, :]
    y_rnn = _rnn(z3, conv_w[0], conv_b[0][None, :], wg, bg, lru_lambda[0][None, :])

    half = jnp.arange(LANES) // HEAD_DIM
    ones_bd = (half[:, None] == half[None, :]).astype(BF16)
    y_attn = _attn(z3, jnp.tile(q_norm_g[0], 2)[None, :], jnp.tile(k_norm_g[0], 2)[None, :],
                   lambda_q1[0][None, :], lambda_k1[0][None, :], lambda_q2[0][None, :], lambda_k2[0][None, :],
                   subln_g[0][None, :], ones_bd)

    rw = jnp.pad(router_w[0], ((0, 0), (0, LANES - N_EXPERTS)))
    rw_hi = rw.astype(BF16)
    rw_lo = (rw - rw_hi.astype(F32)).astype(BF16)
    rb = jnp.pad(router_b[0], (0, LANES - N_EXPERTS))[None, :]
    tok = jnp.arange(TM_PROJ)
    tri = (tok[:, None] < tok[None, :]).astype(BF16)
    x1, h2t, route, gates, counts = _out_proj(
        y_rnn.reshape(n_tok, D_RNN), y_attn.reshape(n_tok, D_ATTN), x2,
        w_out[0].astype(BF16).reshape(2, D_RNN, D_MODEL), norm2_g[0][None, :], rw_hi, rw_lo, rb, tri)

    n_blocks = (n_tok * TOP_K) // BM + N_EXPERTS
    pos, blk_e, n_valid, pad_end, padded = _plan(route, counts[:, 0], n_blocks)
    xs = _dispatch(h2t, pos, pad_end, padded, n_valid, n_blocks * BM)
    ys = _experts(xs, blk_e, n_valid, w1[0], b1[0][:, None, :], w2[0], b2[0][:, None, :])
    out = _combine(x1, gates, pos, ys)
    return out.reshape(bsz, seq, d)
```

```python
import functools
import math

import jax
import jax.numpy as jnp
from jax import lax
from jax.experimental import pallas as pl
from jax.experimental.pallas import tpu as pltpu

F32 = jnp.float32
BF16 = jnp.bfloat16

D_MODEL = 1024
D_RNN = 512
N_RNN_BLOCKS = 8
RNN_BLOCK = 64
CONV_WIDTH = 4
LRU_C = 8.0
HEAD_DIM = 64
N_HEADS = 4
D_ATTN = 512
D_IN = 2 * D_RNN + 3 * D_ATTN
N_EXPERTS = 32
TOP_K = 4
D_FF = 1024
SWIGLU_LIMIT = 7.0
SWIGLU_ALPHA = 1.702
EPS = 1e-5
LAM_INIT = 0.8 - 0.6 * math.exp(0.0)

LANES = 128
SUBLANES = 8
ROW_TILES = D_MODEL // LANES
N_DMA_PRIORITIES = 2
VMEM_LIMIT = 52 * 1024 * 1024

TM_PROJ = 512
T_SCAN = 256
TQ = 256
BM = 256
T_TOK = 512
P_ROWS = 256
RUN_BITS = T_TOK.bit_length()


def _cparams(sem):
    return pltpu.CompilerParams(dimension_semantics=sem, vmem_limit_bytes=VMEM_LIMIT)


def _to_tile_rows(ref, x):
    rows = x.shape[0]
    for s in range(ROW_TILES):
        ref[pl.ds(s, rows, stride=ROW_TILES), :] = x[:, s * LANES:(s + 1) * LANES]


def _from_tile_rows(ref, rows):
    return jnp.concatenate([ref[pl.ds(s, rows, stride=ROW_TILES), :] for s in range(ROW_TILES)], axis=1)


def _in_proj_kernel(x_ref, g_ref, w_ref, z_ref):
    x = x_ref[...]
    ms = jnp.mean(x * x, axis=-1, keepdims=True)
    h = x * lax.rsqrt(ms + EPS) * g_ref[...]
    z_ref[...] = jnp.dot(h.astype(BF16), w_ref[...], preferred_element_type=F32)


def _in_proj(x2, g, w_bf):
    n = x2.shape[0]
    return pl.pallas_call(
        _in_proj_kernel,
        grid=(n // TM_PROJ,),
        in_specs=[
            pl.BlockSpec((TM_PROJ, D_MODEL), lambda i: (i, 0)),
            pl.BlockSpec((1, D_MODEL), lambda i: (0, 0)),
            pl.BlockSpec((D_MODEL, D_IN), lambda i: (0, 0)),
        ],
        out_specs=pl.BlockSpec((TM_PROJ, D_IN), lambda i: (i, 0)),
        out_shape=jax.ShapeDtypeStruct((n, D_IN), F32),
        compiler_params=_cparams(("arbitrary",)),
        name="in_proj",
    )(x2, g, w_bf)


def _rnn_kernel(xr_ref, gr_ref, cw_ref, cb_ref, wg_ref, bg_ref, lam_ref, y_ref, a_ref, u_ref, hin_ref):
    seq = xr_ref.shape[1]
    n_chunks = seq // T_SCAN
    n_groups = T_SCAN // SUBLANES
    cw = cw_ref[...]
    cb = cb_ref[...]
    lam = lam_ref[...]
    nl = -lam
    softplus_neg_lam = jnp.maximum(nl, 0.0) + jnp.log(1.0 + jnp.exp(-jnp.abs(nl)))
    in_group = lax.broadcasted_iota(jnp.int32, (T_SCAN, D_RNN), 0) % SUBLANES
    group = lax.broadcasted_iota(jnp.int32, (n_groups, D_RNN), 0)
    n_slabs = D_RNN // LANES

    def last_row_of_groups(ref, x):
        for c in range(n_slabs):
            ref[c] = x[:, c * LANES:(c + 1) * LANES]
        return jnp.concatenate([ref[c, pl.ds(SUBLANES - 1, n_groups, stride=SUBLANES), :] for c in range(n_slabs)],
                               axis=1)

    def chunk(c, h_prev):
        t0 = pl.multiple_of(c * T_SCAN, T_SCAN)
        cur = xr_ref[0, pl.ds(t0, T_SCAN), :]
        p0 = pl.multiple_of(jnp.maximum(t0 - 8, 0), 8)
        prev = xr_ref[0, pl.ds(p0, 8), :]
        prev = jnp.where(c > 0, prev, 0.0)
        xc = jnp.concatenate([prev, cur], axis=0)
        conv = cb + cw[3:4, :] * cur
        for j in range(1, CONV_WIDTH):
            sh = pltpu.roll(xc, j, axis=0)[8:, :]
            conv = conv + cw[3 - j:4 - j, :] * sh
        gates = jnp.dot(conv.astype(BF16), wg_ref[...], preferred_element_type=F32) + bg_ref[...]
        r = jax.nn.sigmoid(gates[:, :D_RNN])
        i = jax.nn.sigmoid(gates[:, D_RNN:])
        log_a = -LRU_C * r * softplus_neg_lam
        a = jnp.exp(log_a)
        var = 1.0 - a * a
        u = jnp.where(var > 0.0, var * lax.rsqrt(var), 0.0) * (i * conv)
        d = 1
        while d < SUBLANES:
            a_sh = pltpu.roll(a, d, axis=0)
            u_sh = pltpu.roll(u, d, axis=0)
            keep = in_group >= d
            u = jnp.where(keep, a * u_sh + u, u)
            a = jnp.where(keep, a * a_sh, a)
            d *= 2
        ga = last_row_of_groups(a_ref, a)
        gu = last_row_of_groups(u_ref, u)
        d = 1
        while d < n_groups:
            ga_sh = pltpu.roll(ga, d, axis=0)
            gu_sh = pltpu.roll(gu, d, axis=0)
            keep = group >= d
            gu = jnp.where(keep, ga * gu_sh + gu, gu)
            ga = jnp.where(keep, ga * ga_sh, ga)
            d *= 2
        h_after = ga * h_prev + gu
        h_before = jnp.where(group >= 1, pltpu.roll(h_after, 1, axis=0), h_prev)
        for c in range(n_slabs):
            for s in range(SUBLANES):
                hin_ref[c, pl.ds(s, n_groups, stride=SUBLANES), :] = h_before[:, c * LANES:(c + 1) * LANES]
        h = a * jnp.concatenate([hin_ref[c] for c in range(n_slabs)], axis=1) + u
        y_ref[0, pl.ds(t0, T_SCAN), :] = h * jax.nn.gelu(gr_ref[0, pl.ds(t0, T_SCAN), :], approximate=True)
        return h_after[n_groups - 1:n_groups, :]

    lax.fori_loop(0, n_chunks, chunk, jnp.zeros((1, D_RNN), F32))


def _rnn(z3, conv_w, conv_b, wg_bf, bg, lam):
    bsz, seq, _ = z3.shape
    const = lambda shape: pl.BlockSpec(shape, lambda b: (0,) * len(shape))
    return pl.pallas_call(
        _rnn_kernel,
        grid=(bsz,),
        in_specs=[
            pl.BlockSpec((1, seq, D_RNN), lambda b: (b, 0, 0)),
            pl.BlockSpec((1, seq, D_RNN), lambda b: (b, 0, 1)),
            const((CONV_WIDTH, D_RNN)),
            const((1, D_RNN)),
            const((D_RNN, 2 * D_RNN)),
            const((1, 2 * D_RNN)),
            const((1, D_RNN)),
        ],
        out_specs=pl.BlockSpec((1, seq, D_RNN), lambda b: (b, 0, 0)),
        out_shape=jax.ShapeDtypeStruct((bsz, seq, D_RNN), F32),
        scratch_shapes=[pltpu.VMEM((D_RNN // LANES, T_SCAN, LANES), F32)] * 3,
        compiler_params=_cparams(("arbitrary",)),
        name="rnn",
    )(z3, z3, conv_w, conv_b, wg_bf, bg, lam)


def _group_rms(x, ones_bd):
    x2 = x * x
    hi = x2.astype(BF16)
    lo = (x2 - hi.astype(F32)).astype(BF16)
    ssq = jnp.dot(hi, ones_bd, preferred_element_type=F32) + jnp.dot(lo, ones_bd, preferred_element_type=F32)
    return x * lax.rsqrt(ssq * (1.0 / HEAD_DIM) + EPS)


def _attn_kernel(q_ref, k_ref, v_ref, qg_ref, kg_ref, lq1_ref, lk1_ref, lq2_ref, lk2_ref, sg_ref, ones_ref, o_ref):
    seq = q_ref.shape[1]
    ones_bd = ones_ref[...]
    lam = (jnp.exp(jnp.sum(lq1_ref[...] * lk1_ref[...], axis=-1, keepdims=True))
           - jnp.exp(jnp.sum(lq2_ref[...] * lk2_ref[...], axis=-1, keepdims=True)) + LAM_INIT)
    qn = _group_rms(q_ref[0], ones_bd) * qg_ref[...] * (HEAD_DIM ** -0.5)
    kn = _group_rms(k_ref[0], ones_bd) * kg_ref[...]
    lane = lax.broadcasted_iota(jnp.int32, (seq, LANES), 1)
    q1 = jnp.where(lane < HEAD_DIM, qn, 0.0).astype(BF16)
    q2 = jnp.where(lane >= HEAD_DIM, qn, 0.0).astype(BF16)
    kb = kn.astype(BF16)
    vb = v_ref[0].astype(BF16)
    sg = sg_ref[...]
    dn = (((1,), (1,)), ((), ()))
    for qi in range(seq // TQ):
        kv = (qi + 1) * TQ
        rows = slice(qi * TQ, kv)
        qpos = qi * TQ + lax.broadcasted_iota(jnp.int32, (TQ, kv), 0)
        kpos = lax.broadcasted_iota(jnp.int32, (TQ, kv), 1)
        causal = kpos <= qpos

        def probs(qm):
            s = lax.dot_general(qm[rows], kb[:kv], dn, preferred_element_type=F32)
            s = jnp.where(causal, s, -jnp.inf)
            e = jnp.exp(s - jnp.max(s, axis=-1, keepdims=True))
            return e, jnp.sum(e, axis=-1, keepdims=True)

        e1, l1 = probs(q1)
        e2, l2 = probs(q2)
        w = e1 * (1.0 / l1) - e2 * (lam / l2)
        o = jnp.dot(w.astype(BF16), vb[:kv], preferred_element_type=F32)
        o = o * lax.rsqrt(jnp.mean(o * o, axis=-1, keepdims=True) + EPS) * sg * (1.0 - LAM_INIT)
        o_ref[0, rows, :] = o


def _attn(z3, qg2, kg2, lq1, lk1, lq2, lk2, sg, ones_bd):
    bsz, seq, _ = z3.shape
    qoff = 2 * D_RNN // LANES
    koff = qoff + D_ATTN // LANES
    voff = koff + D_ATTN // LANES
    const = lambda shape: pl.BlockSpec(shape, lambda b, h: (0,) * len(shape))
    return pl.pallas_call(
        _attn_kernel,
        grid=(bsz, N_HEADS),
        in_specs=[
            pl.BlockSpec((1, seq, LANES), lambda b, h: (b, 0, qoff + h)),
            pl.BlockSpec((1, seq, LANES), lambda b, h: (b, 0, koff + h)),
            pl.BlockSpec((1, seq, LANES), lambda b, h: (b, 0, voff + h)),
            const((1, LANES)), const((1, LANES)),
            const((1, HEAD_DIM)), const((1, HEAD_DIM)), const((1, HEAD_DIM)), const((1, HEAD_DIM)),
            const((1, LANES)), const((LANES, LANES)),
        ],
        out_specs=pl.BlockSpec((1, seq, LANES), lambda b, h: (b, 0, h)),
        out_shape=jax.ShapeDtypeStruct((bsz, seq, D_ATTN), F32),
        compiler_params=_cparams(("arbitrary", "arbitrary")),
        name="attn",
    )(z3, z3, z3, qg2, kg2, lq1, lk1, lq2, lk2, sg, ones_bd)


def _out_proj_kernel(yr_ref, ya_ref, x_ref, wo_ref, g_ref, rwh_ref, rwl_ref, rb_ref, tri_ref,
                     x1_ref, h2_ref, route_ref, gates_ref, counts_ref):
    acc = jnp.dot(yr_ref[...].astype(BF16), wo_ref[0], preferred_element_type=F32)
    acc = acc + jnp.dot(ya_ref[...].astype(BF16), wo_ref[1], preferred_element_type=F32)
    x1 = x_ref[...] + acc
    x1_ref[...] = x1
    h2 = x1 * lax.rsqrt(jnp.mean(x1 * x1, axis=-1, keepdims=True) + EPS) * g_ref[...]
    hh = h2.astype(BF16)
    h2_ref[...] = hh
    hl = (h2 - hh.astype(F32)).astype(BF16)
    logits = (jnp.dot(hh, rwh_ref[...], preferred_element_type=F32)
              + jnp.dot(hl, rwh_ref[...], preferred_element_type=F32)
              + jnp.dot(hh, rwl_ref[...], preferred_element_type=F32)) + rb_ref[...]
    tm = logits.shape[0]
    l = logits.T[:N_EXPERTS, :]
    eid = lax.broadcasted_iota(jnp.int32, (N_EXPERTS, tm), 0)
    vals, idxs = [], []
    for _ in range(TOP_K):
        m = jnp.max(l, axis=0, keepdims=True)
        idx = jnp.min(jnp.where(l == m, eid, N_EXPERTS), axis=0, keepdims=True)
        vals.append(m)
        idxs.append(idx)
        l = jnp.where(eid == idx, -jnp.inf, l)
    es = [jnp.exp(v - vals[0]) for v in vals]
    inv = 1.0 / (es[0] + es[1] + es[2] + es[3])
    chosen = jnp.zeros((N_EXPERTS, tm), F32)
    for k in range(TOP_K):
        chosen = chosen + (eid == idxs[k]).astype(F32)
    before = jnp.dot(chosen.astype(BF16), tri_ref[...], preferred_element_type=F32)
    sub = lax.broadcasted_iota(jnp.int32, (SUBLANES, tm), 0)
    route = jnp.zeros((SUBLANES, tm), jnp.int32)
    gates = jnp.zeros((SUBLANES, tm), F32)
    for k in range(TOP_K):
        rank = jnp.sum(jnp.where(eid == idxs[k], before, 0.0), axis=0, keepdims=True).astype(jnp.int32)
        route = jnp.where(sub == k, idxs[k], route)
        route = jnp.where(sub == TOP_K + k, rank, route)
        gates = jnp.where(sub == k, es[k] * inv, gates)
    route_ref[...] = route
    gates_ref[...] = gates
    counts_ref[...] = jnp.broadcast_to(jnp.sum(chosen, axis=1, keepdims=True), (N_EXPERTS, LANES)).astype(jnp.int32)


def _out_proj(y_rnn, y_attn, x2, wo_bf, g2, rw_hi, rw_lo, rb, tri):
    n = x2.shape[0]
    n_tiles = n // T_TOK
    row = lambda w: pl.BlockSpec((T_TOK, w), lambda i: (i, 0))
    col = pl.BlockSpec((SUBLANES, T_TOK), lambda i: (0, i))
    const = lambda shape: pl.BlockSpec(shape, lambda i: (0,) * len(shape))
    return pl.pallas_call(
        _out_proj_kernel,
        grid=(n_tiles,),
        in_specs=[row(D_RNN), row(D_ATTN), row(D_MODEL),
                  const((2, D_RNN, D_MODEL)), const((1, D_MODEL)),
                  const((D_MODEL, LANES)), const((D_MODEL, LANES)), const((1, LANES)),
                  const((T_TOK, T_TOK))],
        out_specs=[row(D_MODEL), row(D_MODEL), col, col, pl.BlockSpec((N_EXPERTS, LANES), lambda i: (i, 0))],
        out_shape=[jax.ShapeDtypeStruct((n, D_MODEL), F32),
                   jax.ShapeDtypeStruct((n, D_MODEL), BF16),
                   jax.ShapeDtypeStruct((SUBLANES, n), jnp.int32),
                   jax.ShapeDtypeStruct((SUBLANES, n), F32),
                   jax.ShapeDtypeStruct((n_tiles * N_EXPERTS, LANES), jnp.int32)],
        compiler_params=_cparams(("arbitrary",)),
        name="out_proj",
    )(y_rnn, y_attn, x2, wo_bf, g2, rw_hi, rw_lo, rb, tri)


def _plan(tile_counts, n_blocks):
    n_tiles = tile_counts.shape[0]
    counts = jnp.sum(tile_counts, axis=0)
    padded = (counts + BM - 1) // BM * BM
    pad_end = jnp.cumsum(padded).astype(jnp.int32)
    pad_start = pad_end - padded
    earlier_tiles = jnp.cumsum(tile_counts, axis=0) - tile_counts
    run_off = (jnp.cumsum(tile_counts, axis=1) - tile_counts).astype(jnp.int32)
    run_dst = (pad_start[None, :] + earlier_tiles).astype(jnp.int32)
    off_lanes = jnp.broadcast_to(run_off.reshape(n_tiles * N_EXPERTS, 1), (n_tiles * N_EXPERTS, LANES))
    blk_start = jnp.arange(n_blocks, dtype=jnp.int32) * BM
    blk_e = jnp.minimum(jnp.sum((pad_end[None, :] <= blk_start[:, None]).astype(jnp.int32), axis=1), N_EXPERTS - 1)
    n_valid = (pad_end[-1] // BM).reshape(1)
    return (run_dst.reshape(-1), tile_counts.reshape(-1).astype(jnp.int32), run_off.reshape(-1), off_lanes,
            blk_e, n_valid, pad_end, padded)


def _tile_positions(route, off_col):
    eid = lax.broadcasted_iota(jnp.int32, (N_EXPERTS, route.shape[1]), 0)
    pos = []
    for k in range(TOP_K):
        start = jnp.sum(jnp.where(eid == route[k:k + 1, :], off_col, 0), axis=0, keepdims=True)
        pos.append(start + route[TOP_K + k:TOP_K + k + 1, :])
    return pos


def _run_copies(run_len_ref, tile, make_copy, act):
    def per_expert(e, carry):
        run = tile * N_EXPERTS + e
        length = run_len_ref[run]
        for b in range(RUN_BITS - 1, -1, -1):
            piece_start = (length >> (b + 1)) << (b + 1)

            @pl.when(((length >> b) & 1) == 1)
            def _():
                act(make_copy(run, piece_start, 1 << b))
        return carry

    lax.fori_loop(0, N_EXPERTS, per_expert, 0)


def _dispatch_kernel(run_dst_ref, run_len_ref, run_off_ref, pad_end_ref, padded_ref, n_valid_ref,
                     h_ref, route_ref, off_ref, xs_ref, sorted_ref, zeros_ref, sem, zsem):
    i = pl.program_id(0)
    n_tiles = pl.num_programs(0)
    slot = i % 2
    blk_rows = BM * ROW_TILES
    n_blocks = xs_ref.shape[0] // blk_rows

    def run_copy(s, run, piece_start, size):
        src_row = pl.multiple_of((run_off_ref[run] + piece_start) * ROW_TILES, ROW_TILES)
        dst_row = pl.multiple_of((run_dst_ref[run] + piece_start) * ROW_TILES, ROW_TILES)
        return pltpu.make_async_copy(sorted_ref.at[s, pl.ds(src_row, size * ROW_TILES), :],
                                     xs_ref.at[pl.ds(dst_row, size * ROW_TILES), :], sem.at[s])

    def start_runs(tile, s):
        _run_copies(run_len_ref, tile, functools.partial(run_copy, s), lambda cp: cp.start())

    def wait_runs(tile, s):
        _run_copies(run_len_ref, tile, functools.partial(run_copy, s), lambda cp: cp.wait())

    @pl.when(i == 0)
    def _():
        zeros_ref[...] = jnp.zeros_like(zeros_ref)

        def zero_block(blk):
            start = pl.multiple_of(blk * blk_rows, blk_rows)
            cp = pltpu.make_async_copy(zeros_ref, xs_ref.at[pl.ds(start, blk_rows), :], zsem)
            cp.start()
            cp.wait()

        def zero_tail(e, carry):
            @pl.when(padded_ref[e] > 0)
            def _():
                zero_block(pad_end_ref[e] // BM - 1)
            return carry

        def zero_dead(b, carry):
            zero_block(b)
            return carry

        lax.fori_loop(0, N_EXPERTS, zero_tail, 0)
        lax.fori_loop(n_valid_ref[0], n_blocks, zero_dead, 0)

    @pl.when(i >= 2)
    def _():
        wait_runs(i - 2, slot)

    pos = _tile_positions(route_ref[...], off_ref[:, 0:1])
    hb = h_ref[...]
    for c in range(TOP_K * T_TOK // P_ROWS):
        row = c * P_ROWS + lax.broadcasted_iota(jnp.int32, (P_ROWS, T_TOK), 0)
        hit = row == pos[0]
        for k in range(1, TOP_K):
            hit = hit | (row == pos[k])
        perm = jnp.where(hit, 1.0, 0.0).astype(BF16)
        rows = jnp.dot(perm, hb, preferred_element_type=F32)
        _to_tile_rows(sorted_ref.at[slot, pl.ds(c * P_ROWS * ROW_TILES, P_ROWS * ROW_TILES), :], rows)
    start_runs(i, slot)

    @pl.when(i == n_tiles - 1)
    def _():
        @pl.when(i >= 1)
        def _():
            wait_runs(i - 1, 1 - slot)
        wait_runs(i, slot)


def _dispatch(h2, route, off_lanes, run_dst, run_len, run_off, pad_end, padded, n_valid, n_rows):
    n = h2.shape[0]
    grid_spec = pltpu.PrefetchScalarGridSpec(
        num_scalar_prefetch=6,
        grid=(n // T_TOK,),
        in_specs=[
            pl.BlockSpec((T_TOK, D_MODEL), lambda i, *_: (i, 0)),
            pl.BlockSpec((SUBLANES, T_TOK), lambda i, *_: (0, i)),
            pl.BlockSpec((N_EXPERTS, LANES), lambda i, *_: (i, 0)),
        ],
        out_specs=pl.BlockSpec(memory_space=pl.ANY),
        scratch_shapes=[pltpu.VMEM((2, TOP_K * T_TOK * ROW_TILES, LANES), F32),
                        pltpu.VMEM((BM * ROW_TILES, LANES), F32),
                        pltpu.SemaphoreType.DMA((2,)), pltpu.SemaphoreType.DMA(())],
    )
    return pl.pallas_call(
        _dispatch_kernel,
        grid_spec=grid_spec,
        out_shape=jax.ShapeDtypeStruct((n_rows * ROW_TILES, LANES), F32),
        compiler_params=_cparams(("arbitrary",)),
        name="dispatch",
    )(run_dst, run_len, run_off, pad_end, padded, n_valid, h2, route, off_lanes)


def _experts_kernel(blk_e_ref, n_valid_ref, xs_ref, w1_ref, b1_ref, w2_ref, b2_ref, y_ref, w1b_ref, w2b_ref):
    i = pl.program_id(0)

    @pl.when(i < n_valid_ref[0])
    def _():
        prev_e = blk_e_ref[jnp.maximum(i - 1, 0)]

        @pl.when((i == 0) | (blk_e_ref[i] != prev_e))
        def _():
            w1b_ref[...] = w1_ref[0].astype(BF16)
            w2b_ref[...] = w2_ref[0].astype(BF16)

        x = _from_tile_rows(xs_ref, BM).astype(BF16)
        hcat = jnp.dot(x, w1b_ref[...], preferred_element_type=F32) + b1_ref[0]
        gate = jnp.minimum(hcat[:, :D_FF], SWIGLU_LIMIT)
        up = jnp.clip(hcat[:, D_FF:], -SWIGLU_LIMIT, SWIGLU_LIMIT)
        act = gate * jax.nn.sigmoid(SWIGLU_ALPHA * gate) * (up + 1.0)
        y = jnp.dot(act.astype(BF16), w2b_ref[...], preferred_element_type=F32) + b2_ref[0]
        _to_tile_rows(y_ref, y)

    @pl.when(i >= n_valid_ref[0])
    def _():
        y_ref[...] = jnp.zeros_like(y_ref)


def _experts(xs, blk_e, n_valid, w1, b1, w2, b2):
    blk_rows = BM * ROW_TILES
    n_blocks = xs.shape[0] // blk_rows
    exp3 = lambda i, be, nv: (be[i], 0, 0)
    grid_spec = pltpu.PrefetchScalarGridSpec(
        num_scalar_prefetch=2,
        grid=(n_blocks,),
        in_specs=[
            pl.BlockSpec((blk_rows, LANES), lambda i, be, nv: (jnp.minimum(i, nv[0] - 1), 0)),
            pl.BlockSpec((1, D_MODEL, 2 * D_FF), exp3),
            pl.BlockSpec((1, 1, 2 * D_FF), exp3),
            pl.BlockSpec((1, D_FF, D_MODEL), exp3),
            pl.BlockSpec((1, 1, D_MODEL), exp3),
        ],
        out_specs=pl.BlockSpec((blk_rows, LANES), lambda i, be, nv: (i, 0)),
        scratch_shapes=[pltpu.VMEM((D_MODEL, 2 * D_FF), BF16), pltpu.VMEM((D_FF, D_MODEL), BF16)],
    )
    return pl.pallas_call(
        _experts_kernel,
        grid_spec=grid_spec,
        out_shape=jax.ShapeDtypeStruct(xs.shape, F32),
        compiler_params=_cparams(("arbitrary",)),
        name="experts",
    )(blk_e, n_valid, xs, w1, b1, w2, b2)


def _combine_kernel(run_src_ref, run_len_ref, run_off_ref, x1_ref, route_ref, gates_ref, off_ref, ys_ref, o_ref,
                    buf_ref, sem):
    i = pl.program_id(0)
    n_tiles = pl.num_programs(0)
    slot = i % 2

    def run_copy(s, run, piece_start, size):
        src_row = pl.multiple_of((run_src_ref[run] + piece_start) * ROW_TILES, ROW_TILES)
        dst_row = pl.multiple_of((run_off_ref[run] + piece_start) * ROW_TILES, ROW_TILES)
        return pltpu.make_async_copy(ys_ref.at[pl.ds(src_row, size * ROW_TILES), :],
                                     buf_ref.at[s, pl.ds(dst_row, size * ROW_TILES), :], sem.at[s])

    def start_runs(tile, s):
        _run_copies(run_len_ref, tile, functools.partial(run_copy, s), lambda cp: cp.start())

    @pl.when(i == 0)
    def _():
        start_runs(0, 0)

    @pl.when(i + 1 < n_tiles)
    def _():
        start_runs(i + 1, 1 - slot)

    _run_copies(run_len_ref, i, functools.partial(run_copy, slot), lambda cp: cp.wait())

    pos = _tile_positions(route_ref[...], off_ref[:, 0:1])
    sub = lax.broadcasted_iota(jnp.int32, (SUBLANES, T_TOK), 0)
    gates = gates_ref[...]
    packed = jnp.zeros((SUBLANES, T_TOK), F32)
    for k in range(TOP_K):
        packed = jnp.where(sub == k, pos[k].astype(F32), packed)
        packed = jnp.where(sub == TOP_K + k, gates[k:k + 1, :], packed)
    cols = packed.T
    acc = x1_ref[...]
    for c in range(TOP_K * T_TOK // P_ROWS):
        row = (c * P_ROWS + lax.broadcasted_iota(jnp.int32, (T_TOK, P_ROWS), 1)).astype(F32)
        g = jnp.zeros((T_TOK, P_ROWS), F32)
        for k in range(TOP_K):
            g = g + jnp.where(row == cols[:, k:k + 1], cols[:, TOP_K + k:TOP_K + k + 1], 0.0)
        y = _from_tile_rows(buf_ref.at[slot, pl.ds(c * P_ROWS * ROW_TILES, P_ROWS * ROW_TILES), :], P_ROWS)
        acc = acc + jnp.dot(g.astype(BF16), y.astype(BF16), preferred_element_type=F32)
    o_ref[...] = acc


def _combine(x1, route, gates, off_lanes, run_dst, run_len, run_off, ys):
    n = x1.shape[0]
    grid_spec = pltpu.PrefetchScalarGridSpec(
        num_scalar_prefetch=3,
        grid=(n // T_TOK,),
        in_specs=[
            pl.BlockSpec((T_TOK, D_MODEL), lambda i, *_: (i, 0)),
            pl.BlockSpec((SUBLANES, T_TOK), lambda i, *_: (0, i)),
            pl.BlockSpec((SUBLANES, T_TOK), lambda i, *_: (0, i)),
            pl.BlockSpec((N_EXPERTS, LANES), lambda i, *_: (i, 0)),
            pl.BlockSpec(memory_space=pl.ANY),
        ],
        out_specs=pl.BlockSpec((T_TOK, D_MODEL), lambda i, *_: (i, 0)),
        scratch_shapes=[pltpu.VMEM((2, TOP_K * T_TOK * ROW_TILES, LANES), F32), pltpu.SemaphoreType.DMA((2,))],
    )
    return pl.pallas_call(
        _combine_kernel,
        grid_spec=grid_spec,
        out_shape=jax.ShapeDtypeStruct((n, D_MODEL), F32),
        compiler_params=_cparams(("arbitrary",)),
        name="combine",
    )(run_dst, run_len, run_off, x1, route, gates, off_lanes, ys)


def _block_diag(w):
    n, r, _ = w.shape
    eye = jnp.eye(n, dtype=w.dtype)
    return (eye[:, None, :, None] * w[:, :, None, :]).reshape(n * r, n * r)


def kernel(x, norm1_g, w_in, conv_w, conv_b, lru_wa, lru_ba, lru_wx, lru_bx, lru_lambda, q_norm_g, k_norm_g,
           lambda_q1, lambda_k1, lambda_q2, lambda_k2, subln_g, w_out, norm2_g, router_w, router_b, w1, b1, w2, b2):
    bsz, seq, d = x.shape
    n_tok = bsz * seq
    assert d == D_MODEL and n_tok % TM_PROJ == 0 and seq % T_SCAN == 0 and seq % TQ == 0 and n_tok % T_TOK == 0
    assert (n_tok * TOP_K) % BM == 0
    assert norm1_g.shape[0] == 1, "single-layer stack"
    x2 = x.reshape(n_tok, d)

    z = _in_proj(x2, norm1_g[0][None, :], w_in[0].astype(BF16))
    z3 = z.reshape(bsz, seq, D_IN)

    wg = jnp.concatenate([_block_diag(lru_wa[0]), _block_diag(lru_wx[0])], axis=1).astype(BF16)
    bg = jnp.concatenate([lru_ba[0], lru_bx[0]])[None, :]
    y_rnn = _rnn(z3, conv_w[0], conv_b[0][None, :], wg, bg, lru_lambda[0][None, :])

    half = jnp.arange(LANES) // HEAD_DIM
    ones_bd = (half[:, None] == half[None, :]).astype(BF16)
    y_attn = _attn(z3, jnp.tile(q_norm_g[0], 2)[None, :], jnp.tile(k_norm_g[0], 2)[None, :],
                   lambda_q1[0][None, :], lambda_k1[0][None, :], lambda_q2[0][None, :], lambda_k2[0][None, :],
                   subln_g[0][None, :], ones_bd)

    rw = jnp.pad(router_w[0], ((0, 0), (0, LANES - N_EXPERTS)))
    rw_hi = rw.astype(BF16)
    rw_lo = (rw - rw_hi.astype(F32)).astype(BF16)
    rb = jnp.pad(router_b[0], (0, LANES - N_EXPERTS))[None, :]
    tok = jnp.arange(T_TOK)
    tri = (tok[:, None] < tok[None, :]).astype(BF16)
    x1, h2, route, gates, counts = _out_proj(
        y_rnn.reshape(n_tok, D_RNN), y_attn.reshape(n_tok, D_ATTN), x2,
        w_out[0].astype(BF16).reshape(2, D_RNN, D_MODEL), norm2_g[0][None, :], rw_hi, rw_lo, rb, tri)

    n_blocks = (n_tok * TOP_K) // BM + N_EXPERTS
    tile_counts = counts[:, 0].reshape(n_tok // T_TOK, N_EXPERTS)
    run_dst, run_len, run_off, off_lanes, blk_e, n_valid, pad_end, padded = _plan(tile_counts, n_blocks)
    xs = _dispatch(h2, route, off_lanes, run_dst, run_len, run_off, pad_end, padded, n_valid, n_blocks * BM)
    ys = _experts(xs, blk_e, n_valid, w1[0], b1[0][:, None, :], w2[0], b2[0][:, None, :])
    out = _combine(x1, route, gates, off_lanes, run_dst, run_len, run_off, ys)
    return out.reshape(bsz, seq, d)
```

```python
import functools
import math

import jax
import jax.numpy as jnp
from jax import lax
from jax.experimental import pallas as pl
from jax.experimental.pallas import tpu as pltpu

F32 = jnp.float32
BF16 = jnp.bfloat16

D_MODEL = 1024
D_RNN = 512
N_RNN_BLOCKS = 8
RNN_BLOCK = 64
CONV_WIDTH = 4
LRU_C = 8.0
HEAD_DIM = 64
N_HEADS = 4
D_ATTN = 512
D_IN = 2 * D_RNN + 3 * D_ATTN
N_EXPERTS = 32
TOP_K = 4
D_FF = 1024
SWIGLU_LIMIT = 7.0
SWIGLU_ALPHA = 1.702
EPS = 1e-5
LAM_INIT = 0.8 - 0.6 * math.exp(0.0)

LANES = 128
SUBLANES = 8
ROW_TILES = D_MODEL // LANES
N_DMA_PRIORITIES = 2
VMEM_LIMIT = 52 * 1024 * 1024

TM_PROJ = 512
T_SCAN = 256
TQ = 256
BM = 512
T_TOK = 512
P_ROWS = 256
RUN_BITS = T_TOK.bit_length()


def _cparams(sem):
    return pltpu.CompilerParams(dimension_semantics=sem, vmem_limit_bytes=VMEM_LIMIT)


def _to_tile_rows(ref, x):
    rows = x.shape[0]
    for s in range(ROW_TILES):
        ref[pl.ds(s, rows, stride=ROW_TILES), :] = x[:, s * LANES:(s + 1) * LANES]


def _from_tile_rows(ref, rows):
    return jnp.concatenate([ref[pl.ds(s, rows, stride=ROW_TILES), :] for s in range(ROW_TILES)], axis=1)


def _in_proj_kernel(x_ref, g_ref, w_ref, z_ref):
    x = x_ref[...]
    ms = jnp.mean(x * x, axis=-1, keepdims=True)
    h = x * lax.rsqrt(ms + EPS) * g_ref[...]
    z_ref[...] = jnp.dot(h.astype(BF16), w_ref[...], preferred_element_type=F32)


def _in_proj(x2, g, w_bf):
    n = x2.shape[0]
    return pl.pallas_call(
        _in_proj_kernel,
        grid=(n // TM_PROJ,),
        in_specs=[
            pl.BlockSpec((TM_PROJ, D_MODEL), lambda i: (i, 0)),
            pl.BlockSpec((1, D_MODEL), lambda i: (0, 0)),
            pl.BlockSpec((D_MODEL, D_IN), lambda i: (0, 0)),
        ],
        out_specs=pl.BlockSpec((TM_PROJ, D_IN), lambda i: (i, 0)),
        out_shape=jax.ShapeDtypeStruct((n, D_IN), F32),
        compiler_params=_cparams(("arbitrary",)),
        name="in_proj",
    )(x2, g, w_bf)


def _rnn_kernel(xr_ref, gr_ref, cw_ref, cb_ref, wg_ref, bg_ref, lam_ref, y_ref, a_ref, u_ref, hin_ref):
    seq = xr_ref.shape[1]
    n_chunks = seq // T_SCAN
    n_groups = T_SCAN // SUBLANES
    cw = cw_ref[...]
    cb = cb_ref[...]
    lam = lam_ref[...]
    nl = -lam
    softplus_neg_lam = jnp.maximum(nl, 0.0) + jnp.log(1.0 + jnp.exp(-jnp.abs(nl)))
    in_group = lax.broadcasted_iota(jnp.int32, (T_SCAN, D_RNN), 0) % SUBLANES
    group = lax.broadcasted_iota(jnp.int32, (n_groups, D_RNN), 0)
    n_slabs = D_RNN // LANES

    def last_row_of_groups(ref, x):
        for c in range(n_slabs):
            ref[c] = x[:, c * LANES:(c + 1) * LANES]
        return jnp.concatenate([ref[c, pl.ds(SUBLANES - 1, n_groups, stride=SUBLANES), :] for c in range(n_slabs)],
                               axis=1)

    def chunk(c, h_prev):
        t0 = pl.multiple_of(c * T_SCAN, T_SCAN)
        cur = xr_ref[0, pl.ds(t0, T_SCAN), :]
        p0 = pl.multiple_of(jnp.maximum(t0 - 8, 0), 8)
        prev = xr_ref[0, pl.ds(p0, 8), :]
        prev = jnp.where(c > 0, prev, 0.0)
        xc = jnp.concatenate([prev, cur], axis=0)
        conv = cb + cw[3:4, :] * cur
        for j in range(1, CONV_WIDTH):
            sh = pltpu.roll(xc, j, axis=0)[8:, :]
            conv = conv + cw[3 - j:4 - j, :] * sh
        gates = jnp.dot(conv.astype(BF16), wg_ref[...], preferred_element_type=F32) + bg_ref[...]
        r = jax.nn.sigmoid(gates[:, :D_RNN])
        i = jax.nn.sigmoid(gates[:, D_RNN:])
        log_a = -LRU_C * r * softplus_neg_lam
        a = jnp.exp(log_a)
        var = 1.0 - a * a
        u = jnp.where(var > 0.0, var * lax.rsqrt(var), 0.0) * (i * conv)
        d = 1
        while d < SUBLANES:
            a_sh = pltpu.roll(a, d, axis=0)
            u_sh = pltpu.roll(u, d, axis=0)
            keep = in_group >= d
            u = jnp.where(keep, a * u_sh + u, u)
            a = jnp.where(keep, a * a_sh, a)
            d *= 2
        ga = last_row_of_groups(a_ref, a)
        gu = last_row_of_groups(u_ref, u)
        d = 1
        while d < n_groups:
            ga_sh = pltpu.roll(ga, d, axis=0)
            gu_sh = pltpu.roll(gu, d, axis=0)
            keep = group >= d
            gu = jnp.where(keep, ga * gu_sh + gu, gu)
            ga = jnp.where(keep, ga * ga_sh, ga)
            d *= 2
        h_after = ga * h_prev + gu
        h_before = jnp.where(group >= 1, pltpu.roll(h_after, 1, axis=0), h_prev)
        for c in range(n_slabs):
            for s in range(SUBLANES):
                hin_ref[c, pl.ds(s, n_groups, stride=SUBLANES), :] = h_before[:, c * LANES:(c + 1) * LANES]
        h = a * jnp.concatenate([hin_ref[c] for c in range(n_slabs)], axis=1) + u
        y_ref[0, pl.ds(t0, T_SCAN), :] = h * jax.nn.gelu(gr_ref[0, pl.ds(t0, T_SCAN), :], approximate=True)
        return h_after[n_groups - 1:n_groups, :]

    lax.fori_loop(0, n_chunks, chunk, jnp.zeros((1, D_RNN), F32))


def _rnn(z3, conv_w, conv_b, wg_bf, bg, lam):
    bsz, seq, _ = z3.shape
    const = lambda shape: pl.BlockSpec(shape, lambda b: (0,) * len(shape))
    return pl.pallas_call(
        _rnn_kernel,
        grid=(bsz,),
        in_specs=[
            pl.BlockSpec((1, seq, D_RNN), lambda b: (b, 0, 0)),
            pl.BlockSpec((1, seq, D_RNN), lambda b: (b, 0, 1)),
            const((CONV_WIDTH, D_RNN)),
            const((1, D_RNN)),
            const((D_RNN, 2 * D_RNN)),
            const((1, 2 * D_RNN)),
            const((1, D_RNN)),
        ],
        out_specs=pl.BlockSpec((1, seq, D_RNN), lambda b: (b, 0, 0)),
        out_shape=jax.ShapeDtypeStruct((bsz, seq, D_RNN), F32),
        scratch_shapes=[pltpu.VMEM((D_RNN // LANES, T_SCAN, LANES), F32)] * 3,
        compiler_params=_cparams(("arbitrary",)),
        name="rnn",
    )(z3, z3, conv_w, conv_b, wg_bf, bg, lam)


def _group_rms(x, ones_bd):
    x2 = x * x
    hi = x2.astype(BF16)
    lo = (x2 - hi.astype(F32)).astype(BF16)
    ssq = jnp.dot(hi, ones_bd, preferred_element_type=F32) + jnp.dot(lo, ones_bd, preferred_element_type=F32)
    return x * lax.rsqrt(ssq * (1.0 / HEAD_DIM) + EPS)


def _attn_kernel(q_ref, k_ref, v_ref, qg_ref, kg_ref, lq1_ref, lk1_ref, lq2_ref, lk2_ref, sg_ref, ones_ref, o_ref):
    seq = q_ref.shape[1]
    ones_bd = ones_ref[...]
    lam = (jnp.exp(jnp.sum(lq1_ref[...] * lk1_ref[...], axis=-1, keepdims=True))
           - jnp.exp(jnp.sum(lq2_ref[...] * lk2_ref[...], axis=-1, keepdims=True)) + LAM_INIT)
    qn = _group_rms(q_ref[0], ones_bd) * qg_ref[...] * (HEAD_DIM ** -0.5)
    kn = _group_rms(k_ref[0], ones_bd) * kg_ref[...]
    lane = lax.broadcasted_iota(jnp.int32, (seq, LANES), 1)
    q1 = jnp.where(lane < HEAD_DIM, qn, 0.0).astype(BF16)
    q2 = jnp.where(lane >= HEAD_DIM, qn, 0.0).astype(BF16)
    kb = kn.astype(BF16)
    vb = v_ref[0].astype(BF16)
    sg = sg_ref[...]
    dn = (((1,), (1,)), ((), ()))
    for qi in range(seq // TQ):
        kv = (qi + 1) * TQ
        rows = slice(qi * TQ, kv)
        qpos = qi * TQ + lax.broadcasted_iota(jnp.int32, (TQ, kv), 0)
        kpos = lax.broadcasted_iota(jnp.int32, (TQ, kv), 1)
        causal = kpos <= qpos

        def probs(qm):
            s = lax.dot_general(qm[rows], kb[:kv], dn, preferred_element_type=F32)
            s = jnp.where(causal, s, -jnp.inf)
            e = jnp.exp(s - jnp.max(s, axis=-1, keepdims=True))
            return e, jnp.sum(e, axis=-1, keepdims=True)

        e1, l1 = probs(q1)
        e2, l2 = probs(q2)
        w = e1 * (1.0 / l1) - e2 * (lam / l2)
        o = jnp.dot(w.astype(BF16), vb[:kv], preferred_element_type=F32)
        o = o * lax.rsqrt(jnp.mean(o * o, axis=-1, keepdims=True) + EPS) * sg * (1.0 - LAM_INIT)
        o_ref[0, rows, :] = o


def _attn(z3, qg2, kg2, lq1, lk1, lq2, lk2, sg, ones_bd):
    bsz, seq, _ = z3.shape
    qoff = 2 * D_RNN // LANES
    koff = qoff + D_ATTN // LANES
    voff = koff + D_ATTN // LANES
    const = lambda shape: pl.BlockSpec(shape, lambda b, h: (0,) * len(shape))
    return pl.pallas_call(
        _attn_kernel,
        grid=(bsz, N_HEADS),
        in_specs=[
            pl.BlockSpec((1, seq, LANES), lambda b, h: (b, 0, qoff + h)),
            pl.BlockSpec((1, seq, LANES), lambda b, h: (b, 0, koff + h)),
            pl.BlockSpec((1, seq, LANES), lambda b, h: (b, 0, voff + h)),
            const((1, LANES)), const((1, LANES)),
            const((1, HEAD_DIM)), const((1, HEAD_DIM)), const((1, HEAD_DIM)), const((1, HEAD_DIM)),
            const((1, LANES)), const((LANES, LANES)),
        ],
        out_specs=pl.BlockSpec((1, seq, LANES), lambda b, h: (b, 0, h)),
        out_shape=jax.ShapeDtypeStruct((bsz, seq, D_ATTN), F32),
        compiler_params=_cparams(("arbitrary", "arbitrary")),
        name="attn",
    )(z3, z3, z3, qg2, kg2, lq1, lk1, lq2, lk2, sg, ones_bd)


def _out_proj_kernel(yr_ref, ya_ref, x_ref, wo_ref, g_ref, rwh_ref, rwl_ref, rb_ref, tri_ref,
                     x1_ref, h2_ref, route_ref, gates_ref, counts_ref):
    acc = jnp.dot(yr_ref[...].astype(BF16), wo_ref[0], preferred_element_type=F32)
    acc = acc + jnp.dot(ya_ref[...].astype(BF16), wo_ref[1], preferred_element_type=F32)
    x1 = x_ref[...] + acc
    x1_ref[...] = x1
    h2 = x1 * lax.rsqrt(jnp.mean(x1 * x1, axis=-1, keepdims=True) + EPS) * g_ref[...]
    hh = h2.astype(BF16)
    h2_ref[...] = hh
    hl = (h2 - hh.astype(F32)).astype(BF16)
    logits = (jnp.dot(hh, rwh_ref[...], preferred_element_type=F32)
              + jnp.dot(hl, rwh_ref[...], preferred_element_type=F32)
              + jnp.dot(hh, rwl_ref[...], preferred_element_type=F32)) + rb_ref[...]
    tm = logits.shape[0]
    l = logits.T[:N_EXPERTS, :]
    eid = lax.broadcasted_iota(jnp.int32, (N_EXPERTS, tm), 0)
    vals, idxs = [], []
    for _ in range(TOP_K):
        m = jnp.max(l, axis=0, keepdims=True)
        idx = jnp.min(jnp.where(l == m, eid, N_EXPERTS), axis=0, keepdims=True)
        vals.append(m)
        idxs.append(idx)
        l = jnp.where(eid == idx, -jnp.inf, l)
    es = [jnp.exp(v - vals[0]) for v in vals]
    inv = 1.0 / (es[0] + es[1] + es[2] + es[3])
    chosen = jnp.zeros((N_EXPERTS, tm), F32)
    for k in range(TOP_K):
        chosen = chosen + (eid == idxs[k]).astype(F32)
    before = jnp.dot(chosen.astype(BF16), tri_ref[...], preferred_element_type=F32)
    sub = lax.broadcasted_iota(jnp.int32, (SUBLANES, tm), 0)
    route = jnp.zeros((SUBLANES, tm), jnp.int32)
    gates = jnp.zeros((SUBLANES, tm), F32)
    for k in range(TOP_K):
        rank = jnp.sum(jnp.where(eid == idxs[k], before, 0.0), axis=0, keepdims=True).astype(jnp.int32)
        route = jnp.where(sub == k, idxs[k], route)
        route = jnp.where(sub == TOP_K + k, rank, route)
        gates = jnp.where(sub == k, es[k] * inv, gates)
    route_ref[...] = route
    gates_ref[...] = gates
    counts_ref[...] = jnp.broadcast_to(jnp.sum(chosen, axis=1, keepdims=True), (N_EXPERTS, LANES)).astype(jnp.int32)


def _out_proj(y_rnn, y_attn, x2, wo_bf, g2, rw_hi, rw_lo, rb, tri):
    n = x2.shape[0]
    n_tiles = n // T_TOK
    row = lambda w: pl.BlockSpec((T_TOK, w), lambda i: (i, 0))
    col = pl.BlockSpec((SUBLANES, T_TOK), lambda i: (0, i))
    const = lambda shape: pl.BlockSpec(shape, lambda i: (0,) * len(shape))
    return pl.pallas_call(
        _out_proj_kernel,
        grid=(n_tiles,),
        in_specs=[row(D_RNN), row(D_ATTN), row(D_MODEL),
                  const((2, D_RNN, D_MODEL)), const((1, D_MODEL)),
                  const((D_MODEL, LANES)), const((D_MODEL, LANES)), const((1, LANES)),
                  const((T_TOK, T_TOK))],
        out_specs=[row(D_MODEL), row(D_MODEL), col, col, pl.BlockSpec((N_EXPERTS, LANES), lambda i: (i, 0))],
        out_shape=[jax.ShapeDtypeStruct((n, D_MODEL), F32),
                   jax.ShapeDtypeStruct((n, D_MODEL), BF16),
                   jax.ShapeDtypeStruct((SUBLANES, n), jnp.int32),
                   jax.ShapeDtypeStruct((SUBLANES, n), F32),
                   jax.ShapeDtypeStruct((n_tiles * N_EXPERTS, LANES), jnp.int32)],
        compiler_params=_cparams(("arbitrary",)),
        name="out_proj",
    )(y_rnn, y_attn, x2, wo_bf, g2, rw_hi, rw_lo, rb, tri)


def _plan(tile_counts, n_blocks):
    n_tiles = tile_counts.shape[0]
    counts = jnp.sum(tile_counts, axis=0)
    padded = (counts + BM - 1) // BM * BM
    pad_end = jnp.cumsum(padded).astype(jnp.int32)
    pad_start = pad_end - padded
    earlier_tiles = jnp.cumsum(tile_counts, axis=0) - tile_counts
    run_off = (jnp.cumsum(tile_counts, axis=1) - tile_counts).astype(jnp.int32)
    run_dst = (pad_start[None, :] + earlier_tiles).astype(jnp.int32)
    off_lanes = jnp.broadcast_to(run_off.reshape(n_tiles * N_EXPERTS, 1), (n_tiles * N_EXPERTS, LANES))
    blk_start = jnp.arange(n_blocks, dtype=jnp.int32) * BM
    blk_e = jnp.minimum(jnp.sum((pad_end[None, :] <= blk_start[:, None]).astype(jnp.int32), axis=1), N_EXPERTS - 1)
    n_valid = (pad_end[-1] // BM).reshape(1)
    return (run_dst.reshape(-1), tile_counts.reshape(-1).astype(jnp.int32), run_off.reshape(-1), off_lanes,
            blk_e, n_valid, pad_end, padded)


def _tile_positions(route, off_col):
    eid = lax.broadcasted_iota(jnp.int32, (N_EXPERTS, route.shape[1]), 0)
    pos = []
    for k in range(TOP_K):
        start = jnp.sum(jnp.where(eid == route[k:k + 1, :], off_col, 0), axis=0, keepdims=True)
        pos.append(start + route[TOP_K + k:TOP_K + k + 1, :])
    return pos


def _run_copies(run_len_ref, tile, make_copy, act):
    def per_expert(e, carry):
        run = tile * N_EXPERTS + e
        length = run_len_ref[run]
        for b in range(RUN_BITS - 1, -1, -1):
            piece_start = (length >> (b + 1)) << (b + 1)

            @pl.when(((length >> b) & 1) == 1)
            def _():
                act(make_copy(run, piece_start, 1 << b))
        return carry

    lax.fori_loop(0, N_EXPERTS, per_expert, 0)


def _dispatch_kernel(run_dst_ref, run_len_ref, run_off_ref, pad_end_ref, padded_ref, n_valid_ref,
                     h_ref, route_ref, off_ref, xs_ref, sorted_ref, zeros_ref, sem, zsem):
    i = pl.program_id(0)
    n_tiles = pl.num_programs(0)
    slot = i % 2
    blk_rows = BM * ROW_TILES
    n_blocks = xs_ref.shape[0] // blk_rows

    def run_copy(s, run, piece_start, size):
        src_row = pl.multiple_of((run_off_ref[run] + piece_start) * ROW_TILES, ROW_TILES)
        dst_row = pl.multiple_of((run_dst_ref[run] + piece_start) * ROW_TILES, ROW_TILES)
        return pltpu.make_async_copy(sorted_ref.at[s, pl.ds(src_row, size * ROW_TILES), :],
                                     xs_ref.at[pl.ds(dst_row, size * ROW_TILES), :], sem.at[s])

    def start_runs(tile, s):
        _run_copies(run_len_ref, tile, functools.partial(run_copy, s), lambda cp: cp.start())

    def wait_runs(s):
        pltpu.make_async_copy(sorted_ref.at[s], xs_ref.at[pl.ds(0, sorted_ref.shape[1]), :], sem.at[s]).wait()

    @pl.when(i == 0)
    def _():
        zeros_ref[...] = jnp.zeros_like(zeros_ref)

        def zero_block(blk):
            start = pl.multiple_of(blk * blk_rows, blk_rows)
            cp = pltpu.make_async_copy(zeros_ref, xs_ref.at[pl.ds(start, blk_rows), :], zsem)
            cp.start()
            cp.wait()

        def zero_tail(e, carry):
            @pl.when(padded_ref[e] > 0)
            def _():
                zero_block(pad_end_ref[e] // BM - 1)
            return carry

        def zero_dead(b, carry):
            zero_block(b)
            return carry

        lax.fori_loop(0, N_EXPERTS, zero_tail, 0)
        lax.fori_loop(n_valid_ref[0], n_blocks, zero_dead, 0)

    @pl.when(i >= 2)
    def _():
        wait_runs(slot)

    pos = _tile_positions(route_ref[...], off_ref[:, 0:1])
    hb = h_ref[...]
    for c in range(TOP_K * T_TOK // P_ROWS):
        row = c * P_ROWS + lax.broadcasted_iota(jnp.int32, (P_ROWS, T_TOK), 0)
        hit = row == pos[0]
        for k in range(1, TOP_K):
            hit = hit | (row == pos[k])
        perm = jnp.where(hit, 1.0, 0.0).astype(BF16)
        rows = jnp.dot(perm, hb, preferred_element_type=F32)
        _to_tile_rows(sorted_ref.at[slot, pl.ds(c * P_ROWS * ROW_TILES, P_ROWS * ROW_TILES), :], rows)
    start_runs(i, slot)

    @pl.when(i == n_tiles - 1)
    def _():
        @pl.when(i >= 1)
        def _():
            wait_runs(1 - slot)
        wait_runs(slot)


def _dispatch(h2, route, off_lanes, run_dst, run_len, run_off, pad_end, padded, n_valid, n_rows):
    n = h2.shape[0]
    grid_spec = pltpu.PrefetchScalarGridSpec(
        num_scalar_prefetch=6,
        grid=(n // T_TOK,),
        in_specs=[
            pl.BlockSpec((T_TOK, D_MODEL), lambda i, *_: (i, 0)),
            pl.BlockSpec((SUBLANES, T_TOK), lambda i, *_: (0, i)),
            pl.BlockSpec((N_EXPERTS, LANES), lambda i, *_: (i, 0)),
        ],
        out_specs=pl.BlockSpec(memory_space=pl.ANY),
        scratch_shapes=[pltpu.VMEM((2, TOP_K * T_TOK * ROW_TILES, LANES), F32),
                        pltpu.VMEM((BM * ROW_TILES, LANES), F32),
                        pltpu.SemaphoreType.DMA((2,)), pltpu.SemaphoreType.DMA(())],
    )
    return pl.pallas_call(
        _dispatch_kernel,
        grid_spec=grid_spec,
        out_shape=jax.ShapeDtypeStruct((n_rows * ROW_TILES, LANES), F32),
        compiler_params=_cparams(("arbitrary",)),
        name="dispatch",
    )(run_dst, run_len, run_off, pad_end, padded, n_valid, h2, route, off_lanes)


def _experts_kernel(blk_e_ref, n_valid_ref, xs_ref, w1_ref, b1_ref, w2_ref, b2_ref, y_ref, w1b_ref, w2b_ref):
    i = pl.program_id(0)

    @pl.when(i < n_valid_ref[0])
    def _():
        prev_e = blk_e_ref[jnp.maximum(i - 1, 0)]

        @pl.when((i == 0) | (blk_e_ref[i] != prev_e))
        def _():
            w1b_ref[...] = w1_ref[0].astype(BF16)
            w2b_ref[...] = w2_ref[0].astype(BF16)

        x = _from_tile_rows(xs_ref, BM).astype(BF16)
        hcat = jnp.dot(x, w1b_ref[...], preferred_element_type=F32) + b1_ref[0]
        gate = jnp.minimum(hcat[:, :D_FF], SWIGLU_LIMIT)
        up = jnp.clip(hcat[:, D_FF:], -SWIGLU_LIMIT, SWIGLU_LIMIT)
        act = gate * jax.nn.sigmoid(SWIGLU_ALPHA * gate) * (up + 1.0)
        y = jnp.dot(act.astype(BF16), w2b_ref[...], preferred_element_type=F32) + b2_ref[0]
        _to_tile_rows(y_ref, y)

    @pl.when(i >= n_valid_ref[0])
    def _():
        y_ref[...] = jnp.zeros_like(y_ref)


def _experts(xs, blk_e, n_valid, w1, b1, w2, b2):
    blk_rows = BM * ROW_TILES
    n_blocks = xs.shape[0] // blk_rows
    exp3 = lambda i, be, nv: (be[i], 0, 0)
    grid_spec = pltpu.PrefetchScalarGridSpec(
        num_scalar_prefetch=2,
        grid=(n_blocks,),
        in_specs=[
            pl.BlockSpec((blk_rows, LANES), lambda i, be, nv: (jnp.minimum(i, nv[0] - 1), 0)),
            pl.BlockSpec((1, D_MODEL, 2 * D_FF), exp3),
            pl.BlockSpec((1, 1, 2 * D_FF), exp3),
            pl.BlockSpec((1, D_FF, D_MODEL), exp3),
            pl.BlockSpec((1, 1, D_MODEL), exp3),
        ],
        out_specs=pl.BlockSpec((blk_rows, LANES), lambda i, be, nv: (i, 0)),
        scratch_shapes=[pltpu.VMEM((D_MODEL, 2 * D_FF), BF16), pltpu.VMEM((D_FF, D_MODEL), BF16)],
    )
    return pl.pallas_call(
        _experts_kernel,
        grid_spec=grid_spec,
        out_shape=jax.ShapeDtypeStruct(xs.shape, F32),
        compiler_params=_cparams(("arbitrary",)),
        name="experts",
    )(blk_e, n_valid, xs, w1, b1, w2, b2)


def _combine_kernel(run_src_ref, run_len_ref, run_off_ref, x1_ref, route_ref, gates_ref, off_ref, ys_ref, o_ref,
                    buf_ref, sem):
    i = pl.program_id(0)
    n_tiles = pl.num_programs(0)
    slot = i % 2

    def run_copy(s, run, piece_start, size):
        src_row = pl.multiple_of((run_src_ref[run] + piece_start) * ROW_TILES, ROW_TILES)
        dst_row = pl.multiple_of((run_off_ref[run] + piece_start) * ROW_TILES, ROW_TILES)
        return pltpu.make_async_copy(ys_ref.at[pl.ds(src_row, size * ROW_TILES), :],
                                     buf_ref.at[s, pl.ds(dst_row, size * ROW_TILES), :], sem.at[s])

    def start_runs(tile, s):
        _run_copies(run_len_ref, tile, functools.partial(run_copy, s), lambda cp: cp.start())

    @pl.when(i == 0)
    def _():
        start_runs(0, 0)

    @pl.when(i + 1 < n_tiles)
    def _():
        start_runs(i + 1, 1 - slot)

    pltpu.make_async_copy(ys_ref.at[pl.ds(0, buf_ref.shape[1]), :], buf_ref.at[slot], sem.at[slot]).wait()

    pos = _tile_positions(route_ref[...], off_ref[:, 0:1])
    sub = lax.broadcasted_iota(jnp.int32, (SUBLANES, T_TOK), 0)
    gates = gates_ref[...]
    packed = jnp.zeros((SUBLANES, T_TOK), F32)
    for k in range(TOP_K):
        packed = jnp.where(sub == k, pos[k].astype(F32), packed)
        packed = jnp.where(sub == TOP_K + k, gates[k:k + 1, :], packed)
    cols = packed.T
    acc = x1_ref[...]
    for c in range(TOP_K * T_TOK // P_ROWS):
        row = (c * P_ROWS + lax.broadcasted_iota(jnp.int32, (T_TOK, P_ROWS), 1)).astype(F32)
        g = jnp.zeros((T_TOK, P_ROWS), F32)
        for k in range(TOP_K):
            g = g + jnp.where(row == cols[:, k:k + 1], cols[:, TOP_K + k:TOP_K + k + 1], 0.0)
        y = _from_tile_rows(buf_ref.at[slot, pl.ds(c * P_ROWS * ROW_TILES, P_ROWS * ROW_TILES), :], P_ROWS)
        acc = acc + jnp.dot(g.astype(BF16), y.astype(BF16), preferred_element_type=F32)
    o_ref[...] = acc


def _combine(x1, route, gates, off_lanes, run_dst, run_len, run_off, ys):
    n = x1.shape[0]
    grid_spec = pltpu.PrefetchScalarGridSpec(
        num_scalar_prefetch=3,
        grid=(n // T_TOK,),
        in_specs=[
            pl.BlockSpec((T_TOK, D_MODEL), lambda i, *_: (i, 0)),
            pl.BlockSpec((SUBLANES, T_TOK), lambda i, *_: (0, i)),
            pl.BlockSpec((SUBLANES, T_TOK), lambda i, *_: (0, i)),
            pl.BlockSpec((N_EXPERTS, LANES), lambda i, *_: (i, 0)),
            pl.BlockSpec(memory_space=pl.ANY),
        ],
        out_specs=pl.BlockSpec((T_TOK, D_MODEL), lambda i, *_: (i, 0)),
        scratch_shapes=[pltpu.VMEM((2, TOP_K * T_TOK * ROW_TILES, LANES), F32), pltpu.SemaphoreType.DMA((2,))],
    )
    return pl.pallas_call(
        _combine_kernel,
        grid_spec=grid_spec,
        out_shape=jax.ShapeDtypeStruct((n, D_MODEL), F32),
        compiler_params=_cparams(("arbitrary",)),
        name="combine",
    )(run_dst, run_len, run_off, x1, route, gates, off_lanes, ys)


def _block_diag(w):
    n, r, _ = w.shape
    eye = jnp.eye(n, dtype=w.dtype)
    return (eye[:, None, :, None] * w[:, :, None, :]).reshape(n * r, n * r)


def kernel(x, norm1_g, w_in, conv_w, conv_b, lru_wa, lru_ba, lru_wx, lru_bx, lru_lambda, q_norm_g, k_norm_g,
           lambda_q1, lambda_k1, lambda_q2, lambda_k2, subln_g, w_out, norm2_g, router_w, router_b, w1, b1, w2, b2):
    bsz, seq, d = x.shape
    n_tok = bsz * seq
    assert d == D_MODEL and n_tok % TM_PROJ == 0 and seq % T_SCAN == 0 and seq % TQ == 0 and n_tok % T_TOK == 0
    assert (n_tok * TOP_K) % BM == 0
    assert norm1_g.shape[0] == 1, "single-layer stack"
    x2 = x.reshape(n_tok, d)

    z = _in_proj(x2, norm1_g[0][None, :], w_in[0].astype(BF16))
    z3 = z.reshape(bsz, seq, D_IN)

    wg = jnp.concatenate([_block_diag(lru_wa[0]), _block_diag(lru_wx[0])], axis=1).astype(BF16)
    bg = jnp.concatenate([lru_ba[0], lru_bx[0]])[None, :]
    y_rnn = _rnn(z3, conv_w[0], conv_b[0][None, :], wg, bg, lru_lambda[0][None, :])

    half = jnp.arange(LANES) // HEAD_DIM
    ones_bd = (half[:, None] == half[None, :]).astype(BF16)
    y_attn = _attn(z3, jnp.tile(q_norm_g[0], 2)[None, :], jnp.tile(k_norm_g[0], 2)[None, :],
                   lambda_q1[0][None, :], lambda_k1[0][None, :], lambda_q2[0][None, :], lambda_k2[0][None, :],
                   subln_g[0][None, :], ones_bd)

    rw = jnp.pad(router_w[0], ((0, 0), (0, LANES - N_EXPERTS)))
    rw_hi = rw.astype(BF16)
    rw_lo = (rw - rw_hi.astype(F32)).astype(BF16)
    rb = jnp.pad(router_b[0], (0, LANES - N_EXPERTS))[None, :]
    tok = jnp.arange(T_TOK)
    tri = (tok[:, None] < tok[None, :]).astype(BF16)
    x1, h2, route, gates, counts = _out_proj(
        y_rnn.reshape(n_tok, D_RNN), y_attn.reshape(n_tok, D_ATTN), x2,
        w_out[0].astype(BF16).reshape(2, D_RNN, D_MODEL), norm2_g[0][None, :], rw_hi, rw_lo, rb, tri)

    n_blocks = (n_tok * TOP_K) // BM + N_EXPERTS
    tile_counts = counts[:, 0].reshape(n_tok // T_TOK, N_EXPERTS)
    run_dst, run_len, run_off, off_lanes, blk_e, n_valid, pad_end, padded = _plan(tile_counts, n_blocks)
    xs = _dispatch(h2, route, off_lanes, run_dst, run_len, run_off, pad_end, padded, n_valid, n_blocks * BM)
    ys = _experts(xs, blk_e, n_valid, w1[0], b1[0][:, None, :], w2[0], b2[0][:, None, :])
    out = _combine(x1, route, gates, off_lanes, run_dst, run_len, run_off, ys)
    return out.reshape(bsz, seq, d)
```

```python
import functools
import math

import jax
import jax.numpy as jnp
from jax import lax
from jax.experimental import pallas as pl
from jax.experimental.pallas import tpu as pltpu

F32 = jnp.float32
BF16 = jnp.bfloat16

D_MODEL = 1024
D_RNN = 512
N_RNN_BLOCKS = 8
RNN_BLOCK = 64
CONV_WIDTH = 4
LRU_C = 8.0
HEAD_DIM = 64
N_HEADS = 4
D_ATTN = 512
D_IN = 2 * D_RNN + 3 * D_ATTN
N_EXPERTS = 32
TOP_K = 4
D_FF = 1024
SWIGLU_LIMIT = 7.0
SWIGLU_ALPHA = 1.702
EPS = 1e-5
LAM_INIT = 0.8 - 0.6 * math.exp(0.0)

LANES = 128
SUBLANES = 8
ROW_TILES = D_MODEL // LANES
N_DMA_PRIORITIES = 2
VMEM_LIMIT = 52 * 1024 * 1024

TM_PROJ = 512
T_SCAN = 512
TQ = 256
BM = 512
T_TOK = 512
P_ROWS = 256
RUN_BITS = T_TOK.bit_length()
RUN_RARE_BIT = 7


def _cparams(sem):
    return pltpu.CompilerParams(dimension_semantics=sem, vmem_limit_bytes=VMEM_LIMIT)


def _to_tile_rows(ref, x):
    rows = x.shape[0]
    for s in range(ROW_TILES):
        ref[pl.ds(s, rows, stride=ROW_TILES), :] = x[:, s * LANES:(s + 1) * LANES]


def _from_tile_rows(ref, rows):
    return jnp.concatenate([ref[pl.ds(s, rows, stride=ROW_TILES), :] for s in range(ROW_TILES)], axis=1)


def _in_proj_kernel(x_ref, g_ref, w_ref, z_ref):
    x = x_ref[...]
    ms = jnp.mean(x * x, axis=-1, keepdims=True)
    h = x * lax.rsqrt(ms + EPS) * g_ref[...]
    z_ref[...] = jnp.dot(h.astype(BF16), w_ref[...], preferred_element_type=F32)


def _in_proj(x2, g, w_bf):
    n = x2.shape[0]
    return pl.pallas_call(
        _in_proj_kernel,
        grid=(n // TM_PROJ,),
        in_specs=[
            pl.BlockSpec((TM_PROJ, D_MODEL), lambda i: (i, 0)),
            pl.BlockSpec((1, D_MODEL), lambda i: (0, 0)),
            pl.BlockSpec((D_MODEL, D_IN), lambda i: (0, 0)),
        ],
        out_specs=pl.BlockSpec((TM_PROJ, D_IN), lambda i: (i, 0)),
        out_shape=jax.ShapeDtypeStruct((n, D_IN), F32),
        compiler_params=_cparams(("arbitrary",)),
        name="in_proj",
    )(x2, g, w_bf)


def _rnn_kernel(xr_ref, gr_ref, cw_ref, cb_ref, wg_ref, bg_ref, lam_ref, y_ref):
    seq = xr_ref.shape[1]
    n_chunks = seq // T_SCAN
    n_groups = T_SCAN // SUBLANES
    cw = cw_ref[...]
    cb = cb_ref[...]
    nl = -lam_ref[0]
    softplus_neg_lam = jnp.maximum(nl, 0.0) + jnp.log(1.0 + jnp.exp(-jnp.abs(nl)))
    group = lax.broadcasted_iota(jnp.int32, (n_groups, LANES), 0)

    def previous_group(v, first):
        return jnp.where(group >= 1, pltpu.roll(v, 1, axis=0), first)

    def phase_rows(t0, r):
        return pl.ds(t0 + r, n_groups, stride=SUBLANES)

    def chunk(c, carry):
        h_prev, x_tail = carry[0], carry[1:]
        t0 = pl.multiple_of(c * T_SCAN, T_SCAN)
        x = [xr_ref[0, phase_rows(t0, r), :] for r in range(SUBLANES)]
        wrapped = [previous_group(x[SUBLANES - j], x_tail[CONV_WIDTH - 1 - j]) for j in range(1, CONV_WIDTH)]

        def delayed(r, j):
            return x[r - j] if r >= j else wrapped[j - r - 1]

        conv = []
        for r in range(SUBLANES):
            acc = cb + cw[CONV_WIDTH - 1:CONV_WIDTH, :] * x[r]
            for j in range(1, CONV_WIDTH):
                acc = acc + cw[CONV_WIDTH - 1 - j:CONV_WIDTH - j, :] * delayed(r, j)
            conv.append(acc)
        conv = jnp.concatenate(conv, axis=0)
        gates = jnp.dot(conv.astype(BF16), wg_ref[0], preferred_element_type=F32) + bg_ref[0]
        rg = jax.nn.sigmoid(gates[:, :LANES])
        ig = jax.nn.sigmoid(gates[:, LANES:])
        a = jnp.exp(-LRU_C * rg * softplus_neg_lam)
        var = 1.0 - a * a
        u = jnp.where(var > 0.0, var * lax.rsqrt(var), 0.0) * (ig * conv)
        ph = lambda v, r: v[r * n_groups:(r + 1) * n_groups, :]
        a_in, u_in = [ph(a, 0)], [ph(u, 0)]
        for r in range(1, SUBLANES):
            a_in.append(ph(a, r) * a_in[-1])
            u_in.append(ph(a, r) * u_in[-1] + ph(u, r))
        ga, gu = a_in[-1], u_in[-1]
        d = 1
        while d < n_groups:
            keep = group >= d
            gu = jnp.where(keep, ga * pltpu.roll(gu, d, axis=0) + gu, gu)
            ga = jnp.where(keep, ga * pltpu.roll(ga, d, axis=0), ga)
            d *= 2
        h_after = ga * h_prev + gu
        h_before = previous_group(h_after, h_prev)
        for r in range(SUBLANES):
            h = a_in[r] * h_before + u_in[r]
            rows = phase_rows(t0, r)
            y_ref[0, rows, :] = h * jax.nn.gelu(gr_ref[0, rows, :], approximate=True)
        last = slice(n_groups - 1, n_groups)
        return (h_after[last, :],) + tuple(x[SUBLANES - CONV_WIDTH + 1 + j][last, :] for j in range(CONV_WIDTH - 1))

    zero = jnp.zeros((1, LANES), F32)
    lax.fori_loop(0, n_chunks, chunk, (zero,) * CONV_WIDTH)


def _rnn(z3, conv_w, conv_b, wg_slabs, bg_slabs, lam_slabs):
    bsz, seq, _ = z3.shape
    n_slabs = D_RNN // LANES
    slab = lambda rows: pl.BlockSpec((rows, LANES), lambda b, c: (0, c))
    slab3 = lambda shape: pl.BlockSpec((1,) + shape, lambda b, c: (c, 0, 0))
    return pl.pallas_call(
        _rnn_kernel,
        grid=(bsz, n_slabs),
        in_specs=[
            pl.BlockSpec((1, seq, LANES), lambda b, c: (b, 0, c)),
            pl.BlockSpec((1, seq, LANES), lambda b, c: (b, 0, n_slabs + c)),
            slab(CONV_WIDTH),
            slab(1),
            slab3((LANES, 2 * LANES)),
            slab3((1, 2 * LANES)),
            slab3((1, LANES)),
        ],
        out_specs=pl.BlockSpec((1, seq, LANES), lambda b, c: (b, 0, c)),
        out_shape=jax.ShapeDtypeStruct((bsz, seq, D_RNN), F32),
        compiler_params=_cparams(("arbitrary", "arbitrary")),
        name="rnn",
    )(z3, z3, conv_w, conv_b, wg_slabs, bg_slabs, lam_slabs)


def _group_rms(x, ones_bd):
    x2 = x * x
    hi = x2.astype(BF16)
    lo = (x2 - hi.astype(F32)).astype(BF16)
    ssq = jnp.dot(hi, ones_bd, preferred_element_type=F32) + jnp.dot(lo, ones_bd, preferred_element_type=F32)
    return x * lax.rsqrt(ssq * (1.0 / HEAD_DIM) + EPS)


def _attn_kernel(q_ref, k_ref, v_ref, qg_ref, kg_ref, lq1_ref, lk1_ref, lq2_ref, lk2_ref, sg_ref, ones_ref, o_ref):
    seq = q_ref.shape[1]
    ones_bd = ones_ref[...]
    lam = (jnp.exp(jnp.sum(lq1_ref[...] * lk1_ref[...], axis=-1, keepdims=True))
           - jnp.exp(jnp.sum(lq2_ref[...] * lk2_ref[...], axis=-1, keepdims=True)) + LAM_INIT)
    qn = _group_rms(q_ref[0], ones_bd) * qg_ref[...] * (HEAD_DIM ** -0.5)
    kn = _group_rms(k_ref[0], ones_bd) * kg_ref[...]
    lane = lax.broadcasted_iota(jnp.int32, (seq, LANES), 1)
    q1 = jnp.where(lane < HEAD_DIM, qn, 0.0).astype(BF16)
    q2 = jnp.where(lane >= HEAD_DIM, qn, 0.0).astype(BF16)
    kb = kn.astype(BF16)
    vb = v_ref[0].astype(BF16)
    sg = sg_ref[...]
    dn = (((1,), (1,)), ((), ()))
    for qi in range(seq // TQ):
        kv = (qi + 1) * TQ
        rows = slice(qi * TQ, kv)
        qpos = qi * TQ + lax.broadcasted_iota(jnp.int32, (TQ, kv), 0)
        kpos = lax.broadcasted_iota(jnp.int32, (TQ, kv), 1)
        causal = kpos <= qpos

        def probs(qm):
            s = lax.dot_general(qm[rows], kb[:kv], dn, preferred_element_type=F32)
            s = jnp.where(causal, s, -jnp.inf)
            e = jnp.exp(s - jnp.max(s, axis=-1, keepdims=True))
            return e, jnp.sum(e, axis=-1, keepdims=True)

        e1, l1 = probs(q1)
        e2, l2 = probs(q2)
        w = e1 * (1.0 / l1) - e2 * (lam / l2)
        o = jnp.dot(w.astype(BF16), vb[:kv], preferred_element_type=F32)
        o = o * lax.rsqrt(jnp.mean(o * o, axis=-1, keepdims=True) + EPS) * sg * (1.0 - LAM_INIT)
        o_ref[0, rows, :] = o


def _attn(z3, qg2, kg2, lq1, lk1, lq2, lk2, sg, ones_bd):
    bsz, seq, _ = z3.shape
    qoff = 2 * D_RNN // LANES
    koff = qoff + D_ATTN // LANES
    voff = koff + D_ATTN // LANES
    const = lambda shape: pl.BlockSpec(shape, lambda b, h: (0,) * len(shape))
    return pl.pallas_call(
        _attn_kernel,
        grid=(bsz, N_HEADS),
        in_specs=[
            pl.BlockSpec((1, seq, LANES), lambda b, h: (b, 0, qoff + h)),
            pl.BlockSpec((1, seq, LANES), lambda b, h: (b, 0, koff + h)),
            pl.BlockSpec((1, seq, LANES), lambda b, h: (b, 0, voff + h)),
            const((1, LANES)), const((1, LANES)),
            const((1, HEAD_DIM)), const((1, HEAD_DIM)), const((1, HEAD_DIM)), const((1, HEAD_DIM)),
            const((1, LANES)), const((LANES, LANES)),
        ],
        out_specs=pl.BlockSpec((1, seq, LANES), lambda b, h: (b, 0, h)),
        out_shape=jax.ShapeDtypeStruct((bsz, seq, D_ATTN), F32),
        compiler_params=_cparams(("arbitrary", "arbitrary")),
        name="attn",
    )(z3, z3, z3, qg2, kg2, lq1, lk1, lq2, lk2, sg, ones_bd)


def _out_proj_kernel(yr_ref, ya_ref, x_ref, wo_ref, g_ref, rwh_ref, rwl_ref, rb_ref, tri_ref,
                     x1_ref, h2_ref, route_ref, gates_ref, counts_ref):
    acc = jnp.dot(yr_ref[...].astype(BF16), wo_ref[0], preferred_element_type=F32)
    acc = acc + jnp.dot(ya_ref[...].astype(BF16), wo_ref[1], preferred_element_type=F32)
    x1 = x_ref[...] + acc
    x1_ref[...] = x1
    h2 = x1 * lax.rsqrt(jnp.mean(x1 * x1, axis=-1, keepdims=True) + EPS) * g_ref[...]
    hh = h2.astype(BF16)
    h2_ref[...] = hh
    hl = (h2 - hh.astype(F32)).astype(BF16)
    logits = (jnp.dot(hh, rwh_ref[...], preferred_element_type=F32)
              + jnp.dot(hl, rwh_ref[...], preferred_element_type=F32)
              + jnp.dot(hh, rwl_ref[...], preferred_element_type=F32)) + rb_ref[...]
    tm = logits.shape[0]
    l = logits.T[:N_EXPERTS, :]
    eid = lax.broadcasted_iota(jnp.int32, (N_EXPERTS, tm), 0)
    vals, idxs = [], []
    for _ in range(TOP_K):
        m = jnp.max(l, axis=0, keepdims=True)
        idx = jnp.min(jnp.where(l == m, eid, N_EXPERTS), axis=0, keepdims=True)
        vals.append(m)
        idxs.append(idx)
        l = jnp.where(eid == idx, -jnp.inf, l)
    es = [jnp.exp(v - vals[0]) for v in vals]
    inv = 1.0 / (es[0] + es[1] + es[2] + es[3])
    chosen = jnp.zeros((N_EXPERTS, tm), F32)
    for k in range(TOP_K):
        chosen = chosen + (eid == idxs[k]).astype(F32)
    before = jnp.dot(chosen.astype(BF16), tri_ref[...], preferred_element_type=F32)
    sub = lax.broadcasted_iota(jnp.int32, (SUBLANES, tm), 0)
    route = jnp.zeros((SUBLANES, tm), jnp.int32)
    gates = jnp.zeros((SUBLANES, tm), F32)
    for k in range(TOP_K):
        rank = jnp.sum(jnp.where(eid == idxs[k], before, 0.0), axis=0, keepdims=True).astype(jnp.int32)
        route = jnp.where(sub == k, idxs[k], route)
        route = jnp.where(sub == TOP_K + k, rank, route)
        gates = jnp.where(sub == k, es[k] * inv, gates)
    route_ref[...] = route
    gates_ref[...] = gates
    counts_ref[...] = jnp.broadcast_to(jnp.sum(chosen, axis=1, keepdims=True), (N_EXPERTS, LANES)).astype(jnp.int32)


def _out_proj(y_rnn, y_attn, x2, wo_bf, g2, rw_hi, rw_lo, rb, tri):
    n = x2.shape[0]
    n_tiles = n // T_TOK
    row = lambda w: pl.BlockSpec((T_TOK, w), lambda i: (i, 0))
    col = pl.BlockSpec((SUBLANES, T_TOK), lambda i: (0, i))
    const = lambda shape: pl.BlockSpec(shape, lambda i: (0,) * len(shape))
    return pl.pallas_call(
        _out_proj_kernel,
        grid=(n_tiles,),
        in_specs=[row(D_RNN), row(D_ATTN), row(D_MODEL),
                  const((2, D_RNN, D_MODEL)), const((1, D_MODEL)),
                  const((D_MODEL, LANES)), const((D_MODEL, LANES)), const((1, LANES)),
                  const((T_TOK, T_TOK))],
        out_specs=[row(D_MODEL), row(D_MODEL), col, col, pl.BlockSpec((N_EXPERTS, LANES), lambda i: (i, 0))],
        out_shape=[jax.ShapeDtypeStruct((n, D_MODEL), F32),
                   jax.ShapeDtypeStruct((n, D_MODEL), BF16),
                   jax.ShapeDtypeStruct((SUBLANES, n), jnp.int32),
                   jax.ShapeDtypeStruct((SUBLANES, n), F32),
                   jax.ShapeDtypeStruct((n_tiles * N_EXPERTS, LANES), jnp.int32)],
        compiler_params=_cparams(("arbitrary",)),
        name="out_proj",
    )(y_rnn, y_attn, x2, wo_bf, g2, rw_hi, rw_lo, rb, tri)


def _plan(tile_counts, n_blocks):
    n_tiles = tile_counts.shape[0]
    counts = jnp.sum(tile_counts, axis=0)
    padded = (counts + BM - 1) // BM * BM
    pad_end = jnp.cumsum(padded).astype(jnp.int32)
    pad_start = pad_end - padded
    earlier_tiles = jnp.cumsum(tile_counts, axis=0) - tile_counts
    run_off = (jnp.cumsum(tile_counts, axis=1) - tile_counts).astype(jnp.int32)
    run_dst = (pad_start[None, :] + earlier_tiles).astype(jnp.int32)
    off_lanes = jnp.broadcast_to(run_off.reshape(n_tiles * N_EXPERTS, 1), (n_tiles * N_EXPERTS, LANES))
    blk_start = jnp.arange(n_blocks, dtype=jnp.int32) * BM
    blk_e = jnp.minimum(jnp.sum((pad_end[None, :] <= blk_start[:, None]).astype(jnp.int32), axis=1), N_EXPERTS - 1)
    n_valid = (pad_end[-1] // BM).reshape(1)
    return (run_dst.reshape(-1), tile_counts.reshape(-1).astype(jnp.int32), run_off.reshape(-1), off_lanes,
            blk_e, n_valid, pad_end, padded)


def _tile_positions(route, off_col):
    eid = lax.broadcasted_iota(jnp.int32, (N_EXPERTS, route.shape[1]), 0)
    pos = []
    for k in range(TOP_K):
        start = jnp.sum(jnp.where(eid == route[k:k + 1, :], off_col, 0), axis=0, keepdims=True)
        pos.append(start + route[TOP_K + k:TOP_K + k + 1, :])
    return pos


def _run_copies(run_len_ref, tile, make_copy, act):
    def per_expert(e, carry):
        run = tile * N_EXPERTS + e
        length = run_len_ref[run]

        def pieces(bits):
            for b in bits:
                piece_start = (length >> (b + 1)) << (b + 1)

                @pl.when(((length >> b) & 1) == 1)
                def _():
                    act(make_copy(run, piece_start, 1 << b))

        @pl.when(length >= (1 << RUN_RARE_BIT))
        def _():
            pieces(range(RUN_BITS - 1, RUN_RARE_BIT - 1, -1))

        pieces(range(RUN_RARE_BIT - 1, -1, -1))
        return carry

    lax.fori_loop(0, N_EXPERTS, per_expert, 0)


def _dispatch_kernel(run_dst_ref, run_len_ref, run_off_ref, pad_end_ref, padded_ref, n_valid_ref,
                     h_ref, route_ref, off_ref, xs_ref, sorted_ref, zeros_ref, sem, zsem):
    i = pl.program_id(0)
    n_tiles = pl.num_programs(0)
    slot = i % 2
    blk_rows = BM * ROW_TILES
    n_blocks = xs_ref.shape[0] // blk_rows

    def run_copy(s, run, piece_start, size):
        src_row = pl.multiple_of((run_off_ref[run] + piece_start) * ROW_TILES, ROW_TILES)
        dst_row = pl.multiple_of((run_dst_ref[run] + piece_start) * ROW_TILES, ROW_TILES)
        return pltpu.make_async_copy(sorted_ref.at[s, pl.ds(src_row, size * ROW_TILES), :],
                                     xs_ref.at[pl.ds(dst_row, size * ROW_TILES), :], sem.at[s])

    def start_runs(tile, s):
        _run_copies(run_len_ref, tile, functools.partial(run_copy, s), lambda cp: cp.start())

    def wait_runs(s):
        pltpu.make_async_copy(sorted_ref.at[s], xs_ref.at[pl.ds(0, sorted_ref.shape[1]), :], sem.at[s]).wait()

    @pl.when(i == 0)
    def _():
        zeros_ref[...] = jnp.zeros_like(zeros_ref)

        def zero_block(blk):
            start = pl.multiple_of(blk * blk_rows, blk_rows)
            cp = pltpu.make_async_copy(zeros_ref, xs_ref.at[pl.ds(start, blk_rows), :], zsem)
            cp.start()
            cp.wait()

        def zero_tail(e, carry):
            @pl.when(padded_ref[e] > 0)
            def _():
                zero_block(pad_end_ref[e] // BM - 1)
            return carry

        def zero_dead(b, carry):
            zero_block(b)
            return carry

        lax.fori_loop(0, N_EXPERTS, zero_tail, 0)
        lax.fori_loop(n_valid_ref[0], n_blocks, zero_dead, 0)

    @pl.when(i >= 2)
    def _():
        wait_runs(slot)

    pos = _tile_positions(route_ref[...], off_ref[:, 0:1])
    hb = h_ref[...]
    for c in range(TOP_K * T_TOK // P_ROWS):
        row = c * P_ROWS + lax.broadcasted_iota(jnp.int32, (P_ROWS, T_TOK), 0)
        hit = row == pos[0]
        for k in range(1, TOP_K):
            hit = hit | (row == pos[k])
        perm = jnp.where(hit, 1.0, 0.0).astype(BF16)
        rows = jnp.dot(perm, hb, preferred_element_type=F32)
        _to_tile_rows(sorted_ref.at[slot, pl.ds(c * P_ROWS * ROW_TILES, P_ROWS * ROW_TILES), :], rows)
    start_runs(i, slot)

    @pl.when(i == n_tiles - 1)
    def _():
        @pl.when(i >= 1)
        def _():
            wait_runs(1 - slot)
        wait_runs(slot)


def _dispatch(h2, route, off_lanes, run_dst, run_len, run_off, pad_end, padded, n_valid, n_rows):
    n = h2.shape[0]
    grid_spec = pltpu.PrefetchScalarGridSpec(
        num_scalar_prefetch=6,
        grid=(n // T_TOK,),
        in_specs=[
            pl.BlockSpec((T_TOK, D_MODEL), lambda i, *_: (i, 0)),
            pl.BlockSpec((SUBLANES, T_TOK), lambda i, *_: (0, i)),
            pl.BlockSpec((N_EXPERTS, LANES), lambda i, *_: (i, 0)),
        ],
        out_specs=pl.BlockSpec(memory_space=pl.ANY),
        scratch_shapes=[pltpu.VMEM((2, TOP_K * T_TOK * ROW_TILES, LANES), F32),
                        pltpu.VMEM((BM * ROW_TILES, LANES), F32),
                        pltpu.SemaphoreType.DMA((2,)), pltpu.SemaphoreType.DMA(())],
    )
    return pl.pallas_call(
        _dispatch_kernel,
        grid_spec=grid_spec,
        out_shape=jax.ShapeDtypeStruct((n_rows * ROW_TILES, LANES), F32),
        compiler_params=_cparams(("arbitrary",)),
        name="dispatch",
    )(run_dst, run_len, run_off, pad_end, padded, n_valid, h2, route, off_lanes)


def _experts_kernel(blk_e_ref, n_valid_ref, xs_ref, w1_ref, b1_ref, w2_ref, b2_ref, y_ref, w1b_ref, w2b_ref):
    i = pl.program_id(0)

    @pl.when(i < n_valid_ref[0])
    def _():
        prev_e = blk_e_ref[jnp.maximum(i - 1, 0)]

        @pl.when((i == 0) | (blk_e_ref[i] != prev_e))
        def _():
            w1b_ref[...] = w1_ref[0].astype(BF16)
            w2b_ref[...] = w2_ref[0].astype(BF16)

        x = _from_tile_rows(xs_ref, BM).astype(BF16)
        hcat = jnp.dot(x, w1b_ref[...], preferred_element_type=F32) + b1_ref[0]
        gate = jnp.minimum(hcat[:, :D_FF], SWIGLU_LIMIT)
        up = jnp.clip(hcat[:, D_FF:], -SWIGLU_LIMIT, SWIGLU_LIMIT)
        act = gate * jax.nn.sigmoid(SWIGLU_ALPHA * gate) * (up + 1.0)
        y = jnp.dot(act.astype(BF16), w2b_ref[...], preferred_element_type=F32) + b2_ref[0]
        _to_tile_rows(y_ref, y)

    @pl.when(i >= n_valid_ref[0])
    def _():
        y_ref[...] = jnp.zeros_like(y_ref)


def _experts(xs, blk_e, n_valid, w1, b1, w2, b2):
    blk_rows = BM * ROW_TILES
    n_blocks = xs.shape[0] // blk_rows
    exp3 = lambda i, be, nv: (be[i], 0, 0)
    grid_spec = pltpu.PrefetchScalarGridSpec(
        num_scalar_prefetch=2,
        grid=(n_blocks,),
        in_specs=[
            pl.BlockSpec((blk_rows, LANES), lambda i, be, nv: (jnp.minimum(i, nv[0] - 1), 0)),
            pl.BlockSpec((1, D_MODEL, 2 * D_FF), exp3),
            pl.BlockSpec((1, 1, 2 * D_FF), exp3),
            pl.BlockSpec((1, D_FF, D_MODEL), exp3),
            pl.BlockSpec((1, 1, D_MODEL), exp3),
        ],
        out_specs=pl.BlockSpec((blk_rows, LANES), lambda i, be, nv: (i, 0)),
        scratch_shapes=[pltpu.VMEM((D_MODEL, 2 * D_FF), BF16), pltpu.VMEM((D_FF, D_MODEL), BF16)],
    )
    return pl.pallas_call(
        _experts_kernel,
        grid_spec=grid_spec,
        out_shape=jax.ShapeDtypeStruct(xs.shape, F32),
        compiler_params=_cparams(("arbitrary",)),
        name="experts",
    )(blk_e, n_valid, xs, w1, b1, w2, b2)


def _combine_kernel(run_src_ref, run_len_ref, run_off_ref, x1_ref, route_ref, gates_ref, off_ref, ys_ref, o_ref,
                    buf_ref, sem):
    i = pl.program_id(0)
    n_tiles = pl.num_programs(0)
    slot = i % 2

    def run_copy(s, run, piece_start, size):
        src_row = pl.multiple_of((run_src_ref[run] + piece_start) * ROW_TILES, ROW_TILES)
        dst_row = pl.multiple_of((run_off_ref[run] + piece_start) * ROW_TILES, ROW_TILES)
        return pltpu.make_async_copy(ys_ref.at[pl.ds(src_row, size * ROW_TILES), :],
                                     buf_ref.at[s, pl.ds(dst_row, size * ROW_TILES), :], sem.at[s])

    def start_runs(tile, s):
        _run_copies(run_len_ref, tile, functools.partial(run_copy, s), lambda cp: cp.start())

    @pl.when(i == 0)
    def _():
        start_runs(0, 0)

    @pl.when(i + 1 < n_tiles)
    def _():
        start_runs(i + 1, 1 - slot)

    pltpu.make_async_copy(ys_ref.at[pl.ds(0, buf_ref.shape[1]), :], buf_ref.at[slot], sem.at[slot]).wait()

    pos = _tile_positions(route_ref[...], off_ref[:, 0:1])
    sub = lax.broadcasted_iota(jnp.int32, (SUBLANES, T_TOK), 0)
    gates = gates_ref[...]
    packed = jnp.zeros((SUBLANES, T_TOK), F32)
    for k in range(TOP_K):
        packed = jnp.where(sub == k, pos[k].astype(F32), packed)
        packed = jnp.where(sub == TOP_K + k, gates[k:k + 1, :], packed)
    cols = packed.T
    acc = x1_ref[...]
    for c in range(TOP_K * T_TOK // P_ROWS):
        row = (c * P_ROWS + lax.broadcasted_iota(jnp.int32, (T_TOK, P_ROWS), 1)).astype(F32)
        g = jnp.zeros((T_TOK, P_ROWS), F32)
        for k in range(TOP_K):
            g = g + jnp.where(row == cols[:, k:k + 1], cols[:, TOP_K + k:TOP_K + k + 1], 0.0)
        y = _from_tile_rows(buf_ref.at[slot, pl.ds(c * P_ROWS * ROW_TILES, P_ROWS * ROW_TILES), :], P_ROWS)
        acc = acc + jnp.dot(g.astype(BF16), y.astype(BF16), preferred_element_type=F32)
    o_ref[...] = acc


def _combine(x1, route, gates, off_lanes, run_dst, run_len, run_off, ys):
    n = x1.shape[0]
    grid_spec = pltpu.PrefetchScalarGridSpec(
        num_scalar_prefetch=3,
        grid=(n // T_TOK,),
        in_specs=[
            pl.BlockSpec((T_TOK, D_MODEL), lambda i, *_: (i, 0)),
            pl.BlockSpec((SUBLANES, T_TOK), lambda i, *_: (0, i)),
            pl.BlockSpec((SUBLANES, T_TOK), lambda i, *_: (0, i)),
            pl.BlockSpec((N_EXPERTS, LANES), lambda i, *_: (i, 0)),
            pl.BlockSpec(memory_space=pl.ANY),
        ],
        out_specs=pl.BlockSpec((T_TOK, D_MODEL), lambda i, *_: (i, 0)),
        scratch_shapes=[pltpu.VMEM((2, TOP_K * T_TOK * ROW_TILES, LANES), F32), pltpu.SemaphoreType.DMA((2,))],
    )
    return pl.pallas_call(
        _combine_kernel,
        grid_spec=grid_spec,
        out_shape=jax.ShapeDtypeStruct((n, D_MODEL), F32),
        compiler_params=_cparams(("arbitrary",)),
        name="combine",
    )(run_dst, run_len, run_off, x1, route, gates, off_lanes, ys)


def _block_diag(w):
    n, r, _ = w.shape
    eye = jnp.eye(n, dtype=w.dtype)
    return (eye[:, None, :, None] * w[:, :, None, :]).reshape(n * r, n * r)


def kernel(x, norm1_g, w_in, conv_w, conv_b, lru_wa, lru_ba, lru_wx, lru_bx, lru_lambda, q_norm_g, k_norm_g,
           lambda_q1, lambda_k1, lambda_q2, lambda_k2, subln_g, w_out, norm2_g, router_w, router_b, w1, b1, w2, b2):
    bsz, seq, d = x.shape
    n_tok = bsz * seq
    assert d == D_MODEL and n_tok % TM_PROJ == 0 and seq % T_SCAN == 0 and seq % TQ == 0 and n_tok % T_TOK == 0
    assert (n_tok * TOP_K) % BM == 0
    assert norm1_g.shape[0] == 1, "single-layer stack"
    x2 = x.reshape(n_tok, d)

    z = _in_proj(x2, norm1_g[0][None, :], w_in[0].astype(BF16))
    z3 = z.reshape(bsz, seq, D_IN)

    n_slabs = D_RNN // LANES
    per_slab = LANES // RNN_BLOCK
    wa = lru_wa[0].reshape(n_slabs, per_slab, RNN_BLOCK, RNN_BLOCK)
    wx = lru_wx[0].reshape(n_slabs, per_slab, RNN_BLOCK, RNN_BLOCK)
    wg = jnp.concatenate([jax.vmap(_block_diag)(wa), jax.vmap(_block_diag)(wx)], axis=2).astype(BF16)
    bg = jnp.concatenate([lru_ba[0].reshape(n_slabs, 1, LANES), lru_bx[0].reshape(n_slabs, 1, LANES)], axis=2)
    y_rnn = _rnn(z3, conv_w[0], conv_b[0][None, :], wg, bg, lru_lambda[0].reshape(n_slabs, 1, LANES))

    half = jnp.arange(LANES) // HEAD_DIM
    ones_bd = (half[:, None] == half[None, :]).astype(BF16)
    y_attn = _attn(z3, jnp.tile(q_norm_g[0], 2)[None, :], jnp.tile(k_norm_g[0], 2)[None, :],
                   lambda_q1[0][None, :], lambda_k1[0][None, :], lambda_q2[0][None, :], lambda_k2[0][None, :],
                   subln_g[0][None, :], ones_bd)

    rw = jnp.pad(router_w[0], ((0, 0), (0, LANES - N_EXPERTS)))
    rw_hi = rw.astype(BF16)
    rw_lo = (rw - rw_hi.astype(F32)).astype(BF16)
    rb = jnp.pad(router_b[0], (0, LANES - N_EXPERTS))[None, :]
    tok = jnp.arange(T_TOK)
    tri = (tok[:, None] < tok[None, :]).astype(BF16)
    x1, h2, route, gates, counts = _out_proj(
        y_rnn.reshape(n_tok, D_RNN), y_attn.reshape(n_tok, D_ATTN), x2,
        w_out[0].astype(BF16).reshape(2, D_RNN, D_MODEL), norm2_g[0][None, :], rw_hi, rw_lo, rb, tri)

    n_blocks = (n_tok * TOP_K) // BM + N_EXPERTS
    tile_counts = counts[:, 0].reshape(n_tok // T_TOK, N_EXPERTS)
    run_dst, run_len, run_off, off_lanes, blk_e, n_valid, pad_end, padded = _plan(tile_counts, n_blocks)
    xs = _dispatch(h2, route, off_lanes, run_dst, run_len, run_off, pad_end, padded, n_valid, n_blocks * BM)
    ys = _experts(xs, blk_e, n_valid, w1[0], b1[0][:, None, :], w2[0], b2[0][:, None, :])
    out = _combine(x1, route, gates, off_lanes, run_dst, run_len, run_off, ys)
    return out.reshape(bsz, seq, d)
```

```python
import functools
import math

import jax
import jax.numpy as jnp
from jax import lax
from jax.experimental import pallas as pl
from jax.experimental.pallas import tpu as pltpu

F32 = jnp.float32
BF16 = jnp.bfloat16

D_MODEL = 1024
D_RNN = 512
N_RNN_BLOCKS = 8
RNN_BLOCK = 64
CONV_WIDTH = 4
LRU_C = 8.0
HEAD_DIM = 64
N_HEADS = 4
D_ATTN = 512
D_IN = 2 * D_RNN + 3 * D_ATTN
N_EXPERTS = 32
TOP_K = 4
D_FF = 1024
SWIGLU_LIMIT = 7.0
SWIGLU_ALPHA = 1.702
EPS = 1e-5
LAM_INIT = 0.8 - 0.6 * math.exp(0.0)
LOG2_E = math.log2(math.e)

LANES = 128
SUBLANES = 8
ROW_TILES = D_MODEL // LANES
N_DMA_PRIORITIES = 2
VMEM_LIMIT = 52 * 1024 * 1024

TM_PROJ = 512
T_SCAN = 512
TQ = 128
BM = 512
T_TOK = 512
P_ROWS = 256
RUN_BITS = T_TOK.bit_length()
RUN_RARE_BIT = 7


def _cparams(sem):
    return pltpu.CompilerParams(dimension_semantics=sem, vmem_limit_bytes=VMEM_LIMIT)


def _to_tile_rows(ref, x):
    rows = x.shape[0]
    for s in range(ROW_TILES):
        ref[pl.ds(s, rows, stride=ROW_TILES), :] = x[:, s * LANES:(s + 1) * LANES]


def _from_tile_rows(ref, rows):
    return jnp.concatenate([ref[pl.ds(s, rows, stride=ROW_TILES), :] for s in range(ROW_TILES)], axis=1)


def _in_proj_kernel(x_ref, g_ref, w_ref, z_ref):
    x = x_ref[...]
    ms = jnp.mean(x * x, axis=-1, keepdims=True)
    h = x * lax.rsqrt(ms + EPS) * g_ref[...]
    z_ref[...] = jnp.dot(h.astype(BF16), w_ref[...], preferred_element_type=F32)


def _in_proj(x2, g, w_bf):
    n = x2.shape[0]
    return pl.pallas_call(
        _in_proj_kernel,
        grid=(n // TM_PROJ,),
        in_specs=[
            pl.BlockSpec((TM_PROJ, D_MODEL), lambda i: (i, 0)),
            pl.BlockSpec((1, D_MODEL), lambda i: (0, 0)),
            pl.BlockSpec((D_MODEL, D_IN), lambda i: (0, 0)),
        ],
        out_specs=pl.BlockSpec((TM_PROJ, D_IN), lambda i: (i, 0)),
        out_shape=jax.ShapeDtypeStruct((n, D_IN), F32),
        compiler_params=_cparams(("arbitrary",)),
        name="in_proj",
    )(x2, g, w_bf)


def _rnn_kernel(xr_ref, gr_ref, cw_ref, cb_ref, wg_ref, bg_ref, lam_ref, y_ref):
    seq = xr_ref.shape[1]
    n_chunks = seq // T_SCAN
    n_groups = T_SCAN // SUBLANES
    cw = cw_ref[...]
    cb = cb_ref[...]
    nl = -lam_ref[0]
    softplus_neg_lam = jnp.maximum(nl, 0.0) + jnp.log(1.0 + jnp.exp(-jnp.abs(nl)))
    group = lax.broadcasted_iota(jnp.int32, (n_groups, LANES), 0)

    def previous_group(v, first):
        return jnp.where(group >= 1, pltpu.roll(v, 1, axis=0), first)

    def phase_rows(t0, r):
        return pl.ds(t0 + r, n_groups, stride=SUBLANES)

    def chunk(c, carry):
        h_prev, x_tail = carry[0], carry[1:]
        t0 = pl.multiple_of(c * T_SCAN, T_SCAN)
        x = [xr_ref[0, phase_rows(t0, r), :] for r in range(SUBLANES)]
        wrapped = [previous_group(x[SUBLANES - j], x_tail[CONV_WIDTH - 1 - j]) for j in range(1, CONV_WIDTH)]

        def delayed(r, j):
            return x[r - j] if r >= j else wrapped[j - r - 1]

        conv = []
        for r in range(SUBLANES):
            acc = cb + cw[CONV_WIDTH - 1:CONV_WIDTH, :] * x[r]
            for j in range(1, CONV_WIDTH):
                acc = acc + cw[CONV_WIDTH - 1 - j:CONV_WIDTH - j, :] * delayed(r, j)
            conv.append(acc)
        conv = jnp.concatenate(conv, axis=0)
        gates = jnp.dot(conv.astype(BF16), wg_ref[0], preferred_element_type=F32) + bg_ref[0]
        rg = jax.nn.sigmoid(gates[:, :LANES])
        ig = jax.nn.sigmoid(gates[:, LANES:])
        a = jnp.exp(-LRU_C * rg * softplus_neg_lam)
        var = 1.0 - a * a
        u = jnp.where(var > 0.0, var * lax.rsqrt(var), 0.0) * (ig * conv)
        ph = lambda v, r: v[r * n_groups:(r + 1) * n_groups, :]
        a_in, u_in = [ph(a, 0)], [ph(u, 0)]
        for r in range(1, SUBLANES):
            a_in.append(ph(a, r) * a_in[-1])
            u_in.append(ph(a, r) * u_in[-1] + ph(u, r))
        ga, gu = a_in[-1], u_in[-1]
        d = 1
        while d < n_groups:
            keep = group >= d
            gu = jnp.where(keep, ga * pltpu.roll(gu, d, axis=0) + gu, gu)
            ga = jnp.where(keep, ga * pltpu.roll(ga, d, axis=0), ga)
            d *= 2
        h_after = ga * h_prev + gu
        h_before = previous_group(h_after, h_prev)
        for r in range(SUBLANES):
            h = a_in[r] * h_before + u_in[r]
            rows = phase_rows(t0, r)
            y_ref[0, rows, :] = h * jax.nn.gelu(gr_ref[0, rows, :], approximate=True)
        last = slice(n_groups - 1, n_groups)
        return (h_after[last, :],) + tuple(x[SUBLANES - CONV_WIDTH + 1 + j][last, :] for j in range(CONV_WIDTH - 1))

    zero = jnp.zeros((1, LANES), F32)
    lax.fori_loop(0, n_chunks, chunk, (zero,) * CONV_WIDTH)


def _rnn(z3, conv_w, conv_b, wg_slabs, bg_slabs, lam_slabs):
    bsz, seq, _ = z3.shape
    n_slabs = D_RNN // LANES
    slab = lambda rows: pl.BlockSpec((rows, LANES), lambda b, c: (0, c))
    slab3 = lambda shape: pl.BlockSpec((1,) + shape, lambda b, c: (c, 0, 0))
    return pl.pallas_call(
        _rnn_kernel,
        grid=(bsz, n_slabs),
        in_specs=[
            pl.BlockSpec((1, seq, LANES), lambda b, c: (b, 0, c)),
            pl.BlockSpec((1, seq, LANES), lambda b, c: (b, 0, n_slabs + c)),
            slab(CONV_WIDTH),
            slab(1),
            slab3((LANES, 2 * LANES)),
            slab3((1, 2 * LANES)),
            slab3((1, LANES)),
        ],
        out_specs=pl.BlockSpec((1, seq, LANES), lambda b, c: (b, 0, c)),
        out_shape=jax.ShapeDtypeStruct((bsz, seq, D_RNN), F32),
        compiler_params=_cparams(("arbitrary", "arbitrary")),
        name="rnn",
    )(z3, z3, conv_w, conv_b, wg_slabs, bg_slabs, lam_slabs)


def _group_rms(x, ones_bd):
    x2 = x * x
    hi = x2.astype(BF16)
    lo = (x2 - hi.astype(F32)).astype(BF16)
    ssq = jnp.dot(hi, ones_bd, preferred_element_type=F32) + jnp.dot(lo, ones_bd, preferred_element_type=F32)
    return x * lax.rsqrt(ssq * (1.0 / HEAD_DIM) + EPS)


def _attn_kernel(q_ref, k_ref, v_ref, qg_ref, kg_ref, lq1_ref, lk1_ref, lq2_ref, lk2_ref, sg_ref, ones_ref, o_ref):
    seq = q_ref.shape[1]
    ones_bd = ones_ref[...]
    lam = (jnp.exp(jnp.sum(lq1_ref[...] * lk1_ref[...], axis=-1, keepdims=True))
           - jnp.exp(jnp.sum(lq2_ref[...] * lk2_ref[...], axis=-1, keepdims=True)) + LAM_INIT)
    qn = _group_rms(q_ref[0], ones_bd) * qg_ref[...] * (HEAD_DIM ** -0.5 * LOG2_E)
    kn = _group_rms(k_ref[0], ones_bd) * kg_ref[...]
    lane = lax.broadcasted_iota(jnp.int32, (seq, LANES), 1)
    q1 = jnp.where(lane < HEAD_DIM, qn, 0.0).astype(BF16)
    q2 = jnp.where(lane >= HEAD_DIM, qn, 0.0).astype(BF16)
    kb = kn.astype(BF16)
    vb = v_ref[0].astype(BF16)
    sg = sg_ref[...]
    dn = (((1,), (1,)), ((), ()))
    n_blk = seq // TQ

    def scores(qi, qm):
        kv = (qi + 1) * TQ
        return lax.dot_general(qm[qi * TQ:kv], kb[:kv], dn, preferred_element_type=F32)

    causal = (lax.broadcasted_iota(jnp.int32, (TQ, TQ), 1) <= lax.broadcasted_iota(jnp.int32, (TQ, TQ), 0))

    def weights(qi, s1, s2):
        c0 = qi * TQ

        def probs(s):
            diag = jnp.where(causal, s[:, c0:], -jnp.inf)
            m = jnp.max(diag, axis=-1, keepdims=True)
            if qi > 0:
                m = jnp.maximum(m, jnp.max(s[:, :c0], axis=-1, keepdims=True))
            e = jnp.exp2(diag - m)
            l = jnp.sum(e, axis=-1, keepdims=True)
            if qi > 0:
                e_prev = jnp.exp2(s[:, :c0] - m)
                l = l + jnp.sum(e_prev, axis=-1, keepdims=True)
                e = jnp.concatenate([e_prev, e], axis=1)
            return e, l

        e1, l1 = probs(s1)
        e2, l2 = probs(s2)
        return (e1 - e2 * (lam * l1 / l2)).astype(BF16), 1.0 / l1

    def values(qi, w_and_scale):
        w, scale = w_and_scale
        kv = (qi + 1) * TQ
        o = jnp.dot(w, vb[:kv], preferred_element_type=F32) * scale
        o = o * lax.rsqrt(jnp.mean(o * o, axis=-1, keepdims=True) + EPS) * sg * (1.0 - LAM_INIT)
        o_ref[0, qi * TQ:kv, :] = o

    s = {0: (scores(0, q1), scores(0, q2))}
    w = {}
    for t in range(n_blk + 1):
        if t + 1 < n_blk:
            s[t + 1] = (scores(t + 1, q1), scores(t + 1, q2))
        if t < n_blk:
            w[t] = weights(t, *s.pop(t))
        if t >= 1:
            values(t - 1, w.pop(t - 1))


def _attn(z3, qg2, kg2, lq1, lk1, lq2, lk2, sg, ones_bd):
    bsz, seq, _ = z3.shape
    qoff = 2 * D_RNN // LANES
    koff = qoff + D_ATTN // LANES
    voff = koff + D_ATTN // LANES
    const = lambda shape: pl.BlockSpec(shape, lambda b, h: (0,) * len(shape))
    return pl.pallas_call(
        _attn_kernel,
        grid=(bsz, N_HEADS),
        in_specs=[
            pl.BlockSpec((1, seq, LANES), lambda b, h: (b, 0, qoff + h)),
            pl.BlockSpec((1, seq, LANES), lambda b, h: (b, 0, koff + h)),
            pl.BlockSpec((1, seq, LANES), lambda b, h: (b, 0, voff + h)),
            const((1, LANES)), const((1, LANES)),
            const((1, HEAD_DIM)), const((1, HEAD_DIM)), const((1, HEAD_DIM)), const((1, HEAD_DIM)),
            const((1, LANES)), const((LANES, LANES)),
        ],
        out_specs=pl.BlockSpec((1, seq, LANES), lambda b, h: (b, 0, h)),
        out_shape=jax.ShapeDtypeStruct((bsz, seq, D_ATTN), F32),
        compiler_params=_cparams(("arbitrary", "arbitrary")),
        name="attn",
    )(z3, z3, z3, qg2, kg2, lq1, lk1, lq2, lk2, sg, ones_bd)


def _out_proj_kernel(yr_ref, ya_ref, x_ref, wo_ref, g_ref, rwh_ref, rwl_ref, rb_ref, tri_ref,
                     x1_ref, h2_ref, route_ref, gates_ref, counts_ref):
    acc = jnp.dot(yr_ref[...].astype(BF16), wo_ref[0], preferred_element_type=F32)
    acc = acc + jnp.dot(ya_ref[...].astype(BF16), wo_ref[1], preferred_element_type=F32)
    x1 = x_ref[...] + acc
    x1_ref[...] = x1
    h2 = x1 * lax.rsqrt(jnp.mean(x1 * x1, axis=-1, keepdims=True) + EPS) * g_ref[...]
    hh = h2.astype(BF16)
    h2_ref[...] = hh
    hl = (h2 - hh.astype(F32)).astype(BF16)
    logits = (jnp.dot(hh, rwh_ref[...], preferred_element_type=F32)
              + jnp.dot(hl, rwh_ref[...], preferred_element_type=F32)
              + jnp.dot(hh, rwl_ref[...], preferred_element_type=F32)) + rb_ref[...]
    tm = logits.shape[0]
    l = logits.T[:N_EXPERTS, :]
    eid = lax.broadcasted_iota(jnp.int32, (N_EXPERTS, tm), 0)
    vals, idxs = [], []
    for _ in range(TOP_K):
        m = jnp.max(l, axis=0, keepdims=True)
        idx = jnp.min(jnp.where(l == m, eid, N_EXPERTS), axis=0, keepdims=True)
        vals.append(m)
        idxs.append(idx)
        l = jnp.where(eid == idx, -jnp.inf, l)
    es = [jnp.exp(v - vals[0]) for v in vals]
    inv = 1.0 / (es[0] + es[1] + es[2] + es[3])
    chosen = jnp.zeros((N_EXPERTS, tm), F32)
    for k in range(TOP_K):
        chosen = chosen + (eid == idxs[k]).astype(F32)
    before = jnp.dot(chosen.astype(BF16), tri_ref[...], preferred_element_type=F32)
    sub = lax.broadcasted_iota(jnp.int32, (SUBLANES, tm), 0)
    route = jnp.zeros((SUBLANES, tm), jnp.int32)
    gates = jnp.zeros((SUBLANES, tm), F32)
    for k in range(TOP_K):
        rank = jnp.sum(jnp.where(eid == idxs[k], before, 0.0), axis=0, keepdims=True).astype(jnp.int32)
        route = jnp.where(sub == k, idxs[k], route)
        route = jnp.where(sub == TOP_K + k, rank, route)
        gates = jnp.where(sub == k, es[k] * inv, gates)
    route_ref[...] = route
    gates_ref[...] = gates
    counts_ref[...] = jnp.broadcast_to(jnp.sum(chosen, axis=1, keepdims=True), (N_EXPERTS, LANES)).astype(jnp.int32)


def _out_proj(y_rnn, y_attn, x2, wo_bf, g2, rw_hi, rw_lo, rb, tri):
    n = x2.shape[0]
    n_tiles = n // T_TOK
    row = lambda w: pl.BlockSpec((T_TOK, w), lambda i: (i, 0))
    col = pl.BlockSpec((SUBLANES, T_TOK), lambda i: (0, i))
    const = lambda shape: pl.BlockSpec(shape, lambda i: (0,) * len(shape))
    return pl.pallas_call(
        _out_proj_kernel,
        grid=(n_tiles,),
        in_specs=[row(D_RNN), row(D_ATTN), row(D_MODEL),
                  const((2, D_RNN, D_MODEL)), const((1, D_MODEL)),
                  const((D_MODEL, LANES)), const((D_MODEL, LANES)), const((1, LANES)),
                  const((T_TOK, T_TOK))],
        out_specs=[row(D_MODEL), row(D_MODEL), col, col, pl.BlockSpec((N_EXPERTS, LANES), lambda i: (i, 0))],
        out_shape=[jax.ShapeDtypeStruct((n, D_MODEL), F32),
                   jax.ShapeDtypeStruct((n, D_MODEL), BF16),
                   jax.ShapeDtypeStruct((SUBLANES, n), jnp.int32),
                   jax.ShapeDtypeStruct((SUBLANES, n), F32),
                   jax.ShapeDtypeStruct((n_tiles * N_EXPERTS, LANES), jnp.int32)],
        compiler_params=_cparams(("arbitrary",)),
        name="out_proj",
    )(y_rnn, y_attn, x2, wo_bf, g2, rw_hi, rw_lo, rb, tri)


def _plan(tile_counts, n_blocks):
    n_tiles = tile_counts.shape[0]
    counts = jnp.sum(tile_counts, axis=0)
    padded = (counts + BM - 1) // BM * BM
    pad_end = jnp.cumsum(padded).astype(jnp.int32)
    pad_start = pad_end - padded
    earlier_tiles = jnp.cumsum(tile_counts, axis=0) - tile_counts
    run_off = (jnp.cumsum(tile_counts, axis=1) - tile_counts).astype(jnp.int32)
    run_dst = (pad_start[None, :] + earlier_tiles).astype(jnp.int32)
    off_lanes = jnp.broadcast_to(run_off.reshape(n_tiles * N_EXPERTS, 1), (n_tiles * N_EXPERTS, LANES))
    blk_start = jnp.arange(n_blocks, dtype=jnp.int32) * BM
    blk_e = jnp.minimum(jnp.sum((pad_end[None, :] <= blk_start[:, None]).astype(jnp.int32), axis=1), N_EXPERTS - 1)
    n_valid = (pad_end[-1] // BM).reshape(1)
    return (run_dst.reshape(-1), tile_counts.reshape(-1).astype(jnp.int32), run_off.reshape(-1), off_lanes,
            blk_e, n_valid, pad_end, padded)


def _tile_positions(route, off_col):
    eid = lax.broadcasted_iota(jnp.int32, (N_EXPERTS, route.shape[1]), 0)
    pos = []
    for k in range(TOP_K):
        start = jnp.sum(jnp.where(eid == route[k:k + 1, :], off_col, 0), axis=0, keepdims=True)
        pos.append(start + route[TOP_K + k:TOP_K + k + 1, :])
    return pos


def _run_copies(run_len_ref, tile, make_copy, act):
    def per_expert(e, carry):
        run = tile * N_EXPERTS + e
        length = run_len_ref[run]

        def pieces(bits):
            for b in bits:
                piece_start = (length >> (b + 1)) << (b + 1)

                @pl.when(((length >> b) & 1) == 1)
                def _():
                    act(make_copy(run, piece_start, 1 << b))

        @pl.when(length >= (1 << RUN_RARE_BIT))
        def _():
            pieces(range(RUN_BITS - 1, RUN_RARE_BIT - 1, -1))

        pieces(range(RUN_RARE_BIT - 1, -1, -1))
        return carry

    lax.fori_loop(0, N_EXPERTS, per_expert, 0)


def _dispatch_kernel(run_dst_ref, run_len_ref, run_off_ref, pad_end_ref, padded_ref, n_valid_ref,
                     h_ref, route_ref, off_ref, xs_ref, sorted_ref, zeros_ref, sem, zsem):
    i = pl.program_id(0)
    n_tiles = pl.num_programs(0)
    slot = i % 2
    blk_rows = BM * ROW_TILES
    n_blocks = xs_ref.shape[0] // blk_rows

    def run_copy(s, run, piece_start, size):
        src_row = pl.multiple_of((run_off_ref[run] + piece_start) * ROW_TILES, ROW_TILES)
        dst_row = pl.multiple_of((run_dst_ref[run] + piece_start) * ROW_TILES, ROW_TILES)
        return pltpu.make_async_copy(sorted_ref.at[s, pl.ds(src_row, size * ROW_TILES), :],
                                     xs_ref.at[pl.ds(dst_row, size * ROW_TILES), :], sem.at[s])

    def start_runs(tile, s):
        _run_copies(run_len_ref, tile, functools.partial(run_copy, s), lambda cp: cp.start())

    def wait_runs(s):
        pltpu.make_async_copy(sorted_ref.at[s], xs_ref.at[pl.ds(0, sorted_ref.shape[1]), :], sem.at[s]).wait()

    @pl.when(i == 0)
    def _():
        zeros_ref[...] = jnp.zeros_like(zeros_ref)

        def zero_block(blk):
            start = pl.multiple_of(blk * blk_rows, blk_rows)
            cp = pltpu.make_async_copy(zeros_ref, xs_ref.at[pl.ds(start, blk_rows), :], zsem)
            cp.start()
            cp.wait()

        def zero_tail(e, carry):
            @pl.when(padded_ref[e] > 0)
            def _():
                zero_block(pad_end_ref[e] // BM - 1)
            return carry

        def zero_dead(b, carry):
            zero_block(b)
            return carry

        lax.fori_loop(0, N_EXPERTS, zero_tail, 0)
        lax.fori_loop(n_valid_ref[0], n_blocks, zero_dead, 0)

    @pl.when(i >= 2)
    def _():
        wait_runs(slot)

    pos = _tile_positions(route_ref[...], off_ref[:, 0:1])
    hb = h_ref[...]

    for c in range(TOP_K * T_TOK // P_ROWS):
        row = c * P_ROWS + lax.broadcasted_iota(jnp.int32, (P_ROWS, T_TOK), 0)
        hit = row == pos[0]
        for k in range(1, TOP_K):
            hit = hit | (row == pos[k])
        perm = jnp.where(hit, 1.0, 0.0).astype(BF16)
        rows = jnp.dot(perm, hb, preferred_element_type=F32)
        _to_tile_rows(sorted_ref.at[slot, pl.ds(c * P_ROWS * ROW_TILES, P_ROWS * ROW_TILES), :], rows)
    start_runs(i, slot)

    @pl.when(i == n_tiles - 1)
    def _():
        @pl.when(i >= 1)
        def _():
            wait_runs(1 - slot)
        wait_runs(slot)


def _dispatch(h2, route, off_lanes, run_dst, run_len, run_off, pad_end, padded, n_valid, n_rows):
    n = h2.shape[0]
    grid_spec = pltpu.PrefetchScalarGridSpec(
        num_scalar_prefetch=6,
        grid=(n // T_TOK,),
        in_specs=[
            pl.BlockSpec((T_TOK, D_MODEL), lambda i, *_: (i, 0)),
            pl.BlockSpec((SUBLANES, T_TOK), lambda i, *_: (0, i)),
            pl.BlockSpec((N_EXPERTS, LANES), lambda i, *_: (i, 0)),
        ],
        out_specs=pl.BlockSpec(memory_space=pl.ANY),
        scratch_shapes=[pltpu.VMEM((2, TOP_K * T_TOK * ROW_TILES, LANES), F32),
                        pltpu.VMEM((BM * ROW_TILES, LANES), F32),
                        pltpu.SemaphoreType.DMA((2,)), pltpu.SemaphoreType.DMA(())],
    )
    return pl.pallas_call(
        _dispatch_kernel,
        grid_spec=grid_spec,
        out_shape=jax.ShapeDtypeStruct((n_rows * ROW_TILES, LANES), F32),
        compiler_params=_cparams(("arbitrary",)),
        name="dispatch",
    )(run_dst, run_len, run_off, pad_end, padded, n_valid, h2, route, off_lanes)


def _experts_kernel(blk_e_ref, n_valid_ref, xs_ref, w1_ref, b1_ref, w2_ref, b2_ref, y_ref, w1b_ref, w2b_ref):
    i = pl.program_id(0)

    @pl.when(i < n_valid_ref[0])
    def _():
        prev_e = blk_e_ref[jnp.maximum(i - 1, 0)]

        @pl.when((i == 0) | (blk_e_ref[i] != prev_e))
        def _():
            w1b_ref[...] = w1_ref[0].astype(BF16)
            w2b_ref[...] = w2_ref[0].astype(BF16)

        x = _from_tile_rows(xs_ref, BM).astype(BF16)
        hcat = jnp.dot(x, w1b_ref[...], preferred_element_type=F32) + b1_ref[0]
        gate = jnp.minimum(hcat[:, :D_FF], SWIGLU_LIMIT)
        up = jnp.clip(hcat[:, D_FF:], -SWIGLU_LIMIT, SWIGLU_LIMIT)
        act = gate * jax.nn.sigmoid(SWIGLU_ALPHA * gate) * (up + 1.0)
        y = jnp.dot(act.astype(BF16), w2b_ref[...], preferred_element_type=F32) + b2_ref[0]
        _to_tile_rows(y_ref, y)

    @pl.when(i >= n_valid_ref[0])
    def _():
        y_ref[...] = jnp.zeros_like(y_ref)


def _experts(xs, blk_e, n_valid, w1, b1, w2, b2):
    blk_rows = BM * ROW_TILES
    n_blocks = xs.shape[0] // blk_rows
    exp3 = lambda i, be, nv: (be[i], 0, 0)
    grid_spec = pltpu.PrefetchScalarGridSpec(
        num_scalar_prefetch=2,
        grid=(n_blocks,),
        in_specs=[
            pl.BlockSpec((blk_rows, LANES), lambda i, be, nv: (jnp.minimum(i, nv[0] - 1), 0)),
            pl.BlockSpec((1, D_MODEL, 2 * D_FF), exp3),
            pl.BlockSpec((1, 1, 2 * D_FF), exp3),
            pl.BlockSpec((1, D_FF, D_MODEL), exp3),
            pl.BlockSpec((1, 1, D_MODEL), exp3),
        ],
        out_specs=pl.BlockSpec((blk_rows, LANES), lambda i, be, nv: (i, 0)),
        scratch_shapes=[pltpu.VMEM((D_MODEL, 2 * D_FF), BF16), pltpu.VMEM((D_FF, D_MODEL), BF16)],
    )
    return pl.pallas_call(
        _experts_kernel,
        grid_spec=grid_spec,
        out_shape=jax.ShapeDtypeStruct(xs.shape, F32),
        compiler_params=_cparams(("arbitrary",)),
        name="experts",
    )(blk_e, n_valid, xs, w1, b1, w2, b2)


def _combine_kernel(run_src_ref, run_len_ref, run_off_ref, x1_ref, route_ref, gates_ref, off_ref, ys_ref, o_ref,
                    buf_ref, sem):
    i = pl.program_id(0)
    n_tiles = pl.num_programs(0)
    slot = i % 2

    def run_copy(s, run, piece_start, size):
        src_row = pl.multiple_of((run_src_ref[run] + piece_start) * ROW_TILES, ROW_TILES)
        dst_row = pl.multiple_of((run_off_ref[run] + piece_start) * ROW_TILES, ROW_TILES)
        return pltpu.make_async_copy(ys_ref.at[pl.ds(src_row, size * ROW_TILES), :],
                                     buf_ref.at[s, pl.ds(dst_row, size * ROW_TILES), :], sem.at[s])

    def start_runs(tile, s):
        _run_copies(run_len_ref, tile, functools.partial(run_copy, s), lambda cp: cp.start())

    @pl.when(i == 0)
    def _():
        start_runs(0, 0)

    @pl.when(i + 1 < n_tiles)
    def _():
        start_runs(i + 1, 1 - slot)

    pltpu.make_async_copy(ys_ref.at[pl.ds(0, buf_ref.shape[1]), :], buf_ref.at[slot], sem.at[slot]).wait()

    pos = _tile_positions(route_ref[...], off_ref[:, 0:1])
    sub = lax.broadcasted_iota(jnp.int32, (SUBLANES, T_TOK), 0)
    gates = gates_ref[...]
    packed = jnp.zeros((SUBLANES, T_TOK), F32)
    for k in range(TOP_K):
        packed = jnp.where(sub == k, pos[k].astype(F32), packed)
        packed = jnp.where(sub == TOP_K + k, gates[k:k + 1, :], packed)
    cols = packed.T
    acc = x1_ref[...]
    for c in range(TOP_K * T_TOK // P_ROWS):
        row = (c * P_ROWS + lax.broadcasted_iota(jnp.int32, (T_TOK, P_ROWS), 1)).astype(F32)
        g = jnp.zeros((T_TOK, P_ROWS), F32)
        for k in range(TOP_K):
            g = g + jnp.where(row == cols[:, k:k + 1], cols[:, TOP_K + k:TOP_K + k + 1], 0.0)
        y = _from_tile_rows(buf_ref.at[slot, pl.ds(c * P_ROWS * ROW_TILES, P_ROWS * ROW_TILES), :], P_ROWS)
        acc = acc + jnp.dot(g.astype(BF16), y.astype(BF16), preferred_element_type=F32)
    o_ref[...] = acc


def _combine(x1, route, gates, off_lanes, run_dst, run_len, run_off, ys):
    n = x1.shape[0]
    grid_spec = pltpu.PrefetchScalarGridSpec(
        num_scalar_prefetch=3,
        grid=(n // T_TOK,),
        in_specs=[
            pl.BlockSpec((T_TOK, D_MODEL), lambda i, *_: (i, 0)),
            pl.BlockSpec((SUBLANES, T_TOK), lambda i, *_: (0, i)),
            pl.BlockSpec((SUBLANES, T_TOK), lambda i, *_: (0, i)),
            pl.BlockSpec((N_EXPERTS, LANES), lambda i, *_: (i, 0)),
            pl.BlockSpec(memory_space=pl.ANY),
        ],
        out_specs=pl.BlockSpec((T_TOK, D_MODEL), lambda i, *_: (i, 0)),
        scratch_shapes=[pltpu.VMEM((2, TOP_K * T_TOK * ROW_TILES, LANES), F32), pltpu.SemaphoreType.DMA((2,))],
    )
    return pl.pallas_call(
        _combine_kernel,
        grid_spec=grid_spec,
        out_shape=jax.ShapeDtypeStruct((n, D_MODEL), F32),
        compiler_params=_cparams(("arbitrary",)),
        name="combine",
    )(run_dst, run_len, run_off, x1, route, gates, off_lanes, ys)


def _block_diag(w):
    n, r, _ = w.shape
    eye = jnp.eye(n, dtype=w.dtype)
    return (eye[:, None, :, None] * w[:, :, None, :]).reshape(n * r, n * r)


def kernel(x, norm1_g, w_in, conv_w, conv_b, lru_wa, lru_ba, lru_wx, lru_bx, lru_lambda, q_norm_g, k_norm_g,
           lambda_q1, lambda_k1, lambda_q2, lambda_k2, subln_g, w_out, norm2_g, router_w, router_b, w1, b1, w2, b2):
    bsz, seq, d = x.shape
    n_tok = bsz * seq
    assert d == D_MODEL and n_tok % TM_PROJ == 0 and seq % T_SCAN == 0 and seq % TQ == 0 and n_tok % T_TOK == 0
    assert (n_tok * TOP_K) % BM == 0
    assert norm1_g.shape[0] == 1, "single-layer stack"
    x2 = x.reshape(n_tok, d)

    z = _in_proj(x2, norm1_g[0][None, :], w_in[0].astype(BF16))
    z3 = z.reshape(bsz, seq, D_IN)

    n_slabs = D_RNN // LANES
    per_slab = LANES // RNN_BLOCK
    wa = lru_wa[0].reshape(n_slabs, per_slab, RNN_BLOCK, RNN_BLOCK)
    wx = lru_wx[0].reshape(n_slabs, per_slab, RNN_BLOCK, RNN_BLOCK)
    wg = jnp.concatenate([jax.vmap(_block_diag)(wa), jax.vmap(_block_diag)(wx)], axis=2).astype(BF16)
    bg = jnp.concatenate([lru_ba[0].reshape(n_slabs, 1, LANES), lru_bx[0].reshape(n_slabs, 1, LANES)], axis=2)
    y_rnn = _rnn(z3, conv_w[0], conv_b[0][None, :], wg, bg, lru_lambda[0].reshape(n_slabs, 1, LANES))

    half = jnp.arange(LANES) // HEAD_DIM
    ones_bd = (half[:, None] == half[None, :]).astype(BF16)
    y_attn = _attn(z3, jnp.tile(q_norm_g[0], 2)[None, :], jnp.tile(k_norm_g[0], 2)[None, :],
                   lambda_q1[0][None, :], lambda_k1[0][None, :], lambda_q2[0][None, :], lambda_k2[0][None, :],
                   subln_g[0][None, :], ones_bd)

    rw = jnp.pad(router_w[0], ((0, 0), (0, LANES - N_EXPERTS)))
    rw_hi = rw.astype(BF16)
    rw_lo = (rw - rw_hi.astype(F32)).astype(BF16)
    rb = jnp.pad(router_b[0], (0, LANES - N_EXPERTS))[None, :]
    tok = jnp.arange(T_TOK)
    tri = (tok[:, None] < tok[None, :]).astype(BF16)
    x1, h2, route, gates, counts = _out_proj(
        y_rnn.reshape(n_tok, D_RNN), y_attn.reshape(n_tok, D_ATTN), x2,
        w_out[0].astype(BF16).reshape(2, D_RNN, D_MODEL), norm2_g[0][None, :], rw_hi, rw_lo, rb, tri)

    n_blocks = (n_tok * TOP_K) // BM + N_EXPERTS
    tile_counts = counts[:, 0].reshape(n_tok // T_TOK, N_EXPERTS)
    run_dst, run_len, run_off, off_lanes, blk_e, n_valid, pad_end, padded = _plan(tile_counts, n_blocks)
    xs = _dispatch(h2, route, off_lanes, run_dst, run_len, run_off, pad_end, padded, n_valid, n_blocks * BM)
    ys = _experts(xs, blk_e, n_valid, w1[0], b1[0][:, None, :], w2[0], b2[0][:, None, :])
    out = _combine(x1, route, gates, off_lanes, run_dst, run_len, run_off, ys)
    return out.reshape(bsz, seq, d)
```

```python
import functools
import math

import jax
import jax.numpy as jnp
from jax import lax
from jax.experimental import pallas as pl
from jax.experimental.pallas import tpu as pltpu

F32 = jnp.float32
BF16 = jnp.bfloat16

D_MODEL = 1024
D_RNN = 512
N_RNN_BLOCKS = 8
RNN_BLOCK = 64
CONV_WIDTH = 4
LRU_C = 8.0
HEAD_DIM = 64
N_HEADS = 4
D_ATTN = 512
D_IN = 2 * D_RNN + 3 * D_ATTN
N_EXPERTS = 32
TOP_K = 4
D_FF = 1024
SWIGLU_LIMIT = 7.0
SWIGLU_ALPHA = 1.702
EPS = 1e-5
LAM_INIT = 0.8 - 0.6 * math.exp(0.0)
LOG2_E = math.log2(math.e)

LANES = 128
SUBLANES = 8
ROW_TILES = D_MODEL // LANES
N_DMA_PRIORITIES = 2
VMEM_LIMIT = 52 * 1024 * 1024

TM_PROJ = 512
T_SCAN = 512
TQ = 128
BM = 512
T_TOK = 512
P_ROWS = 256
RUN_BITS = T_TOK.bit_length()


def _cparams(sem):
    return pltpu.CompilerParams(dimension_semantics=sem, vmem_limit_bytes=VMEM_LIMIT)


def _to_tile_rows(ref, x):
    rows = x.shape[0]
    for s in range(ROW_TILES):
        ref[pl.ds(s, rows, stride=ROW_TILES), :] = x[:, s * LANES:(s + 1) * LANES]


def _from_tile_rows(ref, rows):
    return jnp.concatenate([ref[pl.ds(s, rows, stride=ROW_TILES), :] for s in range(ROW_TILES)], axis=1)


def _in_proj_kernel(x_ref, g_ref, w_ref, z_ref):
    x = x_ref[...]
    ms = jnp.mean(x * x, axis=-1, keepdims=True)
    h = x * lax.rsqrt(ms + EPS) * g_ref[...]
    z_ref[...] = jnp.dot(h.astype(BF16), w_ref[...], preferred_element_type=F32)


def _in_proj(x2, g, w_bf):
    n = x2.shape[0]
    return pl.pallas_call(
        _in_proj_kernel,
        grid=(n // TM_PROJ,),
        in_specs=[
            pl.BlockSpec((TM_PROJ, D_MODEL), lambda i: (i, 0)),
            pl.BlockSpec((1, D_MODEL), lambda i: (0, 0)),
            pl.BlockSpec((D_MODEL, D_IN), lambda i: (0, 0)),
        ],
        out_specs=pl.BlockSpec((TM_PROJ, D_IN), lambda i: (i, 0)),
        out_shape=jax.ShapeDtypeStruct((n, D_IN), F32),
        compiler_params=_cparams(("arbitrary",)),
        name="in_proj",
    )(x2, g, w_bf)


def _rnn_kernel(xr_ref, gr_ref, cw_ref, cb_ref, wg_ref, bg_ref, lam_ref, y_ref):
    seq = xr_ref.shape[1]
    n_chunks = seq // T_SCAN
    n_groups = T_SCAN // SUBLANES
    cw = cw_ref[...]
    cb = cb_ref[...]
    nl = -lam_ref[0]
    softplus_neg_lam = jnp.maximum(nl, 0.0) + jnp.log(1.0 + jnp.exp(-jnp.abs(nl)))
    group = lax.broadcasted_iota(jnp.int32, (n_groups, LANES), 0)

    def previous_group(v, first):
        return jnp.where(group >= 1, pltpu.roll(v, 1, axis=0), first)

    def phase_rows(t0, r):
        return pl.ds(t0 + r, n_groups, stride=SUBLANES)

    def chunk(c, carry):
        h_prev, x_tail = carry[0], carry[1:]
        t0 = pl.multiple_of(c * T_SCAN, T_SCAN)
        x = [xr_ref[0, phase_rows(t0, r), :] for r in range(SUBLANES)]
        wrapped = [previous_group(x[SUBLANES - j], x_tail[CONV_WIDTH - 1 - j]) for j in range(1, CONV_WIDTH)]

        def delayed(r, j):
            return x[r - j] if r >= j else wrapped[j - r - 1]

        conv = []
        for r in range(SUBLANES):
            acc = cb + cw[CONV_WIDTH - 1:CONV_WIDTH, :] * x[r]
            for j in range(1, CONV_WIDTH):
                acc = acc + cw[CONV_WIDTH - 1 - j:CONV_WIDTH - j, :] * delayed(r, j)
            conv.append(acc)
        conv = jnp.concatenate(conv, axis=0)
        gates = jnp.dot(conv.astype(BF16), wg_ref[0], preferred_element_type=F32) + bg_ref[0]
        rg = jax.nn.sigmoid(gates[:, :LANES])
        ig = jax.nn.sigmoid(gates[:, LANES:])
        a = jnp.exp(-LRU_C * rg * softplus_neg_lam)
        var = 1.0 - a * a
        u = jnp.where(var > 0.0, var * lax.rsqrt(var), 0.0) * (ig * conv)
        ph = lambda v, r: v[r * n_groups:(r + 1) * n_groups, :]
        a_in, u_in = [ph(a, 0)], [ph(u, 0)]
        for r in range(1, SUBLANES):
            a_in.append(ph(a, r) * a_in[-1])
            u_in.append(ph(a, r) * u_in[-1] + ph(u, r))
        ga, gu = a_in[-1], u_in[-1]
        d = 1
        while d < n_groups:
            keep = group >= d
            gu = jnp.where(keep, ga * pltpu.roll(gu, d, axis=0) + gu, gu)
            ga = jnp.where(keep, ga * pltpu.roll(ga, d, axis=0), ga)
            d *= 2
        h_after = ga * h_prev + gu
        h_before = previous_group(h_after, h_prev)
        for r in range(SUBLANES):
            h = a_in[r] * h_before + u_in[r]
            rows = phase_rows(t0, r)
            y_ref[0, rows, :] = h * jax.nn.gelu(gr_ref[0, rows, :], approximate=True)
        last = slice(n_groups - 1, n_groups)
        return (h_after[last, :],) + tuple(x[SUBLANES - CONV_WIDTH + 1 + j][last, :] for j in range(CONV_WIDTH - 1))

    zero = jnp.zeros((1, LANES), F32)
    lax.fori_loop(0, n_chunks, chunk, (zero,) * CONV_WIDTH)


def _rnn(z3, conv_w, conv_b, wg_slabs, bg_slabs, lam_slabs):
    bsz, seq, _ = z3.shape
    n_slabs = D_RNN // LANES
    slab = lambda rows: pl.BlockSpec((rows, LANES), lambda b, c: (0, c))
    slab3 = lambda shape: pl.BlockSpec((1,) + shape, lambda b, c: (c, 0, 0))
    return pl.pallas_call(
        _rnn_kernel,
        grid=(bsz, n_slabs),
        in_specs=[
            pl.BlockSpec((1, seq, LANES), lambda b, c: (b, 0, c)),
            pl.BlockSpec((1, seq, LANES), lambda b, c: (b, 0, n_slabs + c)),
            slab(CONV_WIDTH),
            slab(1),
            slab3((LANES, 2 * LANES)),
            slab3((1, 2 * LANES)),
            slab3((1, LANES)),
        ],
        out_specs=pl.BlockSpec((1, seq, LANES), lambda b, c: (b, 0, c)),
        out_shape=jax.ShapeDtypeStruct((bsz, seq, D_RNN), F32),
        compiler_params=_cparams(("arbitrary", "arbitrary")),
        name="rnn",
    )(z3, z3, conv_w, conv_b, wg_slabs, bg_slabs, lam_slabs)


def _group_rms(x, ones_bd):
    x2 = x * x
    hi = x2.astype(BF16)
    lo = (x2 - hi.astype(F32)).astype(BF16)
    ssq = jnp.dot(hi, ones_bd, preferred_element_type=F32) + jnp.dot(lo, ones_bd, preferred_element_type=F32)
    return x * lax.rsqrt(ssq * (1.0 / HEAD_DIM) + EPS)


def _attn_kernel(q_ref, k_ref, v_ref, qg_ref, kg_ref, lq1_ref, lk1_ref, lq2_ref, lk2_ref, sg_ref, ones_ref, o_ref):
    seq = q_ref.shape[1]
    ones_bd = ones_ref[...]
    lam = (jnp.exp(jnp.sum(lq1_ref[...] * lk1_ref[...], axis=-1, keepdims=True))
           - jnp.exp(jnp.sum(lq2_ref[...] * lk2_ref[...], axis=-1, keepdims=True)) + LAM_INIT)
    qn = _group_rms(q_ref[0], ones_bd) * qg_ref[...] * (HEAD_DIM ** -0.5 * LOG2_E)
    kn = _group_rms(k_ref[0], ones_bd) * kg_ref[...]
    lane = lax.broadcasted_iota(jnp.int32, (seq, LANES), 1)
    q1 = jnp.where(lane < HEAD_DIM, qn, 0.0).astype(BF16)
    q2 = jnp.where(lane >= HEAD_DIM, qn, 0.0).astype(BF16)
    kb = kn.astype(BF16)
    vb = v_ref[0].astype(BF16)
    sg = sg_ref[...]
    dn = (((1,), (1,)), ((), ()))
    n_blk = seq // TQ

    def scores(qi, qm):
        kv = (qi + 1) * TQ
        return lax.dot_general(qm[qi * TQ:kv], kb[:kv], dn, preferred_element_type=F32)

    causal = (lax.broadcasted_iota(jnp.int32, (TQ, TQ), 1) <= lax.broadcasted_iota(jnp.int32, (TQ, TQ), 0))

    def weights(qi, s1, s2):
        c0 = qi * TQ

        def probs(s):
            diag = jnp.where(causal, s[:, c0:], -jnp.inf)
            m = jnp.max(diag, axis=-1, keepdims=True)
            if qi > 0:
                m = jnp.maximum(m, jnp.max(s[:, :c0], axis=-1, keepdims=True))
            e = jnp.exp2(diag - m)
            l = jnp.sum(e, axis=-1, keepdims=True)
            if qi > 0:
                e_prev = jnp.exp2(s[:, :c0] - m)
                l = l + jnp.sum(e_prev, axis=-1, keepdims=True)
                e = jnp.concatenate([e_prev, e], axis=1)
            return e, l

        e1, l1 = probs(s1)
        e2, l2 = probs(s2)
        return (e1 - e2 * (lam * l1 / l2)).astype(BF16), 1.0 / l1

    def values(qi, w_and_scale):
        w, scale = w_and_scale
        kv = (qi + 1) * TQ
        o = jnp.dot(w, vb[:kv], preferred_element_type=F32) * scale
        o = o * lax.rsqrt(jnp.mean(o * o, axis=-1, keepdims=True) + EPS) * sg * (1.0 - LAM_INIT)
        o_ref[0, qi * TQ:kv, :] = o

    s = {0: (scores(0, q1), scores(0, q2))}
    w = {}
    for t in range(n_blk + 1):
        if t + 1 < n_blk:
            s[t + 1] = (scores(t + 1, q1), scores(t + 1, q2))
        if t < n_blk:
            w[t] = weights(t, *s.pop(t))
        if t >= 1:
            values(t - 1, w.pop(t - 1))


def _attn(z3, qg2, kg2, lq1, lk1, lq2, lk2, sg, ones_bd):
    bsz, seq, _ = z3.shape
    qoff = 2 * D_RNN // LANES
    koff = qoff + D_ATTN // LANES
    voff = koff + D_ATTN // LANES
    const = lambda shape: pl.BlockSpec(shape, lambda b, h: (0,) * len(shape))
    return pl.pallas_call(
        _attn_kernel,
        grid=(bsz, N_HEADS),
        in_specs=[
            pl.BlockSpec((1, seq, LANES), lambda b, h: (b, 0, qoff + h)),
            pl.BlockSpec((1, seq, LANES), lambda b, h: (b, 0, koff + h)),
            pl.BlockSpec((1, seq, LANES), lambda b, h: (b, 0, voff + h)),
            const((1, LANES)), const((1, LANES)),
            const((1, HEAD_DIM)), const((1, HEAD_DIM)), const((1, HEAD_DIM)), const((1, HEAD_DIM)),
            const((1, LANES)), const((LANES, LANES)),
        ],
        out_specs=pl.BlockSpec((1, seq, LANES), lambda b, h: (b, 0, h)),
        out_shape=jax.ShapeDtypeStruct((bsz, seq, D_ATTN), F32),
        compiler_params=_cparams(("arbitrary", "arbitrary")),
        name="attn",
    )(z3, z3, z3, qg2, kg2, lq1, lk1, lq2, lk2, sg, ones_bd)


def _out_proj_kernel(yr_ref, ya_ref, x_ref, wo_ref, g_ref, rwh_ref, rwl_ref, rb_ref, tri_ref,
                     x1_ref, h2_ref, route_ref, gates_ref, counts_ref):
    acc = jnp.dot(yr_ref[...].astype(BF16), wo_ref[0], preferred_element_type=F32)
    acc = acc + jnp.dot(ya_ref[...].astype(BF16), wo_ref[1], preferred_element_type=F32)
    x1 = x_ref[...] + acc
    x1_ref[...] = x1
    h2 = x1 * lax.rsqrt(jnp.mean(x1 * x1, axis=-1, keepdims=True) + EPS) * g_ref[...]
    hh = h2.astype(BF16)
    h2_ref[...] = hh
    hl = (h2 - hh.astype(F32)).astype(BF16)
    logits = (jnp.dot(hh, rwh_ref[...], preferred_element_type=F32)
              + jnp.dot(hl, rwh_ref[...], preferred_element_type=F32)
              + jnp.dot(hh, rwl_ref[...], preferred_element_type=F32)) + rb_ref[...]
    tm = logits.shape[0]
    l = logits.T[:N_EXPERTS, :]
    eid = lax.broadcasted_iota(jnp.int32, (N_EXPERTS, tm), 0)
    vals, idxs = [], []
    for _ in range(TOP_K):
        m = jnp.max(l, axis=0, keepdims=True)
        idx = jnp.min(jnp.where(l == m, eid, N_EXPERTS), axis=0, keepdims=True)
        vals.append(m)
        idxs.append(idx)
        l = jnp.where(eid == idx, -jnp.inf, l)
    es = [jnp.exp(v - vals[0]) for v in vals]
    inv = 1.0 / (es[0] + es[1] + es[2] + es[3])
    chosen = jnp.zeros((N_EXPERTS, tm), F32)
    for k in range(TOP_K):
        chosen = chosen + (eid == idxs[k]).astype(F32)
    before = jnp.dot(chosen.astype(BF16), tri_ref[...], preferred_element_type=F32)
    sub = lax.broadcasted_iota(jnp.int32, (SUBLANES, tm), 0)
    route = jnp.zeros((SUBLANES, tm), jnp.int32)
    gates = jnp.zeros((SUBLANES, tm), F32)
    for k in range(TOP_K):
        rank = jnp.sum(jnp.where(eid == idxs[k], before, 0.0), axis=0, keepdims=True).astype(jnp.int32)
        route = jnp.where(sub == k, idxs[k], route)
        route = jnp.where(sub == TOP_K + k, rank, route)
        gates = jnp.where(sub == k, es[k] * inv, gates)
    route_ref[...] = route
    gates_ref[...] = gates
    counts_ref[...] = jnp.broadcast_to(jnp.sum(chosen, axis=1, keepdims=True), (N_EXPERTS, LANES)).astype(jnp.int32)


def _out_proj(y_rnn, y_attn, x2, wo_bf, g2, rw_hi, rw_lo, rb, tri):
    n = x2.shape[0]
    n_tiles = n // T_TOK
    row = lambda w: pl.BlockSpec((T_TOK, w), lambda i: (i, 0))
    col = pl.BlockSpec((SUBLANES, T_TOK), lambda i: (0, i))
    const = lambda shape: pl.BlockSpec(shape, lambda i: (0,) * len(shape))
    return pl.pallas_call(
        _out_proj_kernel,
        grid=(n_tiles,),
        in_specs=[row(D_RNN), row(D_ATTN), row(D_MODEL),
                  const((2, D_RNN, D_MODEL)), const((1, D_MODEL)),
                  const((D_MODEL, LANES)), const((D_MODEL, LANES)), const((1, LANES)),
                  const((T_TOK, T_TOK))],
        out_specs=[row(D_MODEL), row(D_MODEL), col, col, pl.BlockSpec((N_EXPERTS, LANES), lambda i: (i, 0))],
        out_shape=[jax.ShapeDtypeStruct((n, D_MODEL), F32),
                   jax.ShapeDtypeStruct((n, D_MODEL), BF16),
                   jax.ShapeDtypeStruct((SUBLANES, n), jnp.int32),
                   jax.ShapeDtypeStruct((SUBLANES, n), F32),
                   jax.ShapeDtypeStruct((n_tiles * N_EXPERTS, LANES), jnp.int32)],
        compiler_params=_cparams(("arbitrary",)),
        name="out_proj",
    )(y_rnn, y_attn, x2, wo_bf, g2, rw_hi, rw_lo, rb, tri)


def _plan(tile_counts, n_blocks):
    n_tiles = tile_counts.shape[0]
    counts = jnp.sum(tile_counts, axis=0)
    padded = (counts + BM - 1) // BM * BM
    pad_end = jnp.cumsum(padded).astype(jnp.int32)
    pad_start = pad_end - padded
    earlier_tiles = jnp.cumsum(tile_counts, axis=0) - tile_counts
    run_off = (jnp.cumsum(tile_counts, axis=1) - tile_counts).astype(jnp.int32)
    run_dst = (pad_start[None, :] + earlier_tiles).astype(jnp.int32)
    off_lanes = jnp.broadcast_to(run_off.reshape(n_tiles * N_EXPERTS, 1), (n_tiles * N_EXPERTS, LANES))
    blk_start = jnp.arange(n_blocks, dtype=jnp.int32) * BM
    blk_e = jnp.minimum(jnp.sum((pad_end[None, :] <= blk_start[:, None]).astype(jnp.int32), axis=1), N_EXPERTS - 1)
    n_valid = (pad_end[-1] // BM).reshape(1)
    return (run_dst.reshape(-1), tile_counts.reshape(-1).astype(jnp.int32), run_off.reshape(-1), off_lanes,
            blk_e, n_valid, pad_end, padded)


def _tile_positions(route, off_col):
    eid = lax.broadcasted_iota(jnp.int32, (N_EXPERTS, route.shape[1]), 0)
    pos = []
    for k in range(TOP_K):
        start = jnp.sum(jnp.where(eid == route[k:k + 1, :], off_col, 0), axis=0, keepdims=True)
        pos.append(start + route[TOP_K + k:TOP_K + k + 1, :])
    return pos


def _run_copies(run_len_ref, tile, run_bases, make_copy, act):
    def per_expert(e, carry):
        run = tile * N_EXPERTS + e
        length = run_len_ref[run]
        bases = run_bases(run)
        for b in range(RUN_BITS - 1, -1, -1):
            piece_start = (length >> (b + 1)) << (b + 1)

            @pl.when(((length >> b) & 1) == 1)
            def _():
                act(make_copy(bases, piece_start, 1 << b))
        return carry

    lax.fori_loop(0, N_EXPERTS, per_expert, 0)


def _dispatch_kernel(run_dst_ref, run_len_ref, run_off_ref, pad_end_ref, padded_ref, n_valid_ref,
                     h_ref, route_ref, off_ref, xs_ref, sorted_ref, zeros_ref, sem, zsem):
    i = pl.program_id(0)
    n_tiles = pl.num_programs(0)
    slot = i % 2
    blk_rows = BM * ROW_TILES
    n_blocks = xs_ref.shape[0] // blk_rows

    def run_bases(run):
        return run_off_ref[run], run_dst_ref[run]

    def run_copy(s, bases, piece_start, size):
        src_row = pl.multiple_of((bases[0] + piece_start) * ROW_TILES, ROW_TILES)
        dst_row = pl.multiple_of((bases[1] + piece_start) * ROW_TILES, ROW_TILES)
        return pltpu.make_async_copy(sorted_ref.at[s, pl.ds(src_row, size * ROW_TILES), :],
                                     xs_ref.at[pl.ds(dst_row, size * ROW_TILES), :], sem.at[s])

    def start_runs(tile, s):
        _run_copies(run_len_ref, tile, run_bases, functools.partial(run_copy, s), lambda cp: cp.start())

    def wait_runs(s):
        pltpu.make_async_copy(sorted_ref.at[s], xs_ref.at[pl.ds(0, sorted_ref.shape[1]), :], sem.at[s]).wait()

    @pl.when(i == 0)
    def _():
        zeros_ref[...] = jnp.zeros_like(zeros_ref)

        def zero_block(blk):
            start = pl.multiple_of(blk * blk_rows, blk_rows)
            cp = pltpu.make_async_copy(zeros_ref, xs_ref.at[pl.ds(start, blk_rows), :], zsem)
            cp.start()
            cp.wait()

        def zero_tail(e, carry):
            @pl.when(padded_ref[e] > 0)
            def _():
                zero_block(pad_end_ref[e] // BM - 1)
            return carry

        def zero_dead(b, carry):
            zero_block(b)
            return carry

        lax.fori_loop(0, N_EXPERTS, zero_tail, 0)
        lax.fori_loop(n_valid_ref[0], n_blocks, zero_dead, 0)

    @pl.when(i >= 2)
    def _():
        wait_runs(slot)

    pos = _tile_positions(route_ref[...], off_ref[:, 0:1])
    hb = h_ref[...]

    for c in range(TOP_K * T_TOK // P_ROWS):
        row = c * P_ROWS + lax.broadcasted_iota(jnp.int32, (P_ROWS, T_TOK), 0)
        hit = row == pos[0]
        for k in range(1, TOP_K):
            hit = hit | (row == pos[k])
        perm = jnp.where(hit, 1.0, 0.0).astype(BF16)
        rows = jnp.dot(perm, hb, preferred_element_type=F32)
        _to_tile_rows(sorted_ref.at[slot, pl.ds(c * P_ROWS * ROW_TILES, P_ROWS * ROW_TILES), :], rows)
    start_runs(i, slot)

    @pl.when(i == n_tiles - 1)
    def _():
        @pl.when(i >= 1)
        def _():
            wait_runs(1 - slot)
        wait_runs(slot)


def _dispatch(h2, route, off_lanes, run_dst, run_len, run_off, pad_end, padded, n_valid, n_rows):
    n = h2.shape[0]
    grid_spec = pltpu.PrefetchScalarGridSpec(
        num_scalar_prefetch=6,
        grid=(n // T_TOK,),
        in_specs=[
            pl.BlockSpec((T_TOK, D_MODEL), lambda i, *_: (i, 0)),
            pl.BlockSpec((SUBLANES, T_TOK), lambda i, *_: (0, i)),
            pl.BlockSpec((N_EXPERTS, LANES), lambda i, *_: (i, 0)),
        ],
        out_specs=pl.BlockSpec(memory_space=pl.ANY),
        scratch_shapes=[pltpu.VMEM((2, TOP_K * T_TOK * ROW_TILES, LANES), F32),
                        pltpu.VMEM((BM * ROW_TILES, LANES), F32),
                        pltpu.SemaphoreType.DMA((2,)), pltpu.SemaphoreType.DMA(())],
    )
    return pl.pallas_call(
        _dispatch_kernel,
        grid_spec=grid_spec,
        out_shape=jax.ShapeDtypeStruct((n_rows * ROW_TILES, LANES), F32),
        compiler_params=_cparams(("arbitrary",)),
        name="dispatch",
    )(run_dst, run_len, run_off, pad_end, padded, n_valid, h2, route, off_lanes)


def _experts_kernel(blk_e_ref, n_valid_ref, xs_ref, w1_ref, b1_ref, w2_ref, b2_ref, y_ref, w1b_ref, w2b_ref):
    i = pl.program_id(0)

    @pl.when(i < n_valid_ref[0])
    def _():
        prev_e = blk_e_ref[jnp.maximum(i - 1, 0)]

        @pl.when((i == 0) | (blk_e_ref[i] != prev_e))
        def _():
            w1b_ref[...] = w1_ref[0].astype(BF16)
            w2b_ref[...] = w2_ref[0].astype(BF16)

        x = _from_tile_rows(xs_ref, BM).astype(BF16)
        hcat = jnp.dot(x, w1b_ref[...], preferred_element_type=F32) + b1_ref[0]
        gate = jnp.minimum(hcat[:, :D_FF], SWIGLU_LIMIT)
        up = jnp.clip(hcat[:, D_FF:], -SWIGLU_LIMIT, SWIGLU_LIMIT)
        act = gate * jax.nn.sigmoid(SWIGLU_ALPHA * gate) * (up + 1.0)
        y = jnp.dot(act.astype(BF16), w2b_ref[...], preferred_element_type=F32) + b2_ref[0]
        _to_tile_rows(y_ref, y)

    @pl.when(i >= n_valid_ref[0])
    def _():
        y_ref[...] = jnp.zeros_like(y_ref)


def _experts(xs, blk_e, n_valid, w1, b1, w2, b2):
    blk_rows = BM * ROW_TILES
    n_blocks = xs.shape[0] // blk_rows
    exp3 = lambda i, be, nv: (be[i], 0, 0)
    grid_spec = pltpu.PrefetchScalarGridSpec(
        num_scalar_prefetch=2,
        grid=(n_blocks,),
        in_specs=[
            pl.BlockSpec((blk_rows, LANES), lambda i, be, nv: (jnp.minimum(i, nv[0] - 1), 0)),
            pl.BlockSpec((1, D_MODEL, 2 * D_FF), exp3),
            pl.BlockSpec((1, 1, 2 * D_FF), exp3),
            pl.BlockSpec((1, D_FF, D_MODEL), exp3),
            pl.BlockSpec((1, 1, D_MODEL), exp3),
        ],
        out_specs=pl.BlockSpec((blk_rows, LANES), lambda i, be, nv: (i, 0)),
        scratch_shapes=[pltpu.VMEM((D_MODEL, 2 * D_FF), BF16), pltpu.VMEM((D_FF, D_MODEL), BF16)],
    )
    return pl.pallas_call(
        _experts_kernel,
        grid_spec=grid_spec,
        out_shape=jax.ShapeDtypeStruct(xs.shape, F32),
        compiler_params=_cparams(("arbitrary",)),
        name="experts",
    )(blk_e, n_valid, xs, w1, b1, w2, b2)


def _combine_kernel(run_src_ref, run_len_ref, run_off_ref, x1_ref, route_ref, gates_ref, off_ref, ys_ref, o_ref,
                    buf_ref, sem):
    i = pl.program_id(0)
    n_tiles = pl.num_programs(0)
    slot = i % 2

    def run_bases(run):
        return run_src_ref[run], run_off_ref[run]

    def run_copy(s, bases, piece_start, size):
        src_row = pl.multiple_of((bases[0] + piece_start) * ROW_TILES, ROW_TILES)
        dst_row = pl.multiple_of((bases[1] + piece_start) * ROW_TILES, ROW_TILES)
        return pltpu.make_async_copy(ys_ref.at[pl.ds(src_row, size * ROW_TILES), :],
                                     buf_ref.at[s, pl.ds(dst_row, size * ROW_TILES), :], sem.at[s])

    def start_runs(tile, s):
        _run_copies(run_len_ref, tile, run_bases, functools.partial(run_copy, s), lambda cp: cp.start())

    @pl.when(i == 0)
    def _():
        start_runs(0, 0)

    @pl.when(i + 1 < n_tiles)
    def _():
        start_runs(i + 1, 1 - slot)

    pltpu.make_async_copy(ys_ref.at[pl.ds(0, buf_ref.shape[1]), :], buf_ref.at[slot], sem.at[slot]).wait()

    pos = _tile_positions(route_ref[...], off_ref[:, 0:1])
    sub = lax.broadcasted_iota(jnp.int32, (SUBLANES, T_TOK), 0)
    gates = gates_ref[...]
    packed = jnp.zeros((SUBLANES, T_TOK), F32)
    for k in range(TOP_K):
        packed = jnp.where(sub == k, pos[k].astype(F32), packed)
        packed = jnp.where(sub == TOP_K + k, gates[k:k + 1, :], packed)
    cols = packed.T
    acc = x1_ref[...]
    for c in range(TOP_K * T_TOK // P_ROWS):
        row = (c * P_ROWS + lax.broadcasted_iota(jnp.int32, (T_TOK, P_ROWS), 1)).astype(F32)
        g = jnp.zeros((T_TOK, P_ROWS), F32)
        for k in range(TOP_K):
            g = g + jnp.where(row == cols[:, k:k + 1], cols[:, TOP_K + k:TOP_K + k + 1], 0.0)
        y = _from_tile_rows(buf_ref.at[slot, pl.ds(c * P_ROWS * ROW_TILES, P_ROWS * ROW_TILES), :], P_ROWS)
        acc = acc + jnp.dot(g.astype(BF16), y.astype(BF16), preferred_element_type=F32)
    o_ref[...] = acc


def _combine(x1, route, gates, off_lanes, run_dst, run_len, run_off, ys):
    n = x1.shape[0]
    grid_spec = pltpu.PrefetchScalarGridSpec(
        num_scalar_prefetch=3,
        grid=(n // T_TOK,),
        in_specs=[
            pl.BlockSpec((T_TOK, D_MODEL), lambda i, *_: (i, 0)),
            pl.BlockSpec((SUBLANES, T_TOK), lambda i, *_: (0, i)),
            pl.BlockSpec((SUBLANES, T_TOK), lambda i, *_: (0, i)),
            pl.BlockSpec((N_EXPERTS, LANES), lambda i, *_: (i, 0)),
            pl.BlockSpec(memory_space=pl.ANY),
        ],
        out_specs=pl.BlockSpec((T_TOK, D_MODEL), lambda i, *_: (i, 0)),
        scratch_shapes=[pltpu.VMEM((2, TOP_K * T_TOK * ROW_TILES, LANES), F32), pltpu.SemaphoreType.DMA((2,))],
    )
    return pl.pallas_call(
        _combine_kernel,
        grid_spec=grid_spec,
        out_shape=jax.ShapeDtypeStruct((n, D_MODEL), F32),
        compiler_params=_cparams(("arbitrary",)),
        name="combine",
    )(run_dst, run_len, run_off, x1, route, gates, off_lanes, ys)


def _block_diag(w):
    n, r, _ = w.shape
    eye = jnp.eye(n, dtype=w.dtype)
    return (eye[:, None, :, None] * w[:, :, None, :]).reshape(n * r, n * r)


def kernel(x, norm1_g, w_in, conv_w, conv_b, lru_wa, lru_ba, lru_wx, lru_bx, lru_lambda, q_norm_g, k_norm_g,
           lambda_q1, lambda_k1, lambda_q2, lambda_k2, subln_g, w_out, norm2_g, router_w, router_b, w1, b1, w2, b2):
    bsz, seq, d = x.shape
    n_tok = bsz * seq
    assert d == D_MODEL and n_tok % TM_PROJ == 0 and seq % T_SCAN == 0 and seq % TQ == 0 and n_tok % T_TOK == 0
    assert (n_tok * TOP_K) % BM == 0
    assert norm1_g.shape[0] == 1, "single-layer stack"
    x2 = x.reshape(n_tok, d)

    z = _in_proj(x2, norm1_g[0][None, :], w_in[0].astype(BF16))
    z3 = z.reshape(bsz, seq, D_IN)

    n_slabs = D_RNN // LANES
    per_slab = LANES // RNN_BLOCK
    wa = lru_wa[0].reshape(n_slabs, per_slab, RNN_BLOCK, RNN_BLOCK)
    wx = lru_wx[0].reshape(n_slabs, per_slab, RNN_BLOCK, RNN_BLOCK)
    wg = jnp.concatenate([jax.vmap(_block_diag)(wa), jax.vmap(_block_diag)(wx)], axis=2).astype(BF16)
    bg = jnp.concatenate([lru_ba[0].reshape(n_slabs, 1, LANES), lru_bx[0].reshape(n_slabs, 1, LANES)], axis=2)
    y_rnn = _rnn(z3, conv_w[0], conv_b[0][None, :], wg, bg, lru_lambda[0].reshape(n_slabs, 1, LANES))

    half = jnp.arange(LANES) // HEAD_DIM
    ones_bd = (half[:, None] == half[None, :]).astype(BF16)
    y_attn = _attn(z3, jnp.tile(q_norm_g[0], 2)[None, :], jnp.tile(k_norm_g[0], 2)[None, :],
                   lambda_q1[0][None, :], lambda_k1[0][None, :], lambda_q2[0][None, :], lambda_k2[0][None, :],
                   subln_g[0][None, :], ones_bd)

    rw = jnp.pad(router_w[0], ((0, 0), (0, LANES - N_EXPERTS)))
    rw_hi = rw.astype(BF16)
    rw_lo = (rw - rw_hi.astype(F32)).astype(BF16)
    rb = jnp.pad(router_b[0], (0, LANES - N_EXPERTS))[None, :]
    tok = jnp.arange(T_TOK)
    tri = (tok[:, None] < tok[None, :]).astype(BF16)
    x1, h2, route, gates, counts = _out_proj(
        y_rnn.reshape(n_tok, D_RNN), y_attn.reshape(n_tok, D_ATTN), x2,
        w_out[0].astype(BF16).reshape(2, D_RNN, D_MODEL), norm2_g[0][None, :], rw_hi, rw_lo, rb, tri)

    n_blocks = (n_tok * TOP_K) // BM + N_EXPERTS
    tile_counts = counts[:, 0].reshape(n_tok // T_TOK, N_EXPERTS)
    run_dst, run_len, run_off, off_lanes, blk_e, n_valid, pad_end, padded = _plan(tile_counts, n_blocks)
    xs = _dispatch(h2, route, off_lanes, run_dst, run_len, run_off, pad_end, padded, n_valid, n_blocks * BM)
    ys = _experts(xs, blk_e, n_valid, w1[0], b1[0][:, None, :], w2[0], b2[0][:, None, :])
    out = _combine(x1, route, gates, off_lanes, run_dst, run_len, run_off, ys)
    return out.reshape(bsz, seq, d)
```

```python
import functools
import math

import jax
import jax.numpy as jnp
from jax import lax
from jax.experimental import pallas as pl
from jax.experimental.pallas import tpu as pltpu

F32 = jnp.float32
BF16 = jnp.bfloat16

D_MODEL = 1024
D_RNN = 512
N_RNN_BLOCKS = 8
RNN_BLOCK = 64
CONV_WIDTH = 4
LRU_C = 8.0
HEAD_DIM = 64
N_HEADS = 4
D_ATTN = 512
D_IN = 2 * D_RNN + 3 * D_ATTN
N_EXPERTS = 32
TOP_K = 4
D_FF = 1024
SWIGLU_LIMIT = 7.0
SWIGLU_ALPHA = 1.702
EPS = 1e-5
LAM_INIT = 0.8 - 0.6 * math.exp(0.0)
LOG2_E = math.log2(math.e)

LANES = 128
SUBLANES = 8
ROW_TILES = D_MODEL // LANES
N_DMA_PRIORITIES = 2
VMEM_LIMIT = 52 * 1024 * 1024

TM_PROJ = 512
T_SCAN = 512
TQ = 128
BM = 512
T_TOK = 512
P_ROWS = 256
RUN_BITS = T_TOK.bit_length()


def _cparams(sem):
    return pltpu.CompilerParams(dimension_semantics=sem, vmem_limit_bytes=VMEM_LIMIT)


def _to_tile_rows(ref, x):
    rows = x.shape[0]
    for s in range(ROW_TILES):
        ref[pl.ds(s, rows, stride=ROW_TILES), :] = x[:, s * LANES:(s + 1) * LANES]


def _from_tile_rows(ref, rows):
    return jnp.concatenate([ref[pl.ds(s, rows, stride=ROW_TILES), :] for s in range(ROW_TILES)], axis=1)


def _in_proj_kernel(x_ref, g_ref, w_ref, z_ref):
    x = x_ref[...]
    ms = jnp.mean(x * x, axis=-1, keepdims=True)
    h = x * lax.rsqrt(ms + EPS) * g_ref[...]
    z_ref[...] = jnp.dot(h.astype(BF16), w_ref[...], preferred_element_type=F32)


def _in_proj(x2, g, w_bf):
    n = x2.shape[0]
    return pl.pallas_call(
        _in_proj_kernel,
        grid=(n // TM_PROJ,),
        in_specs=[
            pl.BlockSpec((TM_PROJ, D_MODEL), lambda i: (i, 0)),
            pl.BlockSpec((1, D_MODEL), lambda i: (0, 0)),
            pl.BlockSpec((D_MODEL, D_IN), lambda i: (0, 0)),
        ],
        out_specs=pl.BlockSpec((TM_PROJ, D_IN), lambda i: (i, 0)),
        out_shape=jax.ShapeDtypeStruct((n, D_IN), F32),
        compiler_params=_cparams(("arbitrary",)),
        name="in_proj",
    )(x2, g, w_bf)


def _rnn_kernel(xr_ref, gr_ref, cw_ref, cb_ref, wg_ref, bg_ref, lam_ref, y_ref):
    seq = xr_ref.shape[1]
    n_chunks = seq // T_SCAN
    n_groups = T_SCAN // SUBLANES
    cw = cw_ref[...]
    cb = cb_ref[...]
    nl = -lam_ref[0]
    softplus_neg_lam = jnp.maximum(nl, 0.0) + jnp.log(1.0 + jnp.exp(-jnp.abs(nl)))
    group = lax.broadcasted_iota(jnp.int32, (n_groups, LANES), 0)

    def previous_group(v, first):
        return jnp.where(group >= 1, pltpu.roll(v, 1, axis=0), first)

    def phase_rows(t0, r):
        return pl.ds(t0 + r, n_groups, stride=SUBLANES)

    def chunk(c, carry):
        h_prev, x_tail = carry[0], carry[1:]
        t0 = pl.multiple_of(c * T_SCAN, T_SCAN)
        x = [xr_ref[0, phase_rows(t0, r), :] for r in range(SUBLANES)]
        wrapped = [previous_group(x[SUBLANES - j], x_tail[CONV_WIDTH - 1 - j]) for j in range(1, CONV_WIDTH)]

        def delayed(r, j):
            return x[r - j] if r >= j else wrapped[j - r - 1]

        conv = []
        for r in range(SUBLANES):
            acc = cb + cw[CONV_WIDTH - 1:CONV_WIDTH, :] * x[r]
            for j in range(1, CONV_WIDTH):
                acc = acc + cw[CONV_WIDTH - 1 - j:CONV_WIDTH - j, :] * delayed(r, j)
            conv.append(acc)
        conv = jnp.concatenate(conv, axis=0)
        gates = jnp.dot(conv.astype(BF16), wg_ref[0], preferred_element_type=F32) + bg_ref[0]
        rg = jax.nn.sigmoid(gates[:, :LANES])
        ig = jax.nn.sigmoid(gates[:, LANES:])
        a = jnp.exp(-LRU_C * rg * softplus_neg_lam)
        var = 1.0 - a * a
        u = jnp.where(var > 0.0, var * lax.rsqrt(var), 0.0) * (ig * conv)
        ph = lambda v, r: v[r * n_groups:(r + 1) * n_groups, :]
        a_in, u_in = [ph(a, 0)], [ph(u, 0)]
        for r in range(1, SUBLANES):
            a_in.append(ph(a, r) * a_in[-1])
            u_in.append(ph(a, r) * u_in[-1] + ph(u, r))
        ga, gu = a_in[-1], u_in[-1]
        d = 1
        while d < n_groups:
            keep = group >= d
            gu = jnp.where(keep, ga * pltpu.roll(gu, d, axis=0) + gu, gu)
            ga = jnp.where(keep, ga * pltpu.roll(ga, d, axis=0), ga)
            d *= 2
        h_after = ga * h_prev + gu
        h_before = previous_group(h_after, h_prev)
        for r in range(SUBLANES):
            h = a_in[r] * h_before + u_in[r]
            rows = phase_rows(t0, r)
            y_ref[0, rows, :] = h * jax.nn.gelu(gr_ref[0, rows, :], approximate=True)
        last = slice(n_groups - 1, n_groups)
        return (h_after[last, :],) + tuple(x[SUBLANES - CONV_WIDTH + 1 + j][last, :] for j in range(CONV_WIDTH - 1))

    zero = jnp.zeros((1, LANES), F32)
    lax.fori_loop(0, n_chunks, chunk, (zero,) * CONV_WIDTH)


def _rnn(z3, conv_w, conv_b, wg_slabs, bg_slabs, lam_slabs):
    bsz, seq, _ = z3.shape
    n_slabs = D_RNN // LANES
    slab = lambda rows: pl.BlockSpec((rows, LANES), lambda b, c: (0, c))
    slab3 = lambda shape: pl.BlockSpec((1,) + shape, lambda b, c: (c, 0, 0))
    return pl.pallas_call(
        _rnn_kernel,
        grid=(bsz, n_slabs),
        in_specs=[
            pl.BlockSpec((1, seq, LANES), lambda b, c: (b, 0, c)),
            pl.BlockSpec((1, seq, LANES), lambda b, c: (b, 0, n_slabs + c)),
            slab(CONV_WIDTH),
            slab(1),
            slab3((LANES, 2 * LANES)),
            slab3((1, 2 * LANES)),
            slab3((1, LANES)),
        ],
        out_specs=pl.BlockSpec((1, seq, LANES), lambda b, c: (b, 0, c)),
        out_shape=jax.ShapeDtypeStruct((bsz, seq, D_RNN), F32),
        compiler_params=_cparams(("arbitrary", "arbitrary")),
        name="rnn",
    )(z3, z3, conv_w, conv_b, wg_slabs, bg_slabs, lam_slabs)


def _group_rms(x, ones_bd):
    x2 = x * x
    hi = x2.astype(BF16)
    lo = (x2 - hi.astype(F32)).astype(BF16)
    ssq = jnp.dot(hi, ones_bd, preferred_element_type=F32) + jnp.dot(lo, ones_bd, preferred_element_type=F32)
    return x * lax.rsqrt(ssq * (1.0 / HEAD_DIM) + EPS)


def _attn_kernel(q_ref, k_ref, v_ref, qg_ref, kg_ref, lq1_ref, lk1_ref, lq2_ref, lk2_ref, sg_ref, ones_ref, o_ref):
    seq = q_ref.shape[1]
    ones_bd = ones_ref[...]
    lam = (jnp.exp(jnp.sum(lq1_ref[...] * lk1_ref[...], axis=-1, keepdims=True))
           - jnp.exp(jnp.sum(lq2_ref[...] * lk2_ref[...], axis=-1, keepdims=True)) + LAM_INIT)
    qn = _group_rms(q_ref[0], ones_bd) * qg_ref[...] * (HEAD_DIM ** -0.5 * LOG2_E)
    kn = _group_rms(k_ref[0], ones_bd) * kg_ref[...]
    lane = lax.broadcasted_iota(jnp.int32, (seq, LANES), 1)
    q1 = jnp.where(lane < HEAD_DIM, qn, 0.0).astype(BF16)
    q2 = jnp.where(lane >= HEAD_DIM, qn, 0.0).astype(BF16)
    kb = kn.astype(BF16)
    vb = v_ref[0].astype(BF16)
    sg = sg_ref[...]
    dn = (((1,), (1,)), ((), ()))
    n_blk = seq // TQ

    def scores(qi, qm):
        kv = (qi + 1) * TQ
        return lax.dot_general(qm[qi * TQ:kv], kb[:kv], dn, preferred_element_type=F32)

    causal = (lax.broadcasted_iota(jnp.int32, (TQ, TQ), 1) <= lax.broadcasted_iota(jnp.int32, (TQ, TQ), 0))

    def weights(qi, s1, s2):
        c0 = qi * TQ

        def probs(s):
            diag = jnp.where(causal, s[:, c0:], -jnp.inf)
            m = jnp.max(diag, axis=-1, keepdims=True)
            if qi > 0:
                m = jnp.maximum(m, jnp.max(s[:, :c0], axis=-1, keepdims=True))
            e = jnp.exp2(diag - m)
            l = jnp.sum(e, axis=-1, keepdims=True)
            if qi > 0:
                e_prev = jnp.exp2(s[:, :c0] - m)
                l = l + jnp.sum(e_prev, axis=-1, keepdims=True)
                e = jnp.concatenate([e_prev, e], axis=1)
            return e, l

        e1, l1 = probs(s1)
        e2, l2 = probs(s2)
        return (e1 - e2 * (lam * l1 / l2)).astype(BF16), 1.0 / l1

    def values(qi, w_and_scale):
        w, scale = w_and_scale
        kv = (qi + 1) * TQ
        o = jnp.dot(w, vb[:kv], preferred_element_type=F32) * scale
        o = o * lax.rsqrt(jnp.mean(o * o, axis=-1, keepdims=True) + EPS) * sg * (1.0 - LAM_INIT)
        o_ref[0, qi * TQ:kv, :] = o

    s = {0: (scores(0, q1), scores(0, q2))}
    w = {}
    for t in range(n_blk + 1):
        if t + 1 < n_blk:
            s[t + 1] = (scores(t + 1, q1), scores(t + 1, q2))
        if t < n_blk:
            w[t] = weights(t, *s.pop(t))
        if t >= 1:
            values(t - 1, w.pop(t - 1))


def _attn(z3, qg2, kg2, lq1, lk1, lq2, lk2, sg, ones_bd):
    bsz, seq, _ = z3.shape
    qoff = 2 * D_RNN // LANES
    koff = qoff + D_ATTN // LANES
    voff = koff + D_ATTN // LANES
    const = lambda shape: pl.BlockSpec(shape, lambda b, h: (0,) * len(shape))
    return pl.pallas_call(
        _attn_kernel,
        grid=(bsz, N_HEADS),
        in_specs=[
            pl.BlockSpec((1, seq, LANES), lambda b, h: (b, 0, qoff + h)),
            pl.BlockSpec((1, seq, LANES), lambda b, h: (b, 0, koff + h)),
            pl.BlockSpec((1, seq, LANES), lambda b, h: (b, 0, voff + h)),
            const((1, LANES)), const((1, LANES)),
            const((1, HEAD_DIM)), const((1, HEAD_DIM)), const((1, HEAD_DIM)), const((1, HEAD_DIM)),
            const((1, LANES)), const((LANES, LANES)),
        ],
        out_specs=pl.BlockSpec((1, seq, LANES), lambda b, h: (b, 0, h)),
        out_shape=jax.ShapeDtypeStruct((bsz, seq, D_ATTN), F32),
        compiler_params=_cparams(("arbitrary", "arbitrary")),
        name="attn",
    )(z3, z3, z3, qg2, kg2, lq1, lk1, lq2, lk2, sg, ones_bd)


def _out_proj_kernel(yr_ref, ya_ref, x_ref, wo_ref, g_ref, rw_ref, rb_ref, tri_ref,
                     x1_ref, h2_ref, route_ref, gates_ref, counts_ref):
    acc = jnp.dot(yr_ref[...].astype(BF16), wo_ref[0], preferred_element_type=F32)
    acc = acc + jnp.dot(ya_ref[...].astype(BF16), wo_ref[1], preferred_element_type=F32)
    x1 = x_ref[...] + acc
    x1_ref[...] = x1
    h2 = x1 * lax.rsqrt(jnp.mean(x1 * x1, axis=-1, keepdims=True) + EPS) * g_ref[...]
    hh = h2.astype(BF16)
    h2_ref[...] = hh
    hl = (h2 - hh.astype(F32)).astype(BF16)
    ph = jnp.dot(hh, rw_ref[...], preferred_element_type=F32)
    pl_ = jnp.dot(hl, rw_ref[...], preferred_element_type=F32)
    logits = (ph[:, :LANES] + ph[:, LANES:]) + (pl_[:, :LANES] + pl_[:, LANES:]) + rb_ref[...]
    tm = logits.shape[0]
    l = logits.T[:N_EXPERTS, :]
    eid = lax.broadcasted_iota(jnp.int32, (N_EXPERTS, tm), 0)
    vals, idxs = [], []
    for _ in range(TOP_K):
        m = jnp.max(l, axis=0, keepdims=True)
        idx = jnp.min(jnp.where(l == m, eid, N_EXPERTS), axis=0, keepdims=True)
        vals.append(m)
        idxs.append(idx)
        l = jnp.where(eid == idx, -jnp.inf, l)
    es = [jnp.exp(v - vals[0]) for v in vals]
    inv = 1.0 / (es[0] + es[1] + es[2] + es[3])
    chosen = jnp.zeros((N_EXPERTS, tm), F32)
    for k in range(TOP_K):
        chosen = chosen + (eid == idxs[k]).astype(F32)
    before = jnp.dot(chosen.astype(BF16), tri_ref[...], preferred_element_type=F32)
    sub = lax.broadcasted_iota(jnp.int32, (SUBLANES, tm), 0)
    route = jnp.zeros((SUBLANES, tm), jnp.int32)
    gates = jnp.zeros((SUBLANES, tm), F32)
    for k in range(TOP_K):
        rank = jnp.sum(jnp.where(eid == idxs[k], before, 0.0), axis=0, keepdims=True).astype(jnp.int32)
        route = jnp.where(sub == k, idxs[k], route)
        route = jnp.where(sub == TOP_K + k, rank, route)
        gates = jnp.where(sub == k, es[k] * inv, gates)
    route_ref[...] = route
    gates_ref[...] = gates
    counts_ref[...] = jnp.broadcast_to(jnp.sum(chosen, axis=1, keepdims=True), (N_EXPERTS, LANES)).astype(jnp.int32)


def _out_proj(y_rnn, y_attn, x2, wo_bf, g2, rw_parts, rb, tri):
    n = x2.shape[0]
    n_tiles = n // T_TOK
    row = lambda w: pl.BlockSpec((T_TOK, w), lambda i: (i, 0))
    col = pl.BlockSpec((SUBLANES, T_TOK), lambda i: (0, i))
    const = lambda shape: pl.BlockSpec(shape, lambda i: (0,) * len(shape))
    return pl.pallas_call(
        _out_proj_kernel,
        grid=(n_tiles,),
        in_specs=[row(D_RNN), row(D_ATTN), row(D_MODEL),
                  const((2, D_RNN, D_MODEL)), const((1, D_MODEL)),
                  const((D_MODEL, 2 * LANES)), const((1, LANES)),
                  const((T_TOK, T_TOK))],
        out_specs=[row(D_MODEL), row(D_MODEL), col, col, pl.BlockSpec((N_EXPERTS, LANES), lambda i: (i, 0))],
        out_shape=[jax.ShapeDtypeStruct((n, D_MODEL), F32),
                   jax.ShapeDtypeStruct((n, D_MODEL), BF16),
                   jax.ShapeDtypeStruct((SUBLANES, n), jnp.int32),
                   jax.ShapeDtypeStruct((SUBLANES, n), F32),
                   jax.ShapeDtypeStruct((n_tiles * N_EXPERTS, LANES), jnp.int32)],
        compiler_params=_cparams(("arbitrary",)),
        name="out_proj",
    )(y_rnn, y_attn, x2, wo_bf, g2, rw_parts, rb, tri)


def _plan(tile_counts, n_blocks):
    n_tiles = tile_counts.shape[0]
    counts = jnp.sum(tile_counts, axis=0)
    padded = (counts + BM - 1) // BM * BM
    pad_end = jnp.cumsum(padded).astype(jnp.int32)
    pad_start = pad_end - padded
    earlier_tiles = jnp.cumsum(tile_counts, axis=0) - tile_counts
    run_off = (jnp.cumsum(tile_counts, axis=1) - tile_counts).astype(jnp.int32)
    run_dst = (pad_start[None, :] + earlier_tiles).astype(jnp.int32)
    off_lanes = jnp.broadcast_to(run_off.reshape(n_tiles * N_EXPERTS, 1), (n_tiles * N_EXPERTS, LANES))
    blk_start = jnp.arange(n_blocks, dtype=jnp.int32) * BM
    blk_e = jnp.minimum(jnp.sum((pad_end[None, :] <= blk_start[:, None]).astype(jnp.int32), axis=1), N_EXPERTS - 1)
    n_valid = (pad_end[-1] // BM).reshape(1)
    return (run_dst.reshape(-1), tile_counts.reshape(-1).astype(jnp.int32), run_off.reshape(-1), off_lanes,
            blk_e, n_valid, pad_end, padded)


def _tile_positions(route, off_col):
    eid = lax.broadcasted_iota(jnp.int32, (N_EXPERTS, route.shape[1]), 0)
    pos = []
    for k in range(TOP_K):
        start = jnp.sum(jnp.where(eid == route[k:k + 1, :], off_col, 0), axis=0, keepdims=True)
        pos.append(start + route[TOP_K + k:TOP_K + k + 1, :])
    return pos


def _run_copies(run_len_ref, tile, run_bases, make_copy, act):
    def per_expert(e, carry):
        run = tile * N_EXPERTS + e
        length = run_len_ref[run]
        bases = run_bases(run)
        for b in range(RUN_BITS - 1, -1, -1):
            piece_start = length & ~((2 << b) - 1)

            @pl.when((length & (1 << b)) != 0)
            def _():
                act(make_copy(bases, piece_start, 1 << b))
        return carry

    lax.fori_loop(0, N_EXPERTS, per_expert, 0)


def _dispatch_kernel(run_dst_ref, run_len_ref, run_off_ref, pad_end_ref, padded_ref, n_valid_ref,
                     h_ref, route_ref, off_ref, xs_ref, sorted_ref, zeros_ref, sem, zsem):
    i = pl.program_id(0)
    n_tiles = pl.num_programs(0)
    slot = i % 2
    blk_rows = BM * ROW_TILES
    n_blocks = xs_ref.shape[0] // blk_rows

    def run_bases(run):
        return run_off_ref[run], run_dst_ref[run]

    def run_copy(s, bases, piece_start, size):
        src_row = pl.multiple_of((bases[0] + piece_start) * ROW_TILES, ROW_TILES)
        dst_row = pl.multiple_of((bases[1] + piece_start) * ROW_TILES, ROW_TILES)
        return pltpu.make_async_copy(sorted_ref.at[s, pl.ds(src_row, size * ROW_TILES), :],
                                     xs_ref.at[pl.ds(dst_row, size * ROW_TILES), :], sem.at[s])

    def start_runs(tile, s):
        _run_copies(run_len_ref, tile, run_bases, functools.partial(run_copy, s), lambda cp: cp.start())

    def wait_runs(s):
        pltpu.make_async_copy(sorted_ref.at[s], xs_ref.at[pl.ds(0, sorted_ref.shape[1]), :], sem.at[s]).wait()

    @pl.when(i == 0)
    def _():
        zeros_ref[...] = jnp.zeros_like(zeros_ref)

        def zero_block(blk):
            start = pl.multiple_of(blk * blk_rows, blk_rows)
            cp = pltpu.make_async_copy(zeros_ref, xs_ref.at[pl.ds(start, blk_rows), :], zsem)
            cp.start()
            cp.wait()

        def zero_tail(e, carry):
            @pl.when(padded_ref[e] > 0)
            def _():
                zero_block(pad_end_ref[e] // BM - 1)
            return carry

        def zero_dead(b, carry):
            zero_block(b)
            return carry

        lax.fori_loop(0, N_EXPERTS, zero_tail, 0)
        lax.fori_loop(n_valid_ref[0], n_blocks, zero_dead, 0)

    @pl.when(i >= 2)
    def _():
        wait_runs(slot)

    pos = _tile_positions(route_ref[...], off_ref[:, 0:1])
    hb = h_ref[...]

    for c in range(TOP_K * T_TOK // P_ROWS):
        row = c * P_ROWS + lax.broadcasted_iota(jnp.int32, (P_ROWS, T_TOK), 0)
        hit = row == pos[0]
        for k in range(1, TOP_K):
            hit = hit | (row == pos[k])
        perm = jnp.where(hit, 1.0, 0.0).astype(BF16)
        rows = jnp.dot(perm, hb, preferred_element_type=F32)
        _to_tile_rows(sorted_ref.at[slot, pl.ds(c * P_ROWS * ROW_TILES, P_ROWS * ROW_TILES), :], rows)
    start_runs(i, slot)

    @pl.when(i == n_tiles - 1)
    def _():
        @pl.when(i >= 1)
        def _():
            wait_runs(1 - slot)
        wait_runs(slot)


def _dispatch(h2, route, off_lanes, run_dst, run_len, run_off, pad_end, padded, n_valid, n_rows):
    n = h2.shape[0]
    grid_spec = pltpu.PrefetchScalarGridSpec(
        num_scalar_prefetch=6,
        grid=(n // T_TOK,),
        in_specs=[
            pl.BlockSpec((T_TOK, D_MODEL), lambda i, *_: (i, 0)),
            pl.BlockSpec((SUBLANES, T_TOK), lambda i, *_: (0, i)),
            pl.BlockSpec((N_EXPERTS, LANES), lambda i, *_: (i, 0)),
        ],
        out_specs=pl.BlockSpec(memory_space=pl.ANY),
        scratch_shapes=[pltpu.VMEM((2, TOP_K * T_TOK * ROW_TILES, LANES), F32),
                        pltpu.VMEM((BM * ROW_TILES, LANES), F32),
                        pltpu.SemaphoreType.DMA((2,)), pltpu.SemaphoreType.DMA(())],
    )
    return pl.pallas_call(
        _dispatch_kernel,
        grid_spec=grid_spec,
        out_shape=jax.ShapeDtypeStruct((n_rows * ROW_TILES, LANES), F32),
        compiler_params=_cparams(("arbitrary",)),
        name="dispatch",
    )(run_dst, run_len, run_off, pad_end, padded, n_valid, h2, route, off_lanes)


def _experts_kernel(blk_e_ref, n_valid_ref, xs_ref, w1_ref, b1_ref, w2_ref, b2_ref, y_ref, w1b_ref, w2b_ref):
    i = pl.program_id(0)

    @pl.when(i < n_valid_ref[0])
    def _():
        prev_e = blk_e_ref[jnp.maximum(i - 1, 0)]

        @pl.when((i == 0) | (blk_e_ref[i] != prev_e))
        def _():
            w1b_ref[...] = w1_ref[0].astype(BF16)
            w2b_ref[...] = w2_ref[0].astype(BF16)

        x = _from_tile_rows(xs_ref, BM).astype(BF16)
        hcat = jnp.dot(x, w1b_ref[...], preferred_element_type=F32) + b1_ref[0]
        gate = jnp.minimum(hcat[:, :D_FF], SWIGLU_LIMIT)
        up = jnp.clip(hcat[:, D_FF:], -SWIGLU_LIMIT, SWIGLU_LIMIT)
        act = gate * jax.nn.sigmoid(SWIGLU_ALPHA * gate) * (up + 1.0)
        y = jnp.dot(act.astype(BF16), w2b_ref[...], preferred_element_type=F32) + b2_ref[0]
        _to_tile_rows(y_ref, y)

    @pl.when(i >= n_valid_ref[0])
    def _():
        y_ref[...] = jnp.zeros_like(y_ref)


def _experts(xs, blk_e, n_valid, w1, b1, w2, b2):
    blk_rows = BM * ROW_TILES
    n_blocks = xs.shape[0] // blk_rows
    exp3 = lambda i, be, nv: (be[i], 0, 0)
    grid_spec = pltpu.PrefetchScalarGridSpec(
        num_scalar_prefetch=2,
        grid=(n_blocks,),
        in_specs=[
            pl.BlockSpec((blk_rows, LANES), lambda i, be, nv: (jnp.minimum(i, nv[0] - 1), 0)),
            pl.BlockSpec((1, D_MODEL, 2 * D_FF), exp3),
            pl.BlockSpec((1, 1, 2 * D_FF), exp3),
            pl.BlockSpec((1, D_FF, D_MODEL), exp3),
            pl.BlockSpec((1, 1, D_MODEL), exp3),
        ],
        out_specs=pl.BlockSpec((blk_rows, LANES), lambda i, be, nv: (i, 0)),
        scratch_shapes=[pltpu.VMEM((D_MODEL, 2 * D_FF), BF16), pltpu.VMEM((D_FF, D_MODEL), BF16)],
    )
    return pl.pallas_call(
        _experts_kernel,
        grid_spec=grid_spec,
        out_shape=jax.ShapeDtypeStruct(xs.shape, F32),
        compiler_params=_cparams(("arbitrary",)),
        name="experts",
    )(blk_e, n_valid, xs, w1, b1, w2, b2)


def _combine_kernel(run_src_ref, run_len_ref, run_off_ref, x1_ref, route_ref, gates_ref, off_ref, ys_ref, o_ref,
                    buf_ref, sem):
    i = pl.program_id(0)
    n_tiles = pl.num_programs(0)
    slot = i % 2

    def run_bases(run):
        return run_src_ref[run], run_off_ref[run]

    def run_copy(s, bases, piece_start, size):
        src_row = pl.multiple_of((bases[0] + piece_start) * ROW_TILES, ROW_TILES)
        dst_row = pl.multiple_of((bases[1] + piece_start) * ROW_TILES, ROW_TILES)
        return pltpu.make_async_copy(ys_ref.at[pl.ds(src_row, size * ROW_TILES), :],
                                     buf_ref.at[s, pl.ds(dst_row, size * ROW_TILES), :], sem.at[s])

    def start_runs(tile, s):
        _run_copies(run_len_ref, tile, run_bases, functools.partial(run_copy, s), lambda cp: cp.start())

    @pl.when(i == 0)
    def _():
        start_runs(0, 0)

    @pl.when(i + 1 < n_tiles)
    def _():
        start_runs(i + 1, 1 - slot)

    pltpu.make_async_copy(ys_ref.at[pl.ds(0, buf_ref.shape[1]), :], buf_ref.at[slot], sem.at[slot]).wait()

    pos = _tile_positions(route_ref[...], off_ref[:, 0:1])
    sub = lax.broadcasted_iota(jnp.int32, (SUBLANES, T_TOK), 0)
    gates = gates_ref[...]
    packed = jnp.zeros((SUBLANES, T_TOK), F32)
    for k in range(TOP_K):
        packed = jnp.where(sub == k, pos[k].astype(F32), packed)
        packed = jnp.where(sub == TOP_K + k, gates[k:k + 1, :], packed)
    cols = packed.T
    acc = x1_ref[...]
    for c in range(TOP_K * T_TOK // P_ROWS):
        row = (c * P_ROWS + lax.broadcasted_iota(jnp.int32, (T_TOK, P_ROWS), 1)).astype(F32)
        g = jnp.zeros((T_TOK, P_ROWS), F32)
        for k in range(TOP_K):
            g = g + jnp.where(row == cols[:, k:k + 1], cols[:, TOP_K + k:TOP_K + k + 1], 0.0)
        y = _from_tile_rows(buf_ref.at[slot, pl.ds(c * P_ROWS * ROW_TILES, P_ROWS * ROW_TILES), :], P_ROWS)
        acc = acc + jnp.dot(g.astype(BF16), y.astype(BF16), preferred_element_type=F32)
    o_ref[...] = acc


def _combine(x1, route, gates, off_lanes, run_dst, run_len, run_off, ys):
    n = x1.shape[0]
    grid_spec = pltpu.PrefetchScalarGridSpec(
        num_scalar_prefetch=3,
        grid=(n // T_TOK,),
        in_specs=[
            pl.BlockSpec((T_TOK, D_MODEL), lambda i, *_: (i, 0)),
            pl.BlockSpec((SUBLANES, T_TOK), lambda i, *_: (0, i)),
            pl.BlockSpec((SUBLANES, T_TOK), lambda i, *_: (0, i)),
            pl.BlockSpec((N_EXPERTS, LANES), lambda i, *_: (i, 0)),
            pl.BlockSpec(memory_space=pl.ANY),
        ],
        out_specs=pl.BlockSpec((T_TOK, D_MODEL), lambda i, *_: (i, 0)),
        scratch_shapes=[pltpu.VMEM((2, TOP_K * T_TOK * ROW_TILES, LANES), F32), pltpu.SemaphoreType.DMA((2,))],
    )
    return pl.pallas_call(
        _combine_kernel,
        grid_spec=grid_spec,
        out_shape=jax.ShapeDtypeStruct((n, D_MODEL), F32),
        compiler_params=_cparams(("arbitrary",)),
        name="combine",
    )(run_dst, run_len, run_off, x1, route, gates, off_lanes, ys)


def _block_diag(w):
    n, r, _ = w.shape
    eye = jnp.eye(n, dtype=w.dtype)
    return (eye[:, None, :, None] * w[:, :, None, :]).reshape(n * r, n * r)


def kernel(x, norm1_g, w_in, conv_w, conv_b, lru_wa, lru_ba, lru_wx, lru_bx, lru_lambda, q_norm_g, k_norm_g,
           lambda_q1, lambda_k1, lambda_q2, lambda_k2, subln_g, w_out, norm2_g, router_w, router_b, w1, b1, w2, b2):
    bsz, seq, d = x.shape
    n_tok = bsz * seq
    assert d == D_MODEL and n_tok % TM_PROJ == 0 and seq % T_SCAN == 0 and seq % TQ == 0 and n_tok % T_TOK == 0
    assert (n_tok * TOP_K) % BM == 0
    assert norm1_g.shape[0] == 1, "single-layer stack"
    x2 = x.reshape(n_tok, d)

    z = _in_proj(x2, norm1_g[0][None, :], w_in[0].astype(BF16))
    z3 = z.reshape(bsz, seq, D_IN)

    n_slabs = D_RNN // LANES
    per_slab = LANES // RNN_BLOCK
    wa = lru_wa[0].reshape(n_slabs, per_slab, RNN_BLOCK, RNN_BLOCK)
    wx = lru_wx[0].reshape(n_slabs, per_slab, RNN_BLOCK, RNN_BLOCK)
    wg = jnp.concatenate([jax.vmap(_block_diag)(wa), jax.vmap(_block_diag)(wx)], axis=2).astype(BF16)
    bg = jnp.concatenate([lru_ba[0].reshape(n_slabs, 1, LANES), lru_bx[0].reshape(n_slabs, 1, LANES)], axis=2)
    y_rnn = _rnn(z3, conv_w[0], conv_b[0][None, :], wg, bg, lru_lambda[0].reshape(n_slabs, 1, LANES))

    half = jnp.arange(LANES) // HEAD_DIM
    ones_bd = (half[:, None] == half[None, :]).astype(BF16)
    y_attn = _attn(z3, jnp.tile(q_norm_g[0], 2)[None, :], jnp.tile(k_norm_g[0], 2)[None, :],
                   lambda_q1[0][None, :], lambda_k1[0][None, :], lambda_q2[0][None, :], lambda_k2[0][None, :],
                   subln_g[0][None, :], ones_bd)

    rw = jnp.pad(router_w[0], ((0, 0), (0, LANES - N_EXPERTS)))
    rw_hi = rw.astype(BF16)
    rw_lo = (rw - rw_hi.astype(F32)).astype(BF16)
    rw_parts = jnp.concatenate([rw_hi, rw_lo], axis=1)
    rb = jnp.pad(router_b[0], (0, LANES - N_EXPERTS))[None, :]
    tok = jnp.arange(T_TOK)
    tri = (tok[:, None] < tok[None, :]).astype(BF16)
    x1, h2, route, gates, counts = _out_proj(
        y_rnn.reshape(n_tok, D_RNN), y_attn.reshape(n_tok, D_ATTN), x2,
        w_out[0].astype(BF16).reshape(2, D_RNN, D_MODEL), norm2_g[0][None, :], rw_parts, rb, tri)

    n_blocks = (n_tok * TOP_K) // BM + N_EXPERTS
    tile_counts = counts[:, 0].reshape(n_tok // T_TOK, N_EXPERTS)
    run_dst, run_len, run_off, off_lanes, blk_e, n_valid, pad_end, padded = _plan(tile_counts, n_blocks)
    xs = _dispatch(h2, route, off_lanes, run_dst, run_len, run_off, pad_end, padded, n_valid, n_blocks * BM)
    ys = _experts(xs, blk_e, n_valid, w1[0], b1[0][:, None, :], w2[0], b2[0][:, None, :])
    out = _combine(x1, route, gates, off_lanes, run_dst, run_len, run_off, ys)
    return out.reshape(bsz, seq, d)
```

```python
import functools
import math

import jax
import jax.numpy as jnp
from jax import lax
from jax.experimental import pallas as pl
from jax.experimental.pallas import tpu as pltpu

F32 = jnp.float32
BF16 = jnp.bfloat16

D_MODEL = 1024
D_RNN = 512
N_RNN_BLOCKS = 8
RNN_BLOCK = 64
CONV_WIDTH = 4
LRU_C = 8.0
HEAD_DIM = 64
N_HEADS = 4
D_ATTN = 512
D_IN = 2 * D_RNN + 3 * D_ATTN
N_EXPERTS = 32
TOP_K = 4
D_FF = 1024
SWIGLU_LIMIT = 7.0
SWIGLU_ALPHA = 1.702
EPS = 1e-5
LAM_INIT = 0.8 - 0.6 * math.exp(0.0)
LOG2_E = math.log2(math.e)

LANES = 128
SUBLANES = 8
ROW_TILES = D_MODEL // LANES
N_DMA_PRIORITIES = 2
VMEM_LIMIT = 52 * 1024 * 1024

TM_PROJ = 512
T_SCAN = 512
TQ = 128
BM = 512
T_TOK = 512
P_ROWS = 256
RUN_BITS = T_TOK.bit_length()


def _cparams(sem):
    return pltpu.CompilerParams(dimension_semantics=sem, vmem_limit_bytes=VMEM_LIMIT)


def _to_tile_rows(ref, x):
    rows = x.shape[0]
    for s in range(ROW_TILES):
        ref[pl.ds(s, rows, stride=ROW_TILES), :] = x[:, s * LANES:(s + 1) * LANES]


def _from_tile_rows(ref, rows):
    return jnp.concatenate([ref[pl.ds(s, rows, stride=ROW_TILES), :] for s in range(ROW_TILES)], axis=1)


def _in_proj_kernel(x_ref, g_ref, w_ref, z_ref):
    x = x_ref[...]
    ms = jnp.mean(x * x, axis=-1, keepdims=True)
    h = x * lax.rsqrt(ms + EPS) * g_ref[...]
    z_ref[...] = jnp.dot(h.astype(BF16), w_ref[...], preferred_element_type=F32)


def _in_proj(x2, g, w_bf):
    n = x2.shape[0]
    return pl.pallas_call(
        _in_proj_kernel,
        grid=(n // TM_PROJ,),
        in_specs=[
            pl.BlockSpec((TM_PROJ, D_MODEL), lambda i: (i, 0)),
            pl.BlockSpec((1, D_MODEL), lambda i: (0, 0)),
            pl.BlockSpec((D_MODEL, D_IN), lambda i: (0, 0)),
        ],
        out_specs=pl.BlockSpec((TM_PROJ, D_IN), lambda i: (i, 0)),
        out_shape=jax.ShapeDtypeStruct((n, D_IN), F32),
        compiler_params=_cparams(("arbitrary",)),
        name="in_proj",
    )(x2, g, w_bf)


def _rnn_kernel(xr_ref, gr_ref, cw_ref, cb_ref, wg_ref, bg_ref, lam_ref, y_ref):
    seq = xr_ref.shape[1]
    n_chunks = seq // T_SCAN
    n_groups = T_SCAN // SUBLANES
    cw = cw_ref[...]
    cb = cb_ref[...]
    nl = -lam_ref[0]
    softplus_neg_lam = jnp.maximum(nl, 0.0) + jnp.log(1.0 + jnp.exp(-jnp.abs(nl)))
    group = lax.broadcasted_iota(jnp.int32, (n_groups, LANES), 0)

    def previous_group(v, first):
        return jnp.where(group >= 1, pltpu.roll(v, 1, axis=0), first)

    def phase_rows(t0, r):
        return pl.ds(t0 + r, n_groups, stride=SUBLANES)

    def chunk(c, carry):
        h_prev, x_tail = carry[0], carry[1:]
        t0 = pl.multiple_of(c * T_SCAN, T_SCAN)
        x = [xr_ref[0, phase_rows(t0, r), :] for r in range(SUBLANES)]
        wrapped = [previous_group(x[SUBLANES - j], x_tail[CONV_WIDTH - 1 - j]) for j in range(1, CONV_WIDTH)]

        def delayed(r, j):
            return x[r - j] if r >= j else wrapped[j - r - 1]

        conv = []
        for r in range(SUBLANES):
            acc = cb + cw[CONV_WIDTH - 1:CONV_WIDTH, :] * x[r]
            for j in range(1, CONV_WIDTH):
                acc = acc + cw[CONV_WIDTH - 1 - j:CONV_WIDTH - j, :] * delayed(r, j)
            conv.append(acc)
        conv = jnp.concatenate(conv, axis=0)
        gates = jnp.dot(conv.astype(BF16), wg_ref[0], preferred_element_type=F32) + bg_ref[0]
        rg = jax.nn.sigmoid(gates[:, :LANES])
        ig = jax.nn.sigmoid(gates[:, LANES:])
        a = jnp.exp(-LRU_C * rg * softplus_neg_lam)
        var = 1.0 - a * a
        u = jnp.where(var > 0.0, var * lax.rsqrt(var), 0.0) * (ig * conv)
        ph = lambda v, r: v[r * n_groups:(r + 1) * n_groups, :]
        a_in, u_in = [ph(a, 0)], [ph(u, 0)]
        for r in range(1, SUBLANES):
            a_in.append(ph(a, r) * a_in[-1])
            u_in.append(ph(a, r) * u_in[-1] + ph(u, r))
        ga, gu = a_in[-1], u_in[-1]
        d = 1
        while d < n_groups:
            keep = group >= d
            gu = jnp.where(keep, ga * pltpu.roll(gu, d, axis=0) + gu, gu)
            ga = jnp.where(keep, ga * pltpu.roll(ga, d, axis=0), ga)
            d *= 2
        h_after = ga * h_prev + gu
        h_before = previous_group(h_after, h_prev)
        for r in range(SUBLANES):
            h = a_in[r] * h_before + u_in[r]
            rows = phase_rows(t0, r)
            y_ref[0, rows, :] = h * jax.nn.gelu(gr_ref[0, rows, :], approximate=True)
        last = slice(n_groups - 1, n_groups)
        return (h_after[last, :],) + tuple(x[SUBLANES - CONV_WIDTH + 1 + j][last, :] for j in range(CONV_WIDTH - 1))

    zero = jnp.zeros((1, LANES), F32)
    lax.fori_loop(0, n_chunks, chunk, (zero,) * CONV_WIDTH)


def _rnn(z3, conv_w, conv_b, wg_slabs, bg_slabs, lam_slabs):
    bsz, seq, _ = z3.shape
    n_slabs = D_RNN // LANES
    slab = lambda rows: pl.BlockSpec((rows, LANES), lambda b, c: (0, c))
    slab3 = lambda shape: pl.BlockSpec((1,) + shape, lambda b, c: (c, 0, 0))
    return pl.pallas_call(
        _rnn_kernel,
        grid=(bsz, n_slabs),
        in_specs=[
            pl.BlockSpec((1, seq, LANES), lambda b, c: (b, 0, c)),
            pl.BlockSpec((1, seq, LANES), lambda b, c: (b, 0, n_slabs + c)),
            slab(CONV_WIDTH),
            slab(1),
            slab3((LANES, 2 * LANES)),
            slab3((1, 2 * LANES)),
            slab3((1, LANES)),
        ],
        out_specs=pl.BlockSpec((1, seq, LANES), lambda b, c: (b, 0, c)),
        out_shape=jax.ShapeDtypeStruct((bsz, seq, D_RNN), F32),
        compiler_params=_cparams(("arbitrary", "arbitrary")),
        name="rnn",
    )(z3, z3, conv_w, conv_b, wg_slabs, bg_slabs, lam_slabs)


def _group_rms(x, ones_bd):
    x2 = x * x
    hi = x2.astype(BF16)
    lo = (x2 - hi.astype(F32)).astype(BF16)
    ssq = jnp.dot(hi, ones_bd, preferred_element_type=F32) + jnp.dot(lo, ones_bd, preferred_element_type=F32)
    return x * lax.rsqrt(ssq * (1.0 / HEAD_DIM) + EPS)


def _attn_kernel(q_ref, k_ref, v_ref, qg_ref, kg_ref, lq1_ref, lk1_ref, lq2_ref, lk2_ref, sg_ref, ones_ref, o_ref):
    seq = q_ref.shape[1]
    ones_bd = ones_ref[...]
    lam = (jnp.exp(jnp.sum(lq1_ref[...] * lk1_ref[...], axis=-1, keepdims=True))
           - jnp.exp(jnp.sum(lq2_ref[...] * lk2_ref[...], axis=-1, keepdims=True)) + LAM_INIT)
    qn = _group_rms(q_ref[0], ones_bd) * qg_ref[...] * (HEAD_DIM ** -0.5 * LOG2_E)
    kn = _group_rms(k_ref[0], ones_bd) * kg_ref[...]
    lane = lax.broadcasted_iota(jnp.int32, (seq, LANES), 1)
    q1 = jnp.where(lane < HEAD_DIM, qn, 0.0).astype(BF16)
    q2 = jnp.where(lane >= HEAD_DIM, qn, 0.0).astype(BF16)
    kb = kn.astype(BF16)
    vb = v_ref[0].astype(BF16)
    sg = sg_ref[...]
    dn = (((1,), (1,)), ((), ()))
    n_blk = seq // TQ

    def scores(qi, qm):
        kv = (qi + 1) * TQ
        return lax.dot_general(qm[qi * TQ:kv], kb[:kv], dn, preferred_element_type=F32)

    causal = (lax.broadcasted_iota(jnp.int32, (TQ, TQ), 1) <= lax.broadcasted_iota(jnp.int32, (TQ, TQ), 0))

    def weights(qi, s1, s2):
        c0 = qi * TQ

        def probs(s):
            diag = jnp.where(causal, s[:, c0:], -jnp.inf)
            m = jnp.max(diag, axis=-1, keepdims=True)
            if qi > 0:
                m = jnp.maximum(m, jnp.max(s[:, :c0], axis=-1, keepdims=True))
            e = jnp.exp2(diag - m)
            l = jnp.sum(e, axis=-1, keepdims=True)
            if qi > 0:
                e_prev = jnp.exp2(s[:, :c0] - m)
                l = l + jnp.sum(e_prev, axis=-1, keepdims=True)
                e = jnp.concatenate([e_prev, e], axis=1)
            return e, l

        e1, l1 = probs(s1)
        e2, l2 = probs(s2)
        return (e1 - e2 * (lam * l1 / l2)).astype(BF16), 1.0 / l1

    def values(qi, w_and_scale):
        w, scale = w_and_scale
        kv = (qi + 1) * TQ
        o = jnp.dot(w, vb[:kv], preferred_element_type=F32) * scale
        o = o * lax.rsqrt(jnp.mean(o * o, axis=-1, keepdims=True) + EPS) * sg * (1.0 - LAM_INIT)
        o_ref[0, qi * TQ:kv, :] = o

    s = {0: (scores(0, q1), scores(0, q2))}
    w = {}
    for t in range(n_blk + 1):
        if t + 1 < n_blk:
            s[t + 1] = (scores(t + 1, q1), scores(t + 1, q2))
        if t < n_blk:
            w[t] = weights(t, *s.pop(t))
        if t >= 1:
            values(t - 1, w.pop(t - 1))


def _attn(z3, qg2, kg2, lq1, lk1, lq2, lk2, sg, ones_bd):
    bsz, seq, _ = z3.shape
    qoff = 2 * D_RNN // LANES
    koff = qoff + D_ATTN // LANES
    voff = koff + D_ATTN // LANES
    const = lambda shape: pl.BlockSpec(shape, lambda b, h: (0,) * len(shape))
    return pl.pallas_call(
        _attn_kernel,
        grid=(bsz, N_HEADS),
        in_specs=[
            pl.BlockSpec((1, seq, LANES), lambda b, h: (b, 0, qoff + h)),
            pl.BlockSpec((1, seq, LANES), lambda b, h: (b, 0, koff + h)),
            pl.BlockSpec((1, seq, LANES), lambda b, h: (b, 0, voff + h)),
            const((1, LANES)), const((1, LANES)),
            const((1, HEAD_DIM)), const((1, HEAD_DIM)), const((1, HEAD_DIM)), const((1, HEAD_DIM)),
            const((1, LANES)), const((LANES, LANES)),
        ],
        out_specs=pl.BlockSpec((1, seq, LANES), lambda b, h: (b, 0, h)),
        out_shape=jax.ShapeDtypeStruct((bsz, seq, D_ATTN), F32),
        compiler_params=_cparams(("arbitrary", "arbitrary")),
        name="attn",
    )(z3, z3, z3, qg2, kg2, lq1, lk1, lq2, lk2, sg, ones_bd)


def _out_proj_kernel(yr_ref, ya_ref, x_ref, wo_ref, g_ref, rw_ref, rb_ref, tri_ref,
                     x1_ref, h2_ref, route_ref, gates_ref, counts_ref):
    acc = jnp.dot(yr_ref[...].astype(BF16), wo_ref[0], preferred_element_type=F32)
    acc = acc + jnp.dot(ya_ref[...].astype(BF16), wo_ref[1], preferred_element_type=F32)
    x1 = x_ref[...] + acc
    x1_ref[...] = x1
    h2 = x1 * lax.rsqrt(jnp.mean(x1 * x1, axis=-1, keepdims=True) + EPS) * g_ref[...]
    hh = h2.astype(BF16)
    h2_ref[...] = hh
    hl = (h2 - hh.astype(F32)).astype(BF16)
    ph = jnp.dot(hh, rw_ref[...], preferred_element_type=F32)
    pl_ = jnp.dot(hl, rw_ref[...], preferred_element_type=F32)
    logits = (ph[:, :LANES] + ph[:, LANES:]) + (pl_[:, :LANES] + pl_[:, LANES:]) + rb_ref[...]
    tm = logits.shape[0]
    l = logits.T[:N_EXPERTS, :]
    eid = lax.broadcasted_iota(jnp.int32, (N_EXPERTS, tm), 0)
    vals, idxs = [], []
    for _ in range(TOP_K):
        m = jnp.max(l, axis=0, keepdims=True)
        idx = jnp.min(jnp.where(l == m, eid, N_EXPERTS), axis=0, keepdims=True)
        vals.append(m)
        idxs.append(idx)
        l = jnp.where(eid == idx, -jnp.inf, l)
    es = [jnp.exp(v - vals[0]) for v in vals]
    inv = 1.0 / (es[0] + es[1] + es[2] + es[3])
    chosen = jnp.zeros((N_EXPERTS, tm), F32)
    for k in range(TOP_K):
        chosen = chosen + (eid == idxs[k]).astype(F32)
    before = jnp.dot(chosen.astype(BF16), tri_ref[...], preferred_element_type=F32)
    sub = lax.broadcasted_iota(jnp.int32, (SUBLANES, tm), 0)
    route = jnp.zeros((SUBLANES, tm), jnp.int32)
    gates = jnp.zeros((SUBLANES, tm), F32)
    for k in range(TOP_K):
        rank = jnp.sum(jnp.where(eid == idxs[k], before, 0.0), axis=0, keepdims=True).astype(jnp.int32)
        route = jnp.where(sub == k, idxs[k], route)
        route = jnp.where(sub == TOP_K + k, rank, route)
        gates = jnp.where(sub == k, es[k] * inv, gates)
    route_ref[...] = route
    gates_ref[...] = gates
    counts_ref[...] = jnp.broadcast_to(jnp.sum(chosen, axis=1, keepdims=True), (N_EXPERTS, LANES)).astype(jnp.int32)


def _out_proj(y_rnn, y_attn, x2, wo_bf, g2, rw_parts, rb, tri):
    n = x2.shape[0]
    n_tiles = n // T_TOK
    row = lambda w: pl.BlockSpec((T_TOK, w), lambda i: (i, 0))
    col = pl.BlockSpec((SUBLANES, T_TOK), lambda i: (0, i))
    const = lambda shape: pl.BlockSpec(shape, lambda i: (0,) * len(shape))
    return pl.pallas_call(
        _out_proj_kernel,
        grid=(n_tiles,),
        in_specs=[row(D_RNN), row(D_ATTN), row(D_MODEL),
                  const((2, D_RNN, D_MODEL)), const((1, D_MODEL)),
                  const((D_MODEL, 2 * LANES)), const((1, LANES)),
                  const((T_TOK, T_TOK))],
        out_specs=[row(D_MODEL), row(D_MODEL), col, col, pl.BlockSpec((N_EXPERTS, LANES), lambda i: (i, 0))],
        out_shape=[jax.ShapeDtypeStruct((n, D_MODEL), F32),
                   jax.ShapeDtypeStruct((n, D_MODEL), BF16),
                   jax.ShapeDtypeStruct((SUBLANES, n), jnp.int32),
                   jax.ShapeDtypeStruct((SUBLANES, n), F32),
                   jax.ShapeDtypeStruct((n_tiles * N_EXPERTS, LANES), jnp.int32)],
        compiler_params=_cparams(("arbitrary",)),
        name="out_proj",
    )(y_rnn, y_attn, x2, wo_bf, g2, rw_parts, rb, tri)


def _plan(tile_counts, n_blocks):
    n_tiles = tile_counts.shape[0]
    counts = jnp.sum(tile_counts, axis=0)
    padded = (counts + BM - 1) // BM * BM
    pad_end = jnp.cumsum(padded).astype(jnp.int32)
    pad_start = pad_end - padded
    earlier_tiles = jnp.cumsum(tile_counts, axis=0) - tile_counts
    run_off = (jnp.cumsum(tile_counts, axis=1) - tile_counts).astype(jnp.int32)
    run_dst = (pad_start[None, :] + earlier_tiles).astype(jnp.int32)
    off_lanes = jnp.broadcast_to(run_off.reshape(n_tiles * N_EXPERTS, 1), (n_tiles * N_EXPERTS, LANES))
    blk_start = jnp.arange(n_blocks, dtype=jnp.int32) * BM
    blk_e = jnp.minimum(jnp.sum((pad_end[None, :] <= blk_start[:, None]).astype(jnp.int32), axis=1), N_EXPERTS - 1)
    n_valid = (pad_end[-1] // BM).reshape(1)
    return (run_dst.reshape(-1), tile_counts.reshape(-1).astype(jnp.int32), run_off.reshape(-1), off_lanes,
            blk_e, n_valid, pad_end, padded)


def _tile_positions(route, off_col):
    eid = lax.broadcasted_iota(jnp.int32, (N_EXPERTS, route.shape[1]), 0)
    pos = []
    for k in range(TOP_K):
        start = jnp.sum(jnp.where(eid == route[k:k + 1, :], off_col, 0), axis=0, keepdims=True)
        pos.append(start + route[TOP_K + k:TOP_K + k + 1, :])
    return pos


def _run_copies(run_len_ref, tile, run_bases, make_copy, act):
    def per_expert(e, carry):
        run = tile * N_EXPERTS + e
        length = run_len_ref[run]
        bases = run_bases(run)
        for b in range(RUN_BITS - 1, -1, -1):
            piece_start = length & ~((2 << b) - 1)

            @pl.when((length & (1 << b)) != 0)
            def _():
                act(make_copy(bases, piece_start, 1 << b))
        return carry

    lax.fori_loop(0, N_EXPERTS, per_expert, 0)


def _dispatch_kernel(run_dst_ref, run_len_ref, run_off_ref, pad_end_ref, padded_ref, n_valid_ref,
                     h_ref, route_ref, off_ref, xs_ref, sorted_ref, zeros_ref, sem, zsem):
    i = pl.program_id(0)
    n_tiles = pl.num_programs(0)
    slot = i % 2
    blk_rows = BM * ROW_TILES
    n_blocks = xs_ref.shape[0] // blk_rows

    def run_bases(run):
        return run_off_ref[run], run_dst_ref[run]

    def run_copy(s, bases, piece_start, size):
        src_row = pl.multiple_of((bases[0] + piece_start) * ROW_TILES, ROW_TILES)
        dst_row = pl.multiple_of((bases[1] + piece_start) * ROW_TILES, ROW_TILES)
        return pltpu.make_async_copy(sorted_ref.at[s, pl.ds(src_row, size * ROW_TILES), :],
                                     xs_ref.at[pl.ds(dst_row, size * ROW_TILES), :], sem.at[s])

    def start_runs(tile, s):
        _run_copies(run_len_ref, tile, run_bases, functools.partial(run_copy, s), lambda cp: cp.start())

    def wait_runs(s):
        pltpu.make_async_copy(sorted_ref.at[s], xs_ref.at[pl.ds(0, sorted_ref.shape[1]), :], sem.at[s]).wait()

    @pl.when(i == 0)
    def _():
        zeros_ref[...] = jnp.zeros_like(zeros_ref)

        def zero_copy(blk):
            start = pl.multiple_of(blk * blk_rows, blk_rows)
            return pltpu.make_async_copy(zeros_ref, xs_ref.at[pl.ds(start, blk_rows), :], zsem)

        def zero_blocks(act):
            def tail(e, carry):
                @pl.when(padded_ref[e] > 0)
                def _():
                    act(zero_copy(pad_end_ref[e] // BM - 1))
                return carry

            def dead(b, carry):
                act(zero_copy(b))
                return carry

            lax.fori_loop(0, N_EXPERTS, tail, 0)
            lax.fori_loop(n_valid_ref[0], n_blocks, dead, 0)

        zero_blocks(lambda cp: cp.start())
        zero_blocks(lambda cp: cp.wait())

    @pl.when(i >= 2)
    def _():
        wait_runs(slot)

    pos = _tile_positions(route_ref[...], off_ref[:, 0:1])
    hb = h_ref[...]

    for c in range(TOP_K * T_TOK // P_ROWS):
        row = c * P_ROWS + lax.broadcasted_iota(jnp.int32, (P_ROWS, T_TOK), 0)
        hit = row == pos[0]
        for k in range(1, TOP_K):
            hit = hit | (row == pos[k])
        perm = jnp.where(hit, 1.0, 0.0).astype(BF16)
        rows = jnp.dot(perm, hb, preferred_element_type=F32)
        _to_tile_rows(sorted_ref.at[slot, pl.ds(c * P_ROWS * ROW_TILES, P_ROWS * ROW_TILES), :], rows)
    start_runs(i, slot)

    @pl.when(i == n_tiles - 1)
    def _():
        @pl.when(i >= 1)
        def _():
            wait_runs(1 - slot)
        wait_runs(slot)


def _dispatch(h2, route, off_lanes, run_dst, run_len, run_off, pad_end, padded, n_valid, n_rows):
    n = h2.shape[0]
    grid_spec = pltpu.PrefetchScalarGridSpec(
        num_scalar_prefetch=6,
        grid=(n // T_TOK,),
        in_specs=[
            pl.BlockSpec((T_TOK, D_MODEL), lambda i, *_: (i, 0)),
            pl.BlockSpec((SUBLANES, T_TOK), lambda i, *_: (0, i)),
            pl.BlockSpec((N_EXPERTS, LANES), lambda i, *_: (i, 0)),
        ],
        out_specs=pl.BlockSpec(memory_space=pl.ANY),
        scratch_shapes=[pltpu.VMEM((2, TOP_K * T_TOK * ROW_TILES, LANES), F32),
                        pltpu.VMEM((BM * ROW_TILES, LANES), F32),
                        pltpu.SemaphoreType.DMA((2,)), pltpu.SemaphoreType.DMA(())],
    )
    return pl.pallas_call(
        _dispatch_kernel,
        grid_spec=grid_spec,
        out_shape=jax.ShapeDtypeStruct((n_rows * ROW_TILES, LANES), F32),
        compiler_params=_cparams(("arbitrary",)),
        name="dispatch",
    )(run_dst, run_len, run_off, pad_end, padded, n_valid, h2, route, off_lanes)


def _experts_kernel(blk_e_ref, n_valid_ref, xs_ref, w1_ref, b1_ref, w2_ref, b2_ref, y_ref, w1b_ref, w2b_ref):
    i = pl.program_id(0)

    @pl.when(i < n_valid_ref[0])
    def _():
        prev_e = blk_e_ref[jnp.maximum(i - 1, 0)]

        @pl.when((i == 0) | (blk_e_ref[i] != prev_e))
        def _():
            w1b_ref[...] = w1_ref[0].astype(BF16)
            w2b_ref[...] = w2_ref[0].astype(BF16)

        x = _from_tile_rows(xs_ref, BM).astype(BF16)
        hcat = jnp.dot(x, w1b_ref[...], preferred_element_type=F32) + b1_ref[0]
        gate = jnp.minimum(hcat[:, :D_FF], SWIGLU_LIMIT)
        up = jnp.clip(hcat[:, D_FF:], -SWIGLU_LIMIT, SWIGLU_LIMIT)
        act = gate * jax.nn.sigmoid(SWIGLU_ALPHA * gate) * (up + 1.0)
        y = jnp.dot(act.astype(BF16), w2b_ref[...], preferred_element_type=F32) + b2_ref[0]
        _to_tile_rows(y_ref, y)

    @pl.when(i >= n_valid_ref[0])
    def _():
        y_ref[...] = jnp.zeros_like(y_ref)


def _experts(xs, blk_e, n_valid, w1, b1, w2, b2):
    blk_rows = BM * ROW_TILES
    n_blocks = xs.shape[0] // blk_rows
    exp3 = lambda i, be, nv: (be[i], 0, 0)
    grid_spec = pltpu.PrefetchScalarGridSpec(
        num_scalar_prefetch=2,
        grid=(n_blocks,),
        in_specs=[
            pl.BlockSpec((blk_rows, LANES), lambda i, be, nv: (jnp.minimum(i, nv[0] - 1), 0)),
            pl.BlockSpec((1, D_MODEL, 2 * D_FF), exp3),
            pl.BlockSpec((1, 1, 2 * D_FF), exp3),
            pl.BlockSpec((1, D_FF, D_MODEL), exp3),
            pl.BlockSpec((1, 1, D_MODEL), exp3),
        ],
        out_specs=pl.BlockSpec((blk_rows, LANES), lambda i, be, nv: (i, 0)),
        scratch_shapes=[pltpu.VMEM((D_MODEL, 2 * D_FF), BF16), pltpu.VMEM((D_FF, D_MODEL), BF16)],
    )
    return pl.pallas_call(
        _experts_kernel,
        grid_spec=grid_spec,
        out_shape=jax.ShapeDtypeStruct(xs.shape, F32),
        compiler_params=_cparams(("arbitrary",)),
        name="experts",
    )(blk_e, n_valid, xs, w1, b1, w2, b2)


def _combine_kernel(run_src_ref, run_len_ref, run_off_ref, x1_ref, route_ref, gates_ref, off_ref, ys_ref, o_ref,
                    buf_ref, sem):
    i = pl.program_id(0)
    n_tiles = pl.num_programs(0)
    slot = i % 2

    def run_bases(run):
        return run_src_ref[run], run_off_ref[run]

    def run_copy(s, bases, piece_start, size):
        src_row = pl.multiple_of((bases[0] + piece_start) * ROW_TILES, ROW_TILES)
        dst_row = pl.multiple_of((bases[1] + piece_start) * ROW_TILES, ROW_TILES)
        return pltpu.make_async_copy(ys_ref.at[pl.ds(src_row, size * ROW_TILES), :],
                                     buf_ref.at[s, pl.ds(dst_row, size * ROW_TILES), :], sem.at[s])

    def start_runs(tile, s):
        _run_copies(run_len_ref, tile, run_bases, functools.partial(run_copy, s), lambda cp: cp.start())

    @pl.when(i == 0)
    def _():
        start_runs(0, 0)

    @pl.when(i + 1 < n_tiles)
    def _():
        start_runs(i + 1, 1 - slot)

    pltpu.make_async_copy(ys_ref.at[pl.ds(0, buf_ref.shape[1]), :], buf_ref.at[slot], sem.at[slot]).wait()

    pos = _tile_positions(route_ref[...], off_ref[:, 0:1])
    sub = lax.broadcasted_iota(jnp.int32, (SUBLANES, T_TOK), 0)
    gates = gates_ref[...]
    packed = jnp.zeros((SUBLANES, T_TOK), F32)
    for k in range(TOP_K):
        packed = jnp.where(sub == k, pos[k].astype(F32), packed)
        packed = jnp.where(sub == TOP_K + k, gates[k:k + 1, :], packed)
    cols = packed.T
    acc = x1_ref[...]
    for c in range(TOP_K * T_TOK // P_ROWS):
        row = (c * P_ROWS + lax.broadcasted_iota(jnp.int32, (T_TOK, P_ROWS), 1)).astype(F32)
        g = jnp.zeros((T_TOK, P_ROWS), F32)
        for k in range(TOP_K):
            g = g + jnp.where(row == cols[:, k:k + 1], cols[:, TOP_K + k:TOP_K + k + 1], 0.0)
        y = _from_tile_rows(buf_ref.at[slot, pl.ds(c * P_ROWS * ROW_TILES, P_ROWS * ROW_TILES), :], P_ROWS)
        acc = acc + jnp.dot(g.astype(BF16), y.astype(BF16), preferred_element_type=F32)
    o_ref[...] = acc


def _combine(x1, route, gates, off_lanes, run_dst, run_len, run_off, ys):
    n = x1.shape[0]
    grid_spec = pltpu.PrefetchScalarGridSpec(
        num_scalar_prefetch=3,
        grid=(n // T_TOK,),
        in_specs=[
            pl.BlockSpec((T_TOK, D_MODEL), lambda i, *_: (i, 0)),
            pl.BlockSpec((SUBLANES, T_TOK), lambda i, *_: (0, i)),
            pl.BlockSpec((SUBLANES, T_TOK), lambda i, *_: (0, i)),
            pl.BlockSpec((N_EXPERTS, LANES), lambda i, *_: (i, 0)),
            pl.BlockSpec(memory_space=pl.ANY),
        ],
        out_specs=pl.BlockSpec((T_TOK, D_MODEL), lambda i, *_: (i, 0)),
        scratch_shapes=[pltpu.VMEM((2, TOP_K * T_TOK * ROW_TILES, LANES), F32), pltpu.SemaphoreType.DMA((2,))],
    )
    return pl.pallas_call(
        _combine_kernel,
        grid_spec=grid_spec,
        out_shape=jax.ShapeDtypeStruct((n, D_MODEL), F32),
        compiler_params=_cparams(("arbitrary",)),
        name="combine",
    )(run_dst, run_len, run_off, x1, route, gates, off_lanes, ys)


def _block_diag(w):
    n, r, _ = w.shape
    eye = jnp.eye(n, dtype=w.dtype)
    return (eye[:, None, :, None] * w[:, :, None, :]).reshape(n * r, n * r)


def kernel(x, norm1_g, w_in, conv_w, conv_b, lru_wa, lru_ba, lru_wx, lru_bx, lru_lambda, q_norm_g, k_norm_g,
           lambda_q1, lambda_k1, lambda_q2, lambda_k2, subln_g, w_out, norm2_g, router_w, router_b, w1, b1, w2, b2):
    bsz, seq, d = x.shape
    n_tok = bsz * seq
    assert d == D_MODEL and n_tok % TM_PROJ == 0 and seq % T_SCAN == 0 and seq % TQ == 0 and n_tok % T_TOK == 0
    assert (n_tok * TOP_K) % BM == 0
    assert norm1_g.shape[0] == 1, "single-layer stack"
    x2 = x.reshape(n_tok, d)

    z = _in_proj(x2, norm1_g[0][None, :], w_in[0].astype(BF16))
    z3 = z.reshape(bsz, seq, D_IN)

    n_slabs = D_RNN // LANES
    per_slab = LANES // RNN_BLOCK
    wa = lru_wa[0].reshape(n_slabs, per_slab, RNN_BLOCK, RNN_BLOCK)
    wx = lru_wx[0].reshape(n_slabs, per_slab, RNN_BLOCK, RNN_BLOCK)
    wg = jnp.concatenate([jax.vmap(_block_diag)(wa), jax.vmap(_block_diag)(wx)], axis=2).astype(BF16)
    bg = jnp.concatenate([lru_ba[0].reshape(n_slabs, 1, LANES), lru_bx[0].reshape(n_slabs, 1, LANES)], axis=2)
    y_rnn = _rnn(z3, conv_w[0], conv_b[0][None, :], wg, bg, lru_lambda[0].reshape(n_slabs, 1, LANES))

    half = jnp.arange(LANES) // HEAD_DIM
    ones_bd = (half[:, None] == half[None, :]).astype(BF16)
    y_attn = _attn(z3, jnp.tile(q_norm_g[0], 2)[None, :], jnp.tile(k_norm_g[0], 2)[None, :],
                   lambda_q1[0][None, :], lambda_k1[0][None, :], lambda_q2[0][None, :], lambda_k2[0][None, :],
                   subln_g[0][None, :], ones_bd)

    rw = jnp.pad(router_w[0], ((0, 0), (0, LANES - N_EXPERTS)))
    rw_hi = rw.astype(BF16)
    rw_lo = (rw - rw_hi.astype(F32)).astype(BF16)
    rw_parts = jnp.concatenate([rw_hi, rw_lo], axis=1)
    rb = jnp.pad(router_b[0], (0, LANES - N_EXPERTS))[None, :]
    tok = jnp.arange(T_TOK)
    tri = (tok[:, None] < tok[None, :]).astype(BF16)
    x1, h2, route, gates, counts = _out_proj(
        y_rnn.reshape(n_tok, D_RNN), y_attn.reshape(n_tok, D_ATTN), x2,
        w_out[0].astype(BF16).reshape(2, D_RNN, D_MODEL), norm2_g[0][None, :], rw_parts, rb, tri)

    n_blocks = (n_tok * TOP_K) // BM + N_EXPERTS
    tile_counts = counts[:, 0].reshape(n_tok // T_TOK, N_EXPERTS)
    run_dst, run_len, run_off, off_lanes, blk_e, n_valid, pad_end, padded = _plan(tile_counts, n_blocks)
    xs = _dispatch(h2, route, off_lanes, run_dst, run_len, run_off, pad_end, padded, n_valid, n_blocks * BM)
    ys = _experts(xs, blk_e, n_valid, w1[0], b1[0][:, None, :], w2[0], b2[0][:, None, :])
    out = _combine(x1, route, gates, off_lanes, run_dst, run_len, run_off, ys)
    return out.reshape(bsz, seq, d)
```

```python
import functools
import math

import jax
import jax.numpy as jnp
from jax import lax
from jax.experimental import pallas as pl
from jax.experimental.pallas import tpu as pltpu

F32 = jnp.float32
BF16 = jnp.bfloat16

D_MODEL = 1024
D_RNN = 512
RNN_BLOCK = 64
CONV_WIDTH = 4
LRU_C = 8.0
HEAD_DIM = 64
N_HEADS = 4
D_ATTN = 512
D_IN = 2 * D_RNN + 3 * D_ATTN
N_EXPERTS = 32
TOP_K = 4
D_FF = 1024
SWIGLU_LIMIT = 7.0
SWIGLU_ALPHA = 1.702
EPS = 1e-5
LAM_INIT = 0.8 - 0.6 * math.exp(0.0)
LOG2_E = math.log2(math.e)

LANES = 128
SUBLANES = 8
ROW_TILES = D_MODEL // LANES
VMEM_LIMIT = 52 * 1024 * 1024

TM_PROJ = 1024
T_SCAN = 512
TQ = 128
BM = 512
T_TOK = 512
P_ROWS = 256
RUN_BITS = T_TOK.bit_length()


def _cparams(sem):
    return pltpu.CompilerParams(dimension_semantics=sem, vmem_limit_bytes=VMEM_LIMIT)


def _to_tile_rows(ref, x):
    rows = x.shape[0]
    for s in range(ROW_TILES):
        ref[pl.ds(s, rows, stride=ROW_TILES), :] = x[:, s * LANES:(s + 1) * LANES]


def _from_tile_rows(ref, rows):
    return jnp.concatenate([ref[pl.ds(s, rows, stride=ROW_TILES), :] for s in range(ROW_TILES)], axis=1)


def _in_proj_kernel(x_ref, g_ref, w_ref, z_ref):
    x = x_ref[...]
    ms = jnp.mean(x * x, axis=-1, keepdims=True)
    h = x * lax.rsqrt(ms + EPS) * g_ref[...]
    z_ref[...] = jnp.dot(h.astype(BF16), w_ref[...], preferred_element_type=F32)


def _in_proj(x2, g, w_bf):
    n = x2.shape[0]
    return pl.pallas_call(
        _in_proj_kernel,
        grid=(n // TM_PROJ,),
        in_specs=[
            pl.BlockSpec((TM_PROJ, D_MODEL), lambda i: (i, 0)),
            pl.BlockSpec((1, D_MODEL), lambda i: (0, 0)),
            pl.BlockSpec((D_MODEL, D_IN), lambda i: (0, 0)),
        ],
        out_specs=pl.BlockSpec((TM_PROJ, D_IN), lambda i: (i, 0)),
        out_shape=jax.ShapeDtypeStruct((n, D_IN), F32),
        compiler_params=_cparams(("arbitrary",)),
        name="in_proj",
    )(x2, g, w_bf)


def _rnn_kernel(xr_ref, gr_ref, cw_ref, cb_ref, wg_ref, bg_ref, lam_ref, y_ref):
    seq = xr_ref.shape[1]
    n_chunks = seq // T_SCAN
    n_groups = T_SCAN // SUBLANES
    cw = cw_ref[...]
    cb = cb_ref[...]
    nl = -lam_ref[0]
    softplus_neg_lam = jnp.maximum(nl, 0.0) + jnp.log(1.0 + jnp.exp(-jnp.abs(nl)))
    group = lax.broadcasted_iota(jnp.int32, (n_groups, LANES), 0)

    def previous_group(v, first):
        return jnp.where(group >= 1, pltpu.roll(v, 1, axis=0), first)

    def phase_rows(t0, r):
        return pl.ds(t0 + r, n_groups, stride=SUBLANES)

    def chunk(c, carry):
        h_prev, x_tail = carry[0], carry[1:]
        t0 = pl.multiple_of(c * T_SCAN, T_SCAN)
        x = [xr_ref[0, phase_rows(t0, r), :] for r in range(SUBLANES)]
        wrapped = [previous_group(x[SUBLANES - j], x_tail[CONV_WIDTH - 1 - j]) for j in range(1, CONV_WIDTH)]

        def delayed(r, j):
            return x[r - j] if r >= j else wrapped[j - r - 1]

        conv = []
        for r in range(SUBLANES):
            acc = cb + cw[CONV_WIDTH - 1:CONV_WIDTH, :] * x[r]
            for j in range(1, CONV_WIDTH):
                acc = acc + cw[CONV_WIDTH - 1 - j:CONV_WIDTH - j, :] * delayed(r, j)
            conv.append(acc)
        conv = jnp.concatenate(conv, axis=0)
        gates = jnp.dot(conv.astype(BF16), wg_ref[0], preferred_element_type=F32) + bg_ref[0]
        rg = jax.nn.sigmoid(gates[:, :LANES])
        ig = jax.nn.sigmoid(gates[:, LANES:])
        a = jnp.exp(-LRU_C * rg * softplus_neg_lam)
        var = 1.0 - a * a
        u = jnp.where(var > 0.0, var * lax.rsqrt(var), 0.0) * (ig * conv)
        ph = lambda v, r: v[r * n_groups:(r + 1) * n_groups, :]
        a_in, u_in = [ph(a, 0)], [ph(u, 0)]
        for r in range(1, SUBLANES):
            a_in.append(ph(a, r) * a_in[-1])
            u_in.append(ph(a, r) * u_in[-1] + ph(u, r))
        ga, gu = a_in[-1], u_in[-1]
        d = 1
        while d < n_groups:
            keep = group >= d
            gu = jnp.where(keep, ga * pltpu.roll(gu, d, axis=0) + gu, gu)
            ga = jnp.where(keep, ga * pltpu.roll(ga, d, axis=0), ga)
            d *= 2
        h_after = ga * h_prev + gu
        h_before = previous_group(h_after, h_prev)
        for r in range(SUBLANES):
            h = a_in[r] * h_before + u_in[r]
            rows = phase_rows(t0, r)
            y_ref[0, rows, :] = h * jax.nn.gelu(gr_ref[0, rows, :], approximate=True)
        last = slice(n_groups - 1, n_groups)
        return (h_after[last, :],) + tuple(x[SUBLANES - CONV_WIDTH + 1 + j][last, :] for j in range(CONV_WIDTH - 1))

    zero = jnp.zeros((1, LANES), F32)
    lax.fori_loop(0, n_chunks, chunk, (zero,) * CONV_WIDTH)


def _rnn(z3, conv_w, conv_b, wg_slabs, bg_slabs, lam_slabs):
    bsz, seq, _ = z3.shape
    n_slabs = D_RNN // LANES
    slab = lambda rows: pl.BlockSpec((rows, LANES), lambda b, c: (0, c))
    slab3 = lambda shape: pl.BlockSpec((1,) + shape, lambda b, c: (c, 0, 0))
    return pl.pallas_call(
        _rnn_kernel,
        grid=(bsz, n_slabs),
        in_specs=[
            pl.BlockSpec((1, seq, LANES), lambda b, c: (b, 0, c)),
            pl.BlockSpec((1, seq, LANES), lambda b, c: (b, 0, n_slabs + c)),
            slab(CONV_WIDTH),
            slab(1),
            slab3((LANES, 2 * LANES)),
            slab3((1, 2 * LANES)),
            slab3((1, LANES)),
        ],
        out_specs=pl.BlockSpec((1, seq, LANES), lambda b, c: (b, 0, c)),
        out_shape=jax.ShapeDtypeStruct((bsz, seq, D_RNN), F32),
        compiler_params=_cparams(("arbitrary", "arbitrary")),
        name="rnn",
    )(z3, z3, conv_w, conv_b, wg_slabs, bg_slabs, lam_slabs)


def _group_rms(x, ones_bd):
    x2 = x * x
    hi = x2.astype(BF16)
    lo = (x2 - hi.astype(F32)).astype(BF16)
    ssq = jnp.dot(hi, ones_bd, preferred_element_type=F32) + jnp.dot(lo, ones_bd, preferred_element_type=F32)
    return x * lax.rsqrt(ssq * (1.0 / HEAD_DIM) + EPS)


def _attn_kernel(q_ref, k_ref, v_ref, qg_ref, kg_ref, lq1_ref, lk1_ref, lq2_ref, lk2_ref, sg_ref, ones_ref, o_ref):
    seq = q_ref.shape[1]
    ones_bd = ones_ref[...]
    lam = (jnp.exp(jnp.sum(lq1_ref[...] * lk1_ref[...], axis=-1, keepdims=True))
           - jnp.exp(jnp.sum(lq2_ref[...] * lk2_ref[...], axis=-1, keepdims=True)) + LAM_INIT)
    qn = _group_rms(q_ref[0], ones_bd) * qg_ref[...] * (HEAD_DIM ** -0.5 * LOG2_E)
    kn = _group_rms(k_ref[0], ones_bd) * kg_ref[...]
    lane = lax.broadcasted_iota(jnp.int32, (seq, LANES), 1)
    q1 = jnp.where(lane < HEAD_DIM, qn, 0.0).astype(BF16)
    q2 = jnp.where(lane >= HEAD_DIM, qn, 0.0).astype(BF16)
    kb = kn.astype(BF16)
    vb = v_ref[0].astype(BF16)
    sg = sg_ref[...]
    dn = (((1,), (1,)), ((), ()))
    n_blk = seq // TQ

    def scores(qi, qm):
        kv = (qi + 1) * TQ
        return lax.dot_general(qm[qi * TQ:kv], kb[:kv], dn, preferred_element_type=F32)

    causal = (lax.broadcasted_iota(jnp.int32, (TQ, TQ), 1) <= lax.broadcasted_iota(jnp.int32, (TQ, TQ), 0))

    def weights(qi, s1, s2):
        c0 = qi * TQ

        def probs(s):
            diag = jnp.where(causal, s[:, c0:], -jnp.inf)
            m = jnp.max(diag, axis=-1, keepdims=True)
            if qi > 0:
                m = jnp.maximum(m, jnp.max(s[:, :c0], axis=-1, keepdims=True))
            e = jnp.exp2(diag - m)
            l = jnp.sum(e, axis=-1, keepdims=True)
            if qi > 0:
                e_prev = jnp.exp2(s[:, :c0] - m)
                l = l + jnp.sum(e_prev, axis=-1, keepdims=True)
                e = jnp.concatenate([e_prev, e], axis=1)
            return e, l

        e1, l1 = probs(s1)
        e2, l2 = probs(s2)
        return (e1 - e2 * (lam * l1 / l2)).astype(BF16), 1.0 / l1

    def values(qi, w_and_scale):
        w, scale = w_and_scale
        kv = (qi + 1) * TQ
        o = jnp.dot(w, vb[:kv], preferred_element_type=F32) * scale
        o = o * lax.rsqrt(jnp.mean(o * o, axis=-1, keepdims=True) + EPS) * sg * (1.0 - LAM_INIT)
        o_ref[0, qi * TQ:kv, :] = o

    s = {0: (scores(0, q1), scores(0, q2))}
    w = {}
    for t in range(n_blk + 1):
        if t + 1 < n_blk:
            s[t + 1] = (scores(t + 1, q1), scores(t + 1, q2))
        if t < n_blk:
            w[t] = weights(t, *s.pop(t))
        if t >= 1:
            values(t - 1, w.pop(t - 1))


def _attn(z3, qg2, kg2, lq1, lk1, lq2, lk2, sg, ones_bd):
    bsz, seq, _ = z3.shape
    qoff = 2 * D_RNN // LANES
    koff = qoff + D_ATTN // LANES
    voff = koff + D_ATTN // LANES
    const = lambda shape: pl.BlockSpec(shape, lambda b, h: (0,) * len(shape))
    return pl.pallas_call(
        _attn_kernel,
        grid=(bsz, N_HEADS),
        in_specs=[
            pl.BlockSpec((1, seq, LANES), lambda b, h: (b, 0, qoff + h)),
            pl.BlockSpec((1, seq, LANES), lambda b, h: (b, 0, koff + h)),
            pl.BlockSpec((1, seq, LANES), lambda b, h: (b, 0, voff + h)),
            const((1, LANES)), const((1, LANES)),
            const((1, HEAD_DIM)), const((1, HEAD_DIM)), const((1, HEAD_DIM)), const((1, HEAD_DIM)),
            const((1, LANES)), const((LANES, LANES)),
        ],
        out_specs=pl.BlockSpec((1, seq, LANES), lambda b, h: (b, 0, h)),
        out_shape=jax.ShapeDtypeStruct((bsz, seq, D_ATTN), F32),
        compiler_params=_cparams(("arbitrary", "arbitrary")),
        name="attn",
    )(z3, z3, z3, qg2, kg2, lq1, lk1, lq2, lk2, sg, ones_bd)


def _out_proj_kernel(yr_ref, ya_ref, x_ref, wo_ref, g_ref, rw_ref, rb_ref, tri_ref,
                     x1_ref, h2_ref, route_ref, gates_ref, counts_ref):
    acc = jnp.dot(yr_ref[...].astype(BF16), wo_ref[0], preferred_element_type=F32)
    acc = acc + jnp.dot(ya_ref[...].astype(BF16), wo_ref[1], preferred_element_type=F32)
    x1 = x_ref[...] + acc
    x1_ref[...] = x1
    h2 = x1 * lax.rsqrt(jnp.mean(x1 * x1, axis=-1, keepdims=True) + EPS) * g_ref[...]
    hh = h2.astype(BF16)
    h2_ref[...] = hh
    hl = (h2 - hh.astype(F32)).astype(BF16)
    ph = jnp.dot(hh, rw_ref[...], preferred_element_type=F32)
    pl_ = jnp.dot(hl, rw_ref[...], preferred_element_type=F32)
    logits = (ph[:, :LANES] + ph[:, LANES:]) + (pl_[:, :LANES] + pl_[:, LANES:]) + rb_ref[...]
    tm = logits.shape[0]
    l = logits.T[:N_EXPERTS, :]
    eid = lax.broadcasted_iota(jnp.int32, (N_EXPERTS, tm), 0)
    vals, idxs = [], []
    for _ in range(TOP_K):
        m = jnp.max(l, axis=0, keepdims=True)
        idx = jnp.min(jnp.where(l == m, eid, N_EXPERTS), axis=0, keepdims=True)
        vals.append(m)
        idxs.append(idx)
        l = jnp.where(eid == idx, -jnp.inf, l)
    es = [jnp.exp(v - vals[0]) for v in vals]
    inv = 1.0 / (es[0] + es[1] + es[2] + es[3])
    chosen = jnp.zeros((N_EXPERTS, tm), F32)
    for k in range(TOP_K):
        chosen = chosen + (eid == idxs[k]).astype(F32)
    before = jnp.dot(chosen.astype(BF16), tri_ref[...], preferred_element_type=F32)
    sub = lax.broadcasted_iota(jnp.int32, (SUBLANES, tm), 0)
    route = jnp.zeros((SUBLANES, tm), jnp.int32)
    gates = jnp.zeros((SUBLANES, tm), F32)
    for k in range(TOP_K):
        rank = jnp.sum(jnp.where(eid == idxs[k], before, 0.0), axis=0, keepdims=True).astype(jnp.int32)
        route = jnp.where(sub == k, idxs[k], route)
        route = jnp.where(sub == TOP_K + k, rank, route)
        gates = jnp.where(sub == k, es[k] * inv, gates)
    route_ref[...] = route
    gates_ref[...] = gates
    counts_ref[...] = jnp.broadcast_to(jnp.sum(chosen, axis=1, keepdims=True), (N_EXPERTS, LANES)).astype(jnp.int32)


def _out_proj(y_rnn, y_attn, x2, wo_bf, g2, rw_parts, rb, tri):
    n = x2.shape[0]
    n_tiles = n // T_TOK
    row = lambda w: pl.BlockSpec((T_TOK, w), lambda i: (i, 0))
    col = pl.BlockSpec((SUBLANES, T_TOK), lambda i: (0, i))
    const = lambda shape: pl.BlockSpec(shape, lambda i: (0,) * len(shape))
    return pl.pallas_call(
        _out_proj_kernel,
        grid=(n_tiles,),
        in_specs=[row(D_RNN), row(D_ATTN), row(D_MODEL),
                  const((2, D_RNN, D_MODEL)), const((1, D_MODEL)),
                  const((D_MODEL, 2 * LANES)), const((1, LANES)),
                  const((T_TOK, T_TOK))],
        out_specs=[row(D_MODEL), row(D_MODEL), col, col, pl.BlockSpec((N_EXPERTS, LANES), lambda i: (i, 0))],
        out_shape=[jax.ShapeDtypeStruct((n, D_MODEL), F32),
                   jax.ShapeDtypeStruct((n, D_MODEL), BF16),
                   jax.ShapeDtypeStruct((SUBLANES, n), jnp.int32),
                   jax.ShapeDtypeStruct((SUBLANES, n), F32),
                   jax.ShapeDtypeStruct((n_tiles * N_EXPERTS, LANES), jnp.int32)],
        compiler_params=_cparams(("arbitrary",)),
        name="out_proj",
    )(y_rnn, y_attn, x2, wo_bf, g2, rw_parts, rb, tri)


def _plan(tile_counts, n_blocks):
    n_tiles = tile_counts.shape[0]
    counts = jnp.sum(tile_counts, axis=0)
    padded = (counts + BM - 1) // BM * BM
    pad_end = jnp.cumsum(padded).astype(jnp.int32)
    pad_start = pad_end - padded
    earlier_tiles = jnp.cumsum(tile_counts, axis=0) - tile_counts
    run_off = (jnp.cumsum(tile_counts, axis=1) - tile_counts).astype(jnp.int32)
    run_dst = (pad_start[None, :] + earlier_tiles).astype(jnp.int32)
    off_lanes = jnp.broadcast_to(run_off.reshape(n_tiles * N_EXPERTS, 1), (n_tiles * N_EXPERTS, LANES))
    blk_start = jnp.arange(n_blocks, dtype=jnp.int32) * BM
    blk_e = jnp.minimum(jnp.sum((pad_end[None, :] <= blk_start[:, None]).astype(jnp.int32), axis=1), N_EXPERTS - 1)
    n_valid = (pad_end[-1] // BM).reshape(1)
    return (run_dst.reshape(-1), tile_counts.reshape(-1).astype(jnp.int32), run_off.reshape(-1), off_lanes,
            blk_e, n_valid, pad_end, padded)


def _tile_positions(route, off_col):
    eid = lax.broadcasted_iota(jnp.int32, (N_EXPERTS, route.shape[1]), 0)
    pos = []
    for k in range(TOP_K):
        start = jnp.sum(jnp.where(eid == route[k:k + 1, :], off_col, 0), axis=0, keepdims=True)
        pos.append(start + route[TOP_K + k:TOP_K + k + 1, :])
    return pos


def _run_copies(run_len_ref, tile, run_bases, make_copy, act):
    def per_expert(e, carry):
        run = tile * N_EXPERTS + e
        length = run_len_ref[run]
        bases = run_bases(run)
        for b in range(RUN_BITS - 1, -1, -1):
            piece_start = length & ~((2 << b) - 1)

            @pl.when((length & (1 << b)) != 0)
            def _():
                act(make_copy(bases, piece_start, 1 << b))
        return carry

    lax.fori_loop(0, N_EXPERTS, per_expert, 0)


def _dispatch_kernel(run_dst_ref, run_len_ref, run_off_ref, pad_end_ref, padded_ref, n_valid_ref,
                     h_ref, route_ref, off_ref, xs_ref, sorted_ref, zeros_ref, sem, zsem):
    i = pl.program_id(0)
    n_tiles = pl.num_programs(0)
    slot = i % 2
    blk_rows = BM * ROW_TILES
    n_blocks = xs_ref.shape[0] // blk_rows

    def run_bases(run):
        return run_off_ref[run], run_dst_ref[run]

    def run_copy(s, bases, piece_start, size):
        src_row = pl.multiple_of((bases[0] + piece_start) * ROW_TILES, ROW_TILES)
        dst_row = pl.multiple_of((bases[1] + piece_start) * ROW_TILES, ROW_TILES)
        return pltpu.make_async_copy(sorted_ref.at[s, pl.ds(src_row, size * ROW_TILES), :],
                                     xs_ref.at[pl.ds(dst_row, size * ROW_TILES), :], sem.at[s])

    def start_runs(tile, s):
        _run_copies(run_len_ref, tile, run_bases, functools.partial(run_copy, s), lambda cp: cp.start())

    def wait_runs(s):
        pltpu.make_async_copy(sorted_ref.at[s], xs_ref.at[pl.ds(0, sorted_ref.shape[1]), :], sem.at[s]).wait()

    @pl.when(i == 0)
    def _():
        zeros_ref[...] = jnp.zeros_like(zeros_ref)

        def zero_copy(blk):
            start = pl.multiple_of(blk * blk_rows, blk_rows)
            return pltpu.make_async_copy(zeros_ref, xs_ref.at[pl.ds(start, blk_rows), :], zsem)

        def zero_blocks(act):
            def tail(e, carry):
                @pl.when(padded_ref[e] > 0)
                def _():
                    act(zero_copy(pad_end_ref[e] // BM - 1))
                return carry

            def dead(b, carry):
                act(zero_copy(b))
                return carry

            lax.fori_loop(0, N_EXPERTS, tail, 0)
            lax.fori_loop(n_valid_ref[0], n_blocks, dead, 0)

        zero_blocks(lambda cp: cp.start())
        zero_blocks(lambda cp: cp.wait())

    @pl.when(i >= 2)
    def _():
        wait_runs(slot)

    pos = _tile_positions(route_ref[...], off_ref[:, 0:1])
    hb = h_ref[...]

    for c in range(TOP_K * T_TOK // P_ROWS):
        row = c * P_ROWS + lax.broadcasted_iota(jnp.int32, (P_ROWS, T_TOK), 0)
        hit = row == pos[0]
        for k in range(1, TOP_K):
            hit = hit | (row == pos[k])
        perm = jnp.where(hit, 1.0, 0.0).astype(BF16)
        rows = jnp.dot(perm, hb, preferred_element_type=F32)
        _to_tile_rows(sorted_ref.at[slot, pl.ds(c * P_ROWS * ROW_TILES, P_ROWS * ROW_TILES), :], rows)
    start_runs(i, slot)

    @pl.when(i == n_tiles - 1)
    def _():
        @pl.when(i >= 1)
        def _():
            wait_runs(1 - slot)
        wait_runs(slot)


def _dispatch(h2, route, off_lanes, run_dst, run_len, run_off, pad_end, padded, n_valid, n_rows):
    n = h2.shape[0]
    grid_spec = pltpu.PrefetchScalarGridSpec(
        num_scalar_prefetch=6,
        grid=(n // T_TOK,),
        in_specs=[
            pl.BlockSpec((T_TOK, D_MODEL), lambda i, *_: (i, 0)),
            pl.BlockSpec((SUBLANES, T_TOK), lambda i, *_: (0, i)),
            pl.BlockSpec((N_EXPERTS, LANES), lambda i, *_: (i, 0)),
        ],
        out_specs=pl.BlockSpec(memory_space=pl.ANY),
        scratch_shapes=[pltpu.VMEM((2, TOP_K * T_TOK * ROW_TILES, LANES), F32),
                        pltpu.VMEM((BM * ROW_TILES, LANES), F32),
                        pltpu.SemaphoreType.DMA((2,)), pltpu.SemaphoreType.DMA(())],
    )
    return pl.pallas_call(
        _dispatch_kernel,
        grid_spec=grid_spec,
        out_shape=jax.ShapeDtypeStruct((n_rows * ROW_TILES, LANES), F32),
        compiler_params=_cparams(("arbitrary",)),
        name="dispatch",
    )(run_dst, run_len, run_off, pad_end, padded, n_valid, h2, route, off_lanes)


def _experts_kernel(blk_e_ref, n_valid_ref, xs_ref, w1_ref, b1_ref, w2_ref, b2_ref, y_ref, w1b_ref, w2b_ref):
    i = pl.program_id(0)

    @pl.when(i < n_valid_ref[0])
    def _():
        prev_e = blk_e_ref[jnp.maximum(i - 1, 0)]

        @pl.when((i == 0) | (blk_e_ref[i] != prev_e))
        def _():
            w1b_ref[...] = w1_ref[0].astype(BF16)
            w2b_ref[...] = w2_ref[0].astype(BF16)

        x = _from_tile_rows(xs_ref, BM).astype(BF16)
        hcat = jnp.dot(x, w1b_ref[...], preferred_element_type=F32) + b1_ref[0]
        gate = jnp.minimum(hcat[:, :D_FF], SWIGLU_LIMIT)
        up = jnp.clip(hcat[:, D_FF:], -SWIGLU_LIMIT, SWIGLU_LIMIT)
        act = gate * jax.nn.sigmoid(SWIGLU_ALPHA * gate) * (up + 1.0)
        y = jnp.dot(act.astype(BF16), w2b_ref[...], preferred_element_type=F32) + b2_ref[0]
        _to_tile_rows(y_ref, y)

    @pl.when(i >= n_valid_ref[0])
    def _():
        y_ref[...] = jnp.zeros_like(y_ref)


def _experts(xs, blk_e, n_valid, w1, b1, w2, b2):
    blk_rows = BM * ROW_TILES
    n_blocks = xs.shape[0] // blk_rows
    exp3 = lambda i, be, nv: (be[i], 0, 0)
    grid_spec = pltpu.PrefetchScalarGridSpec(
        num_scalar_prefetch=2,
        grid=(n_blocks,),
        in_specs=[
            pl.BlockSpec((blk_rows, LANES), lambda i, be, nv: (jnp.minimum(i, nv[0] - 1), 0)),
            pl.BlockSpec((1, D_MODEL, 2 * D_FF), exp3),
            pl.BlockSpec((1, 1, 2 * D_FF), exp3),
            pl.BlockSpec((1, D_FF, D_MODEL), exp3),
            pl.BlockSpec((1, 1, D_MODEL), exp3),
        ],
        out_specs=pl.BlockSpec((blk_rows, LANES), lambda i, be, nv: (i, 0)),
        scratch_shapes=[pltpu.VMEM((D_MODEL, 2 * D_FF), BF16), pltpu.VMEM((D_FF, D_MODEL), BF16)],
    )
    return pl.pallas_call(
        _experts_kernel,
        grid_spec=grid_spec,
        out_shape=jax.ShapeDtypeStruct(xs.shape, F32),
        compiler_params=_cparams(("arbitrary",)),
        name="experts",
    )(blk_e, n_valid, xs, w1, b1, w2, b2)


def _combine_kernel(run_src_ref, run_len_ref, run_off_ref, x1_ref, route_ref, gates_ref, off_ref, ys_ref, o_ref,
                    buf_ref, sem):
    i = pl.program_id(0)
    n_tiles = pl.num_programs(0)
    slot = i % 2

    def run_bases(run):
        return run_src_ref[run], run_off_ref[run]

    def run_copy(s, bases, piece_start, size):
        src_row = pl.multiple_of((bases[0] + piece_start) * ROW_TILES, ROW_TILES)
        dst_row = pl.multiple_of((bases[1] + piece_start) * ROW_TILES, ROW_TILES)
        return pltpu.make_async_copy(ys_ref.at[pl.ds(src_row, size * ROW_TILES), :],
                                     buf_ref.at[s, pl.ds(dst_row, size * ROW_TILES), :], sem.at[s])

    def start_runs(tile, s):
        _run_copies(run_len_ref, tile, run_bases, functools.partial(run_copy, s), lambda cp: cp.start())

    @pl.when(i == 0)
    def _():
        start_runs(0, 0)

    @pl.when(i + 1 < n_tiles)
    def _():
        start_runs(i + 1, 1 - slot)

    pltpu.make_async_copy(ys_ref.at[pl.ds(0, buf_ref.shape[1]), :], buf_ref.at[slot], sem.at[slot]).wait()

    pos = _tile_positions(route_ref[...], off_ref[:, 0:1])
    sub = lax.broadcasted_iota(jnp.int32, (SUBLANES, T_TOK), 0)
    gates = gates_ref[...]
    packed = jnp.zeros((SUBLANES, T_TOK), F32)
    for k in range(TOP_K):
        packed = jnp.where(sub == k, pos[k].astype(F32), packed)
        packed = jnp.where(sub == TOP_K + k, gates[k:k + 1, :], packed)
    cols = packed.T
    acc = x1_ref[...]
    for c in range(TOP_K * T_TOK // P_ROWS):
        row = (c * P_ROWS + lax.broadcasted_iota(jnp.int32, (T_TOK, P_ROWS), 1)).astype(F32)
        g = jnp.zeros((T_TOK, P_ROWS), F32)
        for k in range(TOP_K):
            g = g + jnp.where(row == cols[:, k:k + 1], cols[:, TOP_K + k:TOP_K + k + 1], 0.0)
        y = _from_tile_rows(buf_ref.at[slot, pl.ds(c * P_ROWS * ROW_TILES, P_ROWS * ROW_TILES), :], P_ROWS)
        acc = acc + jnp.dot(g.astype(BF16), y.astype(BF16), preferred_element_type=F32)
    o_ref[...] = acc


def _combine(x1, route, gates, off_lanes, run_dst, run_len, run_off, ys):
    n = x1.shape[0]
    grid_spec = pltpu.PrefetchScalarGridSpec(
        num_scalar_prefetch=3,
        grid=(n // T_TOK,),
        in_specs=[
            pl.BlockSpec((T_TOK, D_MODEL), lambda i, *_: (i, 0)),
            pl.BlockSpec((SUBLANES, T_TOK), lambda i, *_: (0, i)),
            pl.BlockSpec((SUBLANES, T_TOK), lambda i, *_: (0, i)),
            pl.BlockSpec((N_EXPERTS, LANES), lambda i, *_: (i, 0)),
            pl.BlockSpec(memory_space=pl.ANY),
        ],
        out_specs=pl.BlockSpec((T_TOK, D_MODEL), lambda i, *_: (i, 0)),
        scratch_shapes=[pltpu.VMEM((2, TOP_K * T_TOK * ROW_TILES, LANES), F32), pltpu.SemaphoreType.DMA((2,))],
    )
    return pl.pallas_call(
        _combine_kernel,
        grid_spec=grid_spec,
        out_shape=jax.ShapeDtypeStruct((n, D_MODEL), F32),
        compiler_params=_cparams(("arbitrary",)),
        name="combine",
    )(run_dst, run_len, run_off, x1, route, gates, off_lanes, ys)


def _block_diag(w):
    n, r, _ = w.shape
    eye = jnp.eye(n, dtype=w.dtype)
    return (eye[:, None, :, None] * w[:, :, None, :]).reshape(n * r, n * r)


def kernel(x, norm1_g, w_in, conv_w, conv_b, lru_wa, lru_ba, lru_wx, lru_bx, lru_lambda, q_norm_g, k_norm_g,
           lambda_q1, lambda_k1, lambda_q2, lambda_k2, subln_g, w_out, norm2_g, router_w, router_b, w1, b1, w2, b2):
    bsz, seq, d = x.shape
    n_tok = bsz * seq
    assert d == D_MODEL and n_tok % TM_PROJ == 0 and seq % T_SCAN == 0 and seq % TQ == 0 and n_tok % T_TOK == 0
    assert (n_tok * TOP_K) % BM == 0
    assert norm1_g.shape[0] == 1, "single-layer stack"
    x2 = x.reshape(n_tok, d)

    z = _in_proj(x2, norm1_g[0][None, :], w_in[0].astype(BF16))
    z3 = z.reshape(bsz, seq, D_IN)

    n_slabs = D_RNN // LANES
    per_slab = LANES // RNN_BLOCK
    wa = lru_wa[0].reshape(n_slabs, per_slab, RNN_BLOCK, RNN_BLOCK)
    wx = lru_wx[0].reshape(n_slabs, per_slab, RNN_BLOCK, RNN_BLOCK)
    wg = jnp.concatenate([jax.vmap(_block_diag)(wa), jax.vmap(_block_diag)(wx)], axis=2).astype(BF16)
    bg = jnp.concatenate([lru_ba[0].reshape(n_slabs, 1, LANES), lru_bx[0].reshape(n_slabs, 1, LANES)], axis=2)
    y_rnn = _rnn(z3, conv_w[0], conv_b[0][None, :], wg, bg, lru_lambda[0].reshape(n_slabs, 1, LANES))

    half = jnp.arange(LANES) // HEAD_DIM
    ones_bd = (half[:, None] == half[None, :]).astype(BF16)
    y_attn = _attn(z3, jnp.tile(q_norm_g[0], 2)[None, :], jnp.tile(k_norm_g[0], 2)[None, :],
                   lambda_q1[0][None, :], lambda_k1[0][None, :], lambda_q2[0][None, :], lambda_k2[0][None, :],
                   subln_g[0][None, :], ones_bd)

    rw = jnp.pad(router_w[0], ((0, 0), (0, LANES - N_EXPERTS)))
    rw_hi = rw.astype(BF16)
    rw_lo = (rw - rw_hi.astype(F32)).astype(BF16)
    rw_parts = jnp.concatenate([rw_hi, rw_lo], axis=1)
    rb = jnp.pad(router_b[0], (0, LANES - N_EXPERTS))[None, :]
    tok = jnp.arange(T_TOK)
    tri = (tok[:, None] < tok[None, :]).astype(BF16)
    x1, h2, route, gates, counts = _out_proj(
        y_rnn.reshape(n_tok, D_RNN), y_attn.reshape(n_tok, D_ATTN), x2,
        w_out[0].astype(BF16).reshape(2, D_RNN, D_MODEL), norm2_g[0][None, :], rw_parts, rb, tri)

    n_blocks = (n_tok * TOP_K) // BM + N_EXPERTS
    tile_counts = counts[:, 0].reshape(n_tok // T_TOK, N_EXPERTS)
    run_dst, run_len, run_off, off_lanes, blk_e, n_valid, pad_end, padded = _plan(tile_counts, n_blocks)
    xs = _dispatch(h2, route, off_lanes, run_dst, run_len, run_off, pad_end, padded, n_valid, n_blocks * BM)
    ys = _experts(xs, blk_e, n_valid, w1[0], b1[0][:, None, :], w2[0], b2[0][:, None, :])
    out = _combine(x1, route, gates, off_lanes, run_dst, run_len, run_off, ys)
    return out.reshape(bsz, seq, d)
```

```python
import functools
import math

import jax
import jax.numpy as jnp
from jax import lax
from jax.experimental import pallas as pl
from jax.experimental.pallas import tpu as pltpu

F32 = jnp.float32
BF16 = jnp.bfloat16

D_MODEL = 1024
D_RNN = 512
RNN_BLOCK = 64
CONV_WIDTH = 4
LRU_C = 8.0
HEAD_DIM = 64
N_HEADS = 4
D_ATTN = 512
D_IN = 2 * D_RNN + 3 * D_ATTN
N_EXPERTS = 32
TOP_K = 4
D_FF = 1024
SWIGLU_LIMIT = 7.0
SWIGLU_ALPHA = 1.702
EPS = 1e-5
LAM_INIT = 0.8 - 0.6 * math.exp(0.0)
LOG2_E = math.log2(math.e)

LANES = 128
SUBLANES = 8
ROW_TILES = D_MODEL // LANES
VMEM_LIMIT = 52 * 1024 * 1024

TM_PROJ = 1024
T_SCAN = 512
TQ = 128
BM = 512
T_TOK = 512
P_ROWS = 256
RUN_BITS = T_TOK.bit_length()


def _cparams(sem):
    return pltpu.CompilerParams(dimension_semantics=sem, vmem_limit_bytes=VMEM_LIMIT)


def _to_tile_rows(ref, x):
    rows = x.shape[0]
    for s in range(ROW_TILES):
        ref[pl.ds(s, rows, stride=ROW_TILES), :] = x[:, s * LANES:(s + 1) * LANES]


def _from_tile_rows(ref, rows):
    return jnp.concatenate([ref[pl.ds(s, rows, stride=ROW_TILES), :] for s in range(ROW_TILES)], axis=1)


def _in_proj_kernel(x_ref, g_ref, w_ref, z_ref):
    x = x_ref[...]
    ms = jnp.mean(x * x, axis=-1, keepdims=True)
    h = x * lax.rsqrt(ms + EPS) * g_ref[...]
    z_ref[...] = jnp.dot(h.astype(BF16), w_ref[...], preferred_element_type=F32)


def _in_proj(x2, g, w_bf):
    n = x2.shape[0]
    return pl.pallas_call(
        _in_proj_kernel,
        grid=(n // TM_PROJ,),
        in_specs=[
            pl.BlockSpec((TM_PROJ, D_MODEL), lambda i: (i, 0)),
            pl.BlockSpec((1, D_MODEL), lambda i: (0, 0)),
            pl.BlockSpec((D_MODEL, D_IN), lambda i: (0, 0)),
        ],
        out_specs=pl.BlockSpec((TM_PROJ, D_IN), lambda i: (i, 0)),
        out_shape=jax.ShapeDtypeStruct((n, D_IN), F32),
        compiler_params=_cparams(("arbitrary",)),
        name="in_proj",
    )(x2, g, w_bf)


def _rnn_kernel(xr_ref, gr_ref, cw_ref, cb_ref, wg_ref, bg_ref, lam_ref, y_ref):
    seq = xr_ref.shape[1]
    n_chunks = seq // T_SCAN
    n_groups = T_SCAN // SUBLANES
    cw = cw_ref[...]
    cb = cb_ref[...]
    nl = -lam_ref[0]
    softplus_neg_lam = jnp.maximum(nl, 0.0) + jnp.log(1.0 + jnp.exp(-jnp.abs(nl)))
    group = lax.broadcasted_iota(jnp.int32, (n_groups, LANES), 0)

    def previous_group(v, first):
        return jnp.where(group >= 1, pltpu.roll(v, 1, axis=0), first)

    def phase_rows(t0, r):
        return pl.ds(t0 + r, n_groups, stride=SUBLANES)

    def chunk(c, carry):
        h_prev, x_tail = carry[0], carry[1:]
        t0 = pl.multiple_of(c * T_SCAN, T_SCAN)
        x = [xr_ref[0, phase_rows(t0, r), :] for r in range(SUBLANES)]
        wrapped = [previous_group(x[SUBLANES - j], x_tail[CONV_WIDTH - 1 - j]) for j in range(1, CONV_WIDTH)]

        def delayed(r, j):
            return x[r - j] if r >= j else wrapped[j - r - 1]

        conv = []
        for r in range(SUBLANES):
            acc = cb + cw[CONV_WIDTH - 1:CONV_WIDTH, :] * x[r]
            for j in range(1, CONV_WIDTH):
                acc = acc + cw[CONV_WIDTH - 1 - j:CONV_WIDTH - j, :] * delayed(r, j)
            conv.append(acc)
        conv = jnp.concatenate(conv, axis=0)
        gates = jnp.dot(conv.astype(BF16), wg_ref[0], preferred_element_type=F32) + bg_ref[0]
        rg = jax.nn.sigmoid(gates[:, :LANES])
        ig = jax.nn.sigmoid(gates[:, LANES:])
        a = jnp.exp(-LRU_C * rg * softplus_neg_lam)
        var = 1.0 - a * a
        u = jnp.where(var > 0.0, var * lax.rsqrt(var), 0.0) * (ig * conv)
        ph = lambda v, r: v[r * n_groups:(r + 1) * n_groups, :]
        a_in, u_in = [ph(a, 0)], [ph(u, 0)]
        for r in range(1, SUBLANES):
            a_in.append(ph(a, r) * a_in[-1])
            u_in.append(ph(a, r) * u_in[-1] + ph(u, r))
        ga, gu = a_in[-1], u_in[-1]
        d = 1
        while d < n_groups:
            keep = group >= d
            gu = jnp.where(keep, ga * pltpu.roll(gu, d, axis=0) + gu, gu)
            ga = jnp.where(keep, ga * pltpu.roll(ga, d, axis=0), ga)
            d *= 2
        h_after = ga * h_prev + gu
        h_before = previous_group(h_after, h_prev)
        for r in range(SUBLANES):
            h = a_in[r] * h_before + u_in[r]
            rows = phase_rows(t0, r)
            y_ref[0, rows, :] = h * jax.nn.gelu(gr_ref[0, rows, :], approximate=True)
        last = slice(n_groups - 1, n_groups)
        return (h_after[last, :],) + tuple(x[SUBLANES - CONV_WIDTH + 1 + j][last, :] for j in range(CONV_WIDTH - 1))

    zero = jnp.zeros((1, LANES), F32)
    lax.fori_loop(0, n_chunks, chunk, (zero,) * CONV_WIDTH)


def _rnn(z3, conv_w, conv_b, wg_slabs, bg_slabs, lam_slabs):
    bsz, seq, _ = z3.shape
    n_slabs = D_RNN // LANES
    slab = lambda rows: pl.BlockSpec((rows, LANES), lambda b, c: (0, c))
    slab3 = lambda shape: pl.BlockSpec((1,) + shape, lambda b, c: (c, 0, 0))
    return pl.pallas_call(
        _rnn_kernel,
        grid=(bsz, n_slabs),
        in_specs=[
            pl.BlockSpec((1, seq, LANES), lambda b, c: (b, 0, c)),
            pl.BlockSpec((1, seq, LANES), lambda b, c: (b, 0, n_slabs + c)),
            slab(CONV_WIDTH),
            slab(1),
            slab3((LANES, 2 * LANES)),
            slab3((1, 2 * LANES)),
            slab3((1, LANES)),
        ],
        out_specs=pl.BlockSpec((1, seq, LANES), lambda b, c: (b, 0, c)),
        out_shape=jax.ShapeDtypeStruct((bsz, seq, D_RNN), F32),
        compiler_params=_cparams(("arbitrary", "arbitrary")),
        name="rnn",
    )(z3, z3, conv_w, conv_b, wg_slabs, bg_slabs, lam_slabs)


def _group_rms(x, ones_bd):
    x2 = x * x
    hi = x2.astype(BF16)
    lo = (x2 - hi.astype(F32)).astype(BF16)
    ssq = jnp.dot(hi, ones_bd, preferred_element_type=F32) + jnp.dot(lo, ones_bd, preferred_element_type=F32)
    return x * lax.rsqrt(ssq * (1.0 / HEAD_DIM) + EPS)


def _attn_kernel(q_ref, k_ref, v_ref, qg_ref, kg_ref, lq1_ref, lk1_ref, lq2_ref, lk2_ref, sg_ref, ones_ref, o_ref):
    seq = q_ref.shape[1]
    ones_bd = ones_ref[...]
    lam = (jnp.exp(jnp.sum(lq1_ref[...] * lk1_ref[...], axis=-1, keepdims=True))
           - jnp.exp(jnp.sum(lq2_ref[...] * lk2_ref[...], axis=-1, keepdims=True)) + LAM_INIT)
    qn = _group_rms(q_ref[0], ones_bd) * qg_ref[...] * (HEAD_DIM ** -0.5 * LOG2_E)
    kn = _group_rms(k_ref[0], ones_bd) * kg_ref[...]
    lane = lax.broadcasted_iota(jnp.int32, (seq, LANES), 1)
    q1 = jnp.where(lane < HEAD_DIM, qn, 0.0).astype(BF16)
    q2 = jnp.where(lane >= HEAD_DIM, qn, 0.0).astype(BF16)
    kb = kn.astype(BF16)
    vb = v_ref[0].astype(BF16)
    sg = sg_ref[...]
    dn = (((1,), (1,)), ((), ()))
    n_blk = seq // TQ

    def scores(qi, qm):
        kv = (qi + 1) * TQ
        return lax.dot_general(qm[qi * TQ:kv], kb[:kv], dn, preferred_element_type=F32)

    causal = (lax.broadcasted_iota(jnp.int32, (TQ, TQ), 1) <= lax.broadcasted_iota(jnp.int32, (TQ, TQ), 0))

    def weights(qi, s1, s2):
        c0 = qi * TQ

        def probs(s):
            diag = jnp.where(causal, s[:, c0:], -jnp.inf)
            m = jnp.max(diag, axis=-1, keepdims=True)
            if qi > 0:
                m = jnp.maximum(m, jnp.max(s[:, :c0], axis=-1, keepdims=True))
            e = jnp.exp2(diag - m)
            l = jnp.sum(e, axis=-1, keepdims=True)
            if qi > 0:
                e_prev = jnp.exp2(s[:, :c0] - m)
                l = l + jnp.sum(e_prev, axis=-1, keepdims=True)
                e = jnp.concatenate([e_prev, e], axis=1)
            return e, l

        e1, l1 = probs(s1)
        e2, l2 = probs(s2)
        return (e1 - e2 * (lam * l1 / l2)).astype(BF16), 1.0 / l1

    def values(qi, w_and_scale):
        w, scale = w_and_scale
        kv = (qi + 1) * TQ
        o = jnp.dot(w, vb[:kv], preferred_element_type=F32) * scale
        o = o * lax.rsqrt(jnp.mean(o * o, axis=-1, keepdims=True) + EPS) * sg * (1.0 - LAM_INIT)
        o_ref[0, qi * TQ:kv, :] = o

    s = {0: (scores(0, q1), scores(0, q2))}
    w = {}
    for t in range(n_blk + 1):
        if t + 1 < n_blk:
            s[t + 1] = (scores(t + 1, q1), scores(t + 1, q2))
        if t < n_blk:
            w[t] = weights(t, *s.pop(t))
        if t >= 1:
            values(t - 1, w.pop(t - 1))


def _attn(z3, qg2, kg2, lq1, lk1, lq2, lk2, sg, ones_bd):
    bsz, seq, _ = z3.shape
    qoff = 2 * D_RNN // LANES
    koff = qoff + D_ATTN // LANES
    voff = koff + D_ATTN // LANES
    const = lambda shape: pl.BlockSpec(shape, lambda b, h: (0,) * len(shape))
    return pl.pallas_call(
        _attn_kernel,
        grid=(bsz, N_HEADS),
        in_specs=[
            pl.BlockSpec((1, seq, LANES), lambda b, h: (b, 0, qoff + h)),
            pl.BlockSpec((1, seq, LANES), lambda b, h: (b, 0, koff + h)),
            pl.BlockSpec((1, seq, LANES), lambda b, h: (b, 0, voff + h)),
            const((1, LANES)), const((1, LANES)),
            const((1, HEAD_DIM)), const((1, HEAD_DIM)), const((1, HEAD_DIM)), const((1, HEAD_DIM)),
            const((1, LANES)), const((LANES, LANES)),
        ],
        out_specs=pl.BlockSpec((1, seq, LANES), lambda b, h: (b, 0, h)),
        out_shape=jax.ShapeDtypeStruct((bsz, seq, D_ATTN), F32),
        compiler_params=_cparams(("arbitrary", "arbitrary")),
        name="attn",
    )(z3, z3, z3, qg2, kg2, lq1, lk1, lq2, lk2, sg, ones_bd)


def _out_proj_kernel(yr_ref, ya_ref, x_ref, wo_ref, g_ref, rw_ref, rb_ref, tri_ref,
                     x1_ref, h2_ref, route_ref, gates_ref, counts_ref):
    acc = jnp.dot(yr_ref[...].astype(BF16), wo_ref[0], preferred_element_type=F32)
    acc = acc + jnp.dot(ya_ref[...].astype(BF16), wo_ref[1], preferred_element_type=F32)
    x1 = x_ref[...] + acc
    x1_ref[...] = x1
    h2 = x1 * lax.rsqrt(jnp.mean(x1 * x1, axis=-1, keepdims=True) + EPS) * g_ref[...]
    hh = h2.astype(BF16)
    h2_ref[...] = hh
    hl = (h2 - hh.astype(F32)).astype(BF16)
    ph = jnp.dot(hh, rw_ref[...], preferred_element_type=F32)
    pl_ = jnp.dot(hl, rw_ref[...], preferred_element_type=F32)
    logits = (ph[:, :LANES] + ph[:, LANES:]) + (pl_[:, :LANES] + pl_[:, LANES:]) + rb_ref[...]
    tm = logits.shape[0]
    l = logits.T[:N_EXPERTS, :]
    eid = lax.broadcasted_iota(jnp.int32, (N_EXPERTS, tm), 0)
    vals, idxs = [], []
    for _ in range(TOP_K):
        m = jnp.max(l, axis=0, keepdims=True)
        idx = jnp.min(jnp.where(l == m, eid, N_EXPERTS), axis=0, keepdims=True)
        vals.append(m)
        idxs.append(idx)
        l = jnp.where(eid == idx, -jnp.inf, l)
    es = [jnp.exp(v - vals[0]) for v in vals]
    inv = 1.0 / (es[0] + es[1] + es[2] + es[3])
    chosen = jnp.zeros((N_EXPERTS, tm), F32)
    for k in range(TOP_K):
        chosen = chosen + (eid == idxs[k]).astype(F32)
    before = jnp.dot(chosen.astype(BF16), tri_ref[...], preferred_element_type=F32)
    sub = lax.broadcasted_iota(jnp.int32, (SUBLANES, tm), 0)
    route = jnp.zeros((SUBLANES, tm), jnp.int32)
    gates = jnp.zeros((SUBLANES, tm), F32)
    for k in range(TOP_K):
        rank = jnp.sum(jnp.where(eid == idxs[k], before, 0.0), axis=0, keepdims=True).astype(jnp.int32)
        route = jnp.where(sub == k, idxs[k], route)
        route = jnp.where(sub == TOP_K + k, rank, route)
        gates = jnp.where(sub == k, es[k] * inv, gates)
    route_ref[...] = route
    gates_ref[...] = gates
    counts_ref[...] = jnp.broadcast_to(jnp.sum(chosen, axis=1, keepdims=True), (N_EXPERTS, LANES)).astype(jnp.int32)


def _out_proj(y_rnn, y_attn, x2, wo_bf, g2, rw_parts, rb, tri):
    n = x2.shape[0]
    n_tiles = n // T_TOK
    row = lambda w: pl.BlockSpec((T_TOK, w), lambda i: (i, 0))
    col = pl.BlockSpec((SUBLANES, T_TOK), lambda i: (0, i))
    const = lambda shape: pl.BlockSpec(shape, lambda i: (0,) * len(shape))
    return pl.pallas_call(
        _out_proj_kernel,
        grid=(n_tiles,),
        in_specs=[row(D_RNN), row(D_ATTN), row(D_MODEL),
                  const((2, D_RNN, D_MODEL)), const((1, D_MODEL)),
                  const((D_MODEL, 2 * LANES)), const((1, LANES)),
                  const((T_TOK, T_TOK))],
        out_specs=[row(D_MODEL), row(D_MODEL), col, col, pl.BlockSpec((N_EXPERTS, LANES), lambda i: (i, 0))],
        out_shape=[jax.ShapeDtypeStruct((n, D_MODEL), F32),
                   jax.ShapeDtypeStruct((n, D_MODEL), BF16),
                   jax.ShapeDtypeStruct((SUBLANES, n), jnp.int32),
                   jax.ShapeDtypeStruct((SUBLANES, n), F32),
                   jax.ShapeDtypeStruct((n_tiles * N_EXPERTS, LANES), jnp.int32)],
        compiler_params=_cparams(("arbitrary",)),
        name="out_proj",
    )(y_rnn, y_attn, x2, wo_bf, g2, rw_parts, rb, tri)


def _plan(tile_counts, n_blocks):
    n_tiles = tile_counts.shape[0]
    counts = jnp.sum(tile_counts, axis=0)
    padded = (counts + BM - 1) // BM * BM
    pad_end = jnp.cumsum(padded).astype(jnp.int32)
    pad_start = pad_end - padded
    earlier_tiles = jnp.cumsum(tile_counts, axis=0) - tile_counts
    run_off = (jnp.cumsum(tile_counts, axis=1) - tile_counts).astype(jnp.int32)
    run_dst = (pad_start[None, :] + earlier_tiles).astype(jnp.int32)
    off_lanes = jnp.broadcast_to(run_off.reshape(n_tiles * N_EXPERTS, 1), (n_tiles * N_EXPERTS, LANES))
    blk_start = jnp.arange(n_blocks, dtype=jnp.int32) * BM
    blk_e = jnp.minimum(jnp.sum((pad_end[None, :] <= blk_start[:, None]).astype(jnp.int32), axis=1), N_EXPERTS - 1)
    n_valid = (pad_end[-1] // BM).reshape(1)
    return (run_dst.reshape(-1), tile_counts.reshape(-1).astype(jnp.int32), run_off.reshape(-1), off_lanes,
            blk_e, n_valid, pad_end, padded)


def _tile_positions(route, off_col):
    eid = lax.broadcasted_iota(jnp.int32, (N_EXPERTS, route.shape[1]), 0)
    pos = []
    for k in range(TOP_K):
        start = jnp.sum(jnp.where(eid == route[k:k + 1, :], off_col, 0), axis=0, keepdims=True)
        pos.append(start + route[TOP_K + k:TOP_K + k + 1, :])
    return pos


def _run_copies(run_len_ref, tile, run_bases, make_copy, act):
    def per_expert(e, carry):
        run = tile * N_EXPERTS + e
        length = run_len_ref[run]
        bases = run_bases(run)
        for b in range(RUN_BITS - 1, -1, -1):
            piece_start = length & ~((2 << b) - 1)

            @pl.when((length & (1 << b)) != 0)
            def _():
                act(make_copy(bases, piece_start, 1 << b))
        return carry

    lax.fori_loop(0, N_EXPERTS, per_expert, 0)


def _dispatch_kernel(run_dst_ref, run_len_ref, run_off_ref, pad_end_ref, padded_ref, n_valid_ref,
                     h_ref, route_ref, off_ref, xs_ref, sorted_ref, zeros_ref, sem, zsem):
    i = pl.program_id(0)
    n_tiles = pl.num_programs(0)
    slot = i % 2
    blk_rows = BM * ROW_TILES
    n_blocks = xs_ref.shape[0] // blk_rows

    def run_bases(run):
        return run_off_ref[run], run_dst_ref[run]

    def run_copy(s, bases, piece_start, size):
        src_row = pl.multiple_of((bases[0] + piece_start) * ROW_TILES, ROW_TILES)
        dst_row = pl.multiple_of((bases[1] + piece_start) * ROW_TILES, ROW_TILES)
        return pltpu.make_async_copy(sorted_ref.at[s, pl.ds(src_row, size * ROW_TILES), :],
                                     xs_ref.at[pl.ds(dst_row, size * ROW_TILES), :], sem.at[s])

    def start_runs(tile, s):
        _run_copies(run_len_ref, tile, run_bases, functools.partial(run_copy, s), lambda cp: cp.start())

    def wait_runs(s):
        pltpu.make_async_copy(sorted_ref.at[s], xs_ref.at[pl.ds(0, sorted_ref.shape[1]), :], sem.at[s]).wait()

    @pl.when(i == 0)
    def _():
        zeros_ref[...] = jnp.zeros_like(zeros_ref)

        def zero_copy(blk):
            start = pl.multiple_of(blk * blk_rows, blk_rows)
            return pltpu.make_async_copy(zeros_ref, xs_ref.at[pl.ds(start, blk_rows), :], zsem)

        def zero_blocks(act):
            def tail(e, carry):
                @pl.when(padded_ref[e] > 0)
                def _():
                    act(zero_copy(pad_end_ref[e] // BM - 1))
                return carry

            def dead(b, carry):
                act(zero_copy(b))
                return carry

            lax.fori_loop(0, N_EXPERTS, tail, 0)
            lax.fori_loop(n_valid_ref[0], n_blocks, dead, 0)

        zero_blocks(lambda cp: cp.start())
        zero_blocks(lambda cp: cp.wait())

    @pl.when(i >= 2)
    def _():
        wait_runs(slot)

    pos = _tile_positions(route_ref[...], off_ref[:, 0:1])
    hb = h_ref[...]

    for c in range(TOP_K * T_TOK // P_ROWS):
        row = c * P_ROWS + lax.broadcasted_iota(jnp.int32, (P_ROWS, T_TOK), 0)
        hit = row == pos[0]
        for k in range(1, TOP_K):
            hit = hit | (row == pos[k])
        perm = jnp.where(hit, 1.0, 0.0).astype(BF16)
        rows = jnp.dot(perm, hb, preferred_element_type=F32)
        _to_tile_rows(sorted_ref.at[slot, pl.ds(c * P_ROWS * ROW_TILES, P_ROWS * ROW_TILES), :], rows)
    start_runs(i, slot)

    @pl.when(i == n_tiles - 1)
    def _():
        @pl.when(i >= 1)
        def _():
            wait_runs(1 - slot)
        wait_runs(slot)


def _dispatch(h2, route, off_lanes, run_dst, run_len, run_off, pad_end, padded, n_valid, n_rows):
    n = h2.shape[0]
    grid_spec = pltpu.PrefetchScalarGridSpec(
        num_scalar_prefetch=6,
        grid=(n // T_TOK,),
        in_specs=[
            pl.BlockSpec((T_TOK, D_MODEL), lambda i, *_: (i, 0)),
            pl.BlockSpec((SUBLANES, T_TOK), lambda i, *_: (0, i)),
            pl.BlockSpec((N_EXPERTS, LANES), lambda i, *_: (i, 0)),
        ],
        out_specs=pl.BlockSpec(memory_space=pl.ANY),
        scratch_shapes=[pltpu.VMEM((2, TOP_K * T_TOK * ROW_TILES, LANES), F32),
                        pltpu.VMEM((BM * ROW_TILES, LANES), F32),
                        pltpu.SemaphoreType.DMA((2,)), pltpu.SemaphoreType.DMA(())],
    )
    return pl.pallas_call(
        _dispatch_kernel,
        grid_spec=grid_spec,
        out_shape=jax.ShapeDtypeStruct((n_rows * ROW_TILES, LANES), F32),
        compiler_params=_cparams(("arbitrary",)),
        name="dispatch",
    )(run_dst, run_len, run_off, pad_end, padded, n_valid, h2, route, off_lanes)


def _experts_kernel(blk_e_ref, n_valid_ref, xs_ref, w1_ref, b1_ref, w2_ref, b2_ref, y_ref, w1b_ref, w2b_ref):
    i = pl.program_id(0)

    @pl.when(i < n_valid_ref[0])
    def _():
        prev_e = blk_e_ref[jnp.maximum(i - 1, 0)]

        @pl.when((i == 0) | (blk_e_ref[i] != prev_e))
        def _():
            w1b_ref[...] = w1_ref[0].astype(BF16)
            w2b_ref[...] = w2_ref[0].astype(BF16)

        x = _from_tile_rows(xs_ref, BM).astype(BF16)
        hcat = jnp.dot(x, w1b_ref[...], preferred_element_type=F32) + b1_ref[0]
        gate = jnp.minimum(hcat[:, :D_FF], SWIGLU_LIMIT)
        up = jnp.clip(hcat[:, D_FF:], -SWIGLU_LIMIT, SWIGLU_LIMIT)
        act = gate * jax.nn.sigmoid(SWIGLU_ALPHA * gate) * (up + 1.0)
        y = jnp.dot(act.astype(BF16), w2b_ref[...], preferred_element_type=F32) + b2_ref[0]
        _to_tile_rows(y_ref, y)

    @pl.when(i >= n_valid_ref[0])
    def _():
        y_ref[...] = jnp.zeros_like(y_ref)


def _experts(xs, blk_e, n_valid, w1, b1, w2, b2):
    blk_rows = BM * ROW_TILES
    n_blocks = xs.shape[0] // blk_rows
    exp3 = lambda i, be, nv: (be[i], 0, 0)
    grid_spec = pltpu.PrefetchScalarGridSpec(
        num_scalar_prefetch=2,
        grid=(n_blocks,),
        in_specs=[
            pl.BlockSpec((blk_rows, LANES), lambda i, be, nv: (jnp.minimum(i, nv[0] - 1), 0)),
            pl.BlockSpec((1, D_MODEL, 2 * D_FF), exp3),
            pl.BlockSpec((1, 1, 2 * D_FF), exp3),
            pl.BlockSpec((1, D_FF, D_MODEL), exp3),
            pl.BlockSpec((1, 1, D_MODEL), exp3),
        ],
        out_specs=pl.BlockSpec((blk_rows, LANES), lambda i, be, nv: (i, 0)),
        scratch_shapes=[pltpu.VMEM((D_MODEL, 2 * D_FF), BF16), pltpu.VMEM((D_FF, D_MODEL), BF16)],
    )
    return pl.pallas_call(
        _experts_kernel,
        grid_spec=grid_spec,
        out_shape=jax.ShapeDtypeStruct(xs.shape, F32),
        compiler_params=_cparams(("arbitrary",)),
        name="experts",
    )(blk_e, n_valid, xs, w1, b1, w2, b2)


def _combine_kernel(run_src_ref, run_len_ref, run_off_ref, x1_ref, route_ref, gates_ref, off_ref, ys_ref, o_ref,
                    buf_ref, sem):
    i = pl.program_id(0)
    n_tiles = pl.num_programs(0)
    slot = i % 2

    def run_bases(run):
        return run_src_ref[run], run_off_ref[run]

    def run_copy(s, bases, piece_start, size):
        src_row = pl.multiple_of((bases[0] + piece_start) * ROW_TILES, ROW_TILES)
        dst_row = pl.multiple_of((bases[1] + piece_start) * ROW_TILES, ROW_TILES)
        return pltpu.make_async_copy(ys_ref.at[pl.ds(src_row, size * ROW_TILES), :],
                                     buf_ref.at[s, pl.ds(dst_row, size * ROW_TILES), :], sem.at[s])

    def start_runs(tile, s):
        _run_copies(run_len_ref, tile, run_bases, functools.partial(run_copy, s), lambda cp: cp.start())

    @pl.when(i == 0)
    def _():
        start_runs(0, 0)

    @pl.when(i + 1 < n_tiles)
    def _():
        start_runs(i + 1, 1 - slot)

    pltpu.make_async_copy(ys_ref.at[pl.ds(0, buf_ref.shape[1]), :], buf_ref.at[slot], sem.at[slot]).wait()

    pos = _tile_positions(route_ref[...], off_ref[:, 0:1])
    sub = lax.broadcasted_iota(jnp.int32, (SUBLANES, T_TOK), 0)
    gates = gates_ref[...]
    packed = jnp.zeros((SUBLANES, T_TOK), F32)
    for k in range(TOP_K):
        packed = jnp.where(sub == k, pos[k].astype(F32), packed)
        packed = jnp.where(sub == TOP_K + k, gates[k:k + 1, :], packed)
    cols = packed.T
    acc = x1_ref[...]
    for c in range(TOP_K * T_TOK // P_ROWS):
        row = (c * P_ROWS + lax.broadcasted_iota(jnp.int32, (T_TOK, P_ROWS), 1)).astype(F32)
        g = jnp.zeros((T_TOK, P_ROWS), F32)
        for k in range(TOP_K):
            g = jnp.where(row == cols[:, k:k + 1], cols[:, TOP_K + k:TOP_K + k + 1], g)
        y = _from_tile_rows(buf_ref.at[slot, pl.ds(c * P_ROWS * ROW_TILES, P_ROWS * ROW_TILES), :], P_ROWS)
        acc = acc + jnp.dot(g.astype(BF16), y.astype(BF16), preferred_element_type=F32)
    o_ref[...] = acc


def _combine(x1, route, gates, off_lanes, run_dst, run_len, run_off, ys):
    n = x1.shape[0]
    grid_spec = pltpu.PrefetchScalarGridSpec(
        num_scalar_prefetch=3,
        grid=(n // T_TOK,),
        in_specs=[
            pl.BlockSpec((T_TOK, D_MODEL), lambda i, *_: (i, 0)),
            pl.BlockSpec((SUBLANES, T_TOK), lambda i, *_: (0, i)),
            pl.BlockSpec((SUBLANES, T_TOK), lambda i, *_: (0, i)),
            pl.BlockSpec((N_EXPERTS, LANES), lambda i, *_: (i, 0)),
            pl.BlockSpec(memory_space=pl.ANY),
        ],
        out_specs=pl.BlockSpec((T_TOK, D_MODEL), lambda i, *_: (i, 0)),
        scratch_shapes=[pltpu.VMEM((2, TOP_K * T_TOK * ROW_TILES, LANES), F32), pltpu.SemaphoreType.DMA((2,))],
    )
    return pl.pallas_call(
        _combine_kernel,
        grid_spec=grid_spec,
        out_shape=jax.ShapeDtypeStruct((n, D_MODEL), F32),
        compiler_params=_cparams(("arbitrary",)),
        name="combine",
    )(run_dst, run_len, run_off, x1, route, gates, off_lanes, ys)


def _block_diag(w):
    n, r, _ = w.shape
    eye = jnp.eye(n, dtype=w.dtype)
    return (eye[:, None, :, None] * w[:, :, None, :]).reshape(n * r, n * r)


def kernel(x, norm1_g, w_in, conv_w, conv_b, lru_wa, lru_ba, lru_wx, lru_bx, lru_lambda, q_norm_g, k_norm_g,
           lambda_q1, lambda_k1, lambda_q2, lambda_k2, subln_g, w_out, norm2_g, router_w, router_b, w1, b1, w2, b2):
    bsz, seq, d = x.shape
    n_tok = bsz * seq
    assert d == D_MODEL and n_tok % TM_PROJ == 0 and seq % T_SCAN == 0 and seq % TQ == 0 and n_tok % T_TOK == 0
    assert (n_tok * TOP_K) % BM == 0
    assert norm1_g.shape[0] == 1, "single-layer stack"
    x2 = x.reshape(n_tok, d)

    z = _in_proj(x2, norm1_g[0][None, :], w_in[0].astype(BF16))
    z3 = z.reshape(bsz, seq, D_IN)

    n_slabs = D_RNN // LANES
    per_slab = LANES // RNN_BLOCK
    wa = lru_wa[0].reshape(n_slabs, per_slab, RNN_BLOCK, RNN_BLOCK)
    wx = lru_wx[0].reshape(n_slabs, per_slab, RNN_BLOCK, RNN_BLOCK)
    wg = jnp.concatenate([jax.vmap(_block_diag)(wa), jax.vmap(_block_diag)(wx)], axis=2).astype(BF16)
    bg = jnp.concatenate([lru_ba[0].reshape(n_slabs, 1, LANES), lru_bx[0].reshape(n_slabs, 1, LANES)], axis=2)
    y_rnn = _rnn(z3, conv_w[0], conv_b[0][None, :], wg, bg, lru_lambda[0].reshape(n_slabs, 1, LANES))

    half = jnp.arange(LANES) // HEAD_DIM
    ones_bd = (half[:, None] == half[None, :]).astype(BF16)
    y_attn = _attn(z3, jnp.tile(q_norm_g[0], 2)[None, :], jnp.tile(k_norm_g[0], 2)[None, :],
                   lambda_q1[0][None, :], lambda_k1[0][None, :], lambda_q2[0][None, :], lambda_k2[0][None, :],
                   subln_g[0][None, :], ones_bd)

    rw = jnp.pad(router_w[0], ((0, 0), (0, LANES - N_EXPERTS)))
    rw_hi = rw.astype(BF16)
    rw_lo = (rw - rw_hi.astype(F32)).astype(BF16)
    rw_parts = jnp.concatenate([rw_hi, rw_lo], axis=1)
    rb = jnp.pad(router_b[0], (0, LANES - N_EXPERTS))[None, :]
    tok = jnp.arange(T_TOK)
    tri = (tok[:, None] < tok[None, :]).astype(BF16)
    x1, h2, route, gates, counts = _out_proj(
        y_rnn.reshape(n_tok, D_RNN), y_attn.reshape(n_tok, D_ATTN), x2,
        w_out[0].astype(BF16).reshape(2, D_RNN, D_MODEL), norm2_g[0][None, :], rw_parts, rb, tri)

    n_blocks = (n_tok * TOP_K) // BM + N_EXPERTS
    tile_counts = counts[:, 0].reshape(n_tok // T_TOK, N_EXPERTS)
    run_dst, run_len, run_off, off_lanes, blk_e, n_valid, pad_end, padded = _plan(tile_counts, n_blocks)
    xs = _dispatch(h2, route, off_lanes, run_dst, run_len, run_off, pad_end, padded, n_valid, n_blocks * BM)
    ys = _experts(xs, blk_e, n_valid, w1[0], b1[0][:, None, :], w2[0], b2[0][:, None, :])
    out = _combine(x1, route, gates, off_lanes, run_dst, run_len, run_off, ys)
    return out.reshape(bsz, seq, d)
```

```python
import functools
import math

import jax
import jax.numpy as jnp
from jax import lax
from jax.experimental import pallas as pl
from jax.experimental.pallas import tpu as pltpu

F32 = jnp.float32
BF16 = jnp.bfloat16

D_MODEL = 1024
D_RNN = 512
RNN_BLOCK = 64
CONV_WIDTH = 4
LRU_C = 8.0
HEAD_DIM = 64
N_HEADS = 4
D_ATTN = 512
D_IN = 2 * D_RNN + 3 * D_ATTN
N_EXPERTS = 32
TOP_K = 4
D_FF = 1024
SWIGLU_LIMIT = 7.0
SWIGLU_ALPHA = 1.702
EPS = 1e-5
LAM_INIT = 0.8 - 0.6 * math.exp(0.0)
LOG2_E = math.log2(math.e)

LANES = 128
SUBLANES = 8
ROW_TILES = D_MODEL // LANES
VMEM_LIMIT = 52 * 1024 * 1024

TM_PROJ = 1024
T_SCAN = 1024
TQ = 128
BM = 512
T_TOK = 512
P_ROWS = 256
RUN_BITS = T_TOK.bit_length()


def _cparams(sem):
    return pltpu.CompilerParams(dimension_semantics=sem, vmem_limit_bytes=VMEM_LIMIT)


def _to_tile_rows(ref, x):
    rows = x.shape[0]
    for s in range(ROW_TILES):
        ref[pl.ds(s, rows, stride=ROW_TILES), :] = x[:, s * LANES:(s + 1) * LANES]


def _from_tile_rows(ref, rows):
    return jnp.concatenate([ref[pl.ds(s, rows, stride=ROW_TILES), :] for s in range(ROW_TILES)], axis=1)


def _in_proj_kernel(x_ref, g_ref, w_ref, z_ref):
    x = x_ref[...]
    ms = jnp.mean(x * x, axis=-1, keepdims=True)
    h = x * lax.rsqrt(ms + EPS) * g_ref[...]
    z_ref[...] = jnp.dot(h.astype(BF16), w_ref[...], preferred_element_type=F32)


def _in_proj(x2, g, w_bf):
    n = x2.shape[0]
    return pl.pallas_call(
        _in_proj_kernel,
        grid=(n // TM_PROJ,),
        in_specs=[
            pl.BlockSpec((TM_PROJ, D_MODEL), lambda i: (i, 0)),
            pl.BlockSpec((1, D_MODEL), lambda i: (0, 0)),
            pl.BlockSpec((D_MODEL, D_IN), lambda i: (0, 0)),
        ],
        out_specs=pl.BlockSpec((TM_PROJ, D_IN), lambda i: (i, 0)),
        out_shape=jax.ShapeDtypeStruct((n, D_IN), F32),
        compiler_params=_cparams(("arbitrary",)),
        name="in_proj",
    )(x2, g, w_bf)


def _rnn_kernel(xr_ref, gr_ref, cw_ref, cb_ref, wg_ref, bg_ref, lam_ref, y_ref):
    seq = xr_ref.shape[1]
    n_chunks = seq // T_SCAN
    n_groups = T_SCAN // SUBLANES
    cw = cw_ref[...]
    cb = cb_ref[...]
    nl = -lam_ref[0]
    softplus_neg_lam = jnp.maximum(nl, 0.0) + jnp.log(1.0 + jnp.exp(-jnp.abs(nl)))
    group = lax.broadcasted_iota(jnp.int32, (n_groups, LANES), 0)

    def previous_group(v, first):
        return jnp.where(group >= 1, pltpu.roll(v, 1, axis=0), first)

    def phase_rows(t0, r):
        return pl.ds(t0 + r, n_groups, stride=SUBLANES)

    def chunk(c, carry):
        h_prev, x_tail = carry[0], carry[1:]
        t0 = pl.multiple_of(c * T_SCAN, T_SCAN)
        x = [xr_ref[0, phase_rows(t0, r), :] for r in range(SUBLANES)]
        wrapped = [previous_group(x[SUBLANES - j], x_tail[CONV_WIDTH - 1 - j]) for j in range(1, CONV_WIDTH)]

        def delayed(r, j):
            return x[r - j] if r >= j else wrapped[j - r - 1]

        conv = []
        for r in range(SUBLANES):
            acc = cb + cw[CONV_WIDTH - 1:CONV_WIDTH, :] * x[r]
            for j in range(1, CONV_WIDTH):
                acc = acc + cw[CONV_WIDTH - 1 - j:CONV_WIDTH - j, :] * delayed(r, j)
            conv.append(acc)
        conv = jnp.concatenate(conv, axis=0)
        gates = jnp.dot(conv.astype(BF16), wg_ref[0], preferred_element_type=F32) + bg_ref[0]
        rg = jax.nn.sigmoid(gates[:, :LANES])
        ig = jax.nn.sigmoid(gates[:, LANES:])
        a = jnp.exp(-LRU_C * rg * softplus_neg_lam)
        var = 1.0 - a * a
        u = jnp.where(var > 0.0, var * lax.rsqrt(var), 0.0) * (ig * conv)
        ph = lambda v, r: v[r * n_groups:(r + 1) * n_groups, :]
        a_in, u_in = [ph(a, 0)], [ph(u, 0)]
        for r in range(1, SUBLANES):
            a_in.append(ph(a, r) * a_in[-1])
            u_in.append(ph(a, r) * u_in[-1] + ph(u, r))
        ga, gu = a_in[-1], u_in[-1]
        d = 1
        while d < n_groups:
            keep = group >= d
            gu = jnp.where(keep, ga * pltpu.roll(gu, d, axis=0) + gu, gu)
            ga = jnp.where(keep, ga * pltpu.roll(ga, d, axis=0), ga)
            d *= 2
        h_after = ga * h_prev + gu
        h_before = previous_group(h_after, h_prev)
        for r in range(SUBLANES):
            h = a_in[r] * h_before + u_in[r]
            rows = phase_rows(t0, r)
            y_ref[0, rows, :] = h * jax.nn.gelu(gr_ref[0, rows, :], approximate=True)
        last = slice(n_groups - 1, n_groups)
        return (h_after[last, :],) + tuple(x[SUBLANES - CONV_WIDTH + 1 + j][last, :] for j in range(CONV_WIDTH - 1))

    zero = jnp.zeros((1, LANES), F32)
    lax.fori_loop(0, n_chunks, chunk, (zero,) * CONV_WIDTH)


def _rnn(z3, conv_w, conv_b, wg_slabs, bg_slabs, lam_slabs):
    bsz, seq, _ = z3.shape
    n_slabs = D_RNN // LANES
    slab = lambda rows: pl.BlockSpec((rows, LANES), lambda b, c: (0, c))
    slab3 = lambda shape: pl.BlockSpec((1,) + shape, lambda b, c: (c, 0, 0))
    return pl.pallas_call(
        _rnn_kernel,
        grid=(bsz, n_slabs),
        in_specs=[
            pl.BlockSpec((1, seq, LANES), lambda b, c: (b, 0, c)),
            pl.BlockSpec((1, seq, LANES), lambda b, c: (b, 0, n_slabs + c)),
            slab(CONV_WIDTH),
            slab(1),
            slab3((LANES, 2 * LANES)),
            slab3((1, 2 * LANES)),
            slab3((1, LANES)),
        ],
        out_specs=pl.BlockSpec((1, seq, LANES), lambda b, c: (b, 0, c)),
        out_shape=jax.ShapeDtypeStruct((bsz, seq, D_RNN), F32),
        compiler_params=_cparams(("arbitrary", "arbitrary")),
        name="rnn",
    )(z3, z3, conv_w, conv_b, wg_slabs, bg_slabs, lam_slabs)


def _group_rms(x, ones_bd):
    x2 = x * x
    hi = x2.astype(BF16)
    lo = (x2 - hi.astype(F32)).astype(BF16)
    ssq = jnp.dot(hi, ones_bd, preferred_element_type=F32) + jnp.dot(lo, ones_bd, preferred_element_type=F32)
    return x * lax.rsqrt(ssq * (1.0 / HEAD_DIM) + EPS)


def _attn_kernel(q_ref, k_ref, v_ref, qg_ref, kg_ref, lq1_ref, lk1_ref, lq2_ref, lk2_ref, sg_ref, ones_ref, o_ref):
    seq = q_ref.shape[1]
    ones_bd = ones_ref[...]
    lam = (jnp.exp(jnp.sum(lq1_ref[...] * lk1_ref[...], axis=-1, keepdims=True))
           - jnp.exp(jnp.sum(lq2_ref[...] * lk2_ref[...], axis=-1, keepdims=True)) + LAM_INIT)
    qn = _group_rms(q_ref[0], ones_bd) * qg_ref[...] * (HEAD_DIM ** -0.5 * LOG2_E)
    kn = _group_rms(k_ref[0], ones_bd) * kg_ref[...]
    lane = lax.broadcasted_iota(jnp.int32, (seq, LANES), 1)
    q1 = jnp.where(lane < HEAD_DIM, qn, 0.0).astype(BF16)
    q2 = jnp.where(lane >= HEAD_DIM, qn, 0.0).astype(BF16)
    kb = kn.astype(BF16)
    vb = v_ref[0].astype(BF16)
    sg = sg_ref[...]
    dn = (((1,), (1,)), ((), ()))
    n_blk = seq // TQ

    def scores(qi, qm):
        kv = (qi + 1) * TQ
        return lax.dot_general(qm[qi * TQ:kv], kb[:kv], dn, preferred_element_type=F32)

    causal = (lax.broadcasted_iota(jnp.int32, (TQ, TQ), 1) <= lax.broadcasted_iota(jnp.int32, (TQ, TQ), 0))

    def weights(qi, s1, s2):
        c0 = qi * TQ

        def probs(s):
            diag = jnp.where(causal, s[:, c0:], -jnp.inf)
            m = jnp.max(diag, axis=-1, keepdims=True)
            if qi > 0:
                m = jnp.maximum(m, jnp.max(s[:, :c0], axis=-1, keepdims=True))
            e = jnp.exp2(diag - m)
            l = jnp.sum(e, axis=-1, keepdims=True)
            if qi > 0:
                e_prev = jnp.exp2(s[:, :c0] - m)
                l = l + jnp.sum(e_prev, axis=-1, keepdims=True)
                e = jnp.concatenate([e_prev, e], axis=1)
            return e, l

        e1, l1 = probs(s1)
        e2, l2 = probs(s2)
        return (e1 - e2 * (lam * l1 / l2)).astype(BF16), 1.0 / l1

    def values(qi, w_and_scale):
        w, scale = w_and_scale
        kv = (qi + 1) * TQ
        o = jnp.dot(w, vb[:kv], preferred_element_type=F32) * scale
        o = o * lax.rsqrt(jnp.mean(o * o, axis=-1, keepdims=True) + EPS) * sg * (1.0 - LAM_INIT)
        o_ref[0, qi * TQ:kv, :] = o

    s = {0: (scores(0, q1), scores(0, q2))}
    w = {}
    for t in range(n_blk + 1):
        if t + 1 < n_blk:
            s[t + 1] = (scores(t + 1, q1), scores(t + 1, q2))
        if t < n_blk:
            w[t] = weights(t, *s.pop(t))
        if t >= 1:
            values(t - 1, w.pop(t - 1))


def _attn(z3, qg2, kg2, lq1, lk1, lq2, lk2, sg, ones_bd):
    bsz, seq, _ = z3.shape
    qoff = 2 * D_RNN // LANES
    koff = qoff + D_ATTN // LANES
    voff = koff + D_ATTN // LANES
    const = lambda shape: pl.BlockSpec(shape, lambda b, h: (0,) * len(shape))
    return pl.pallas_call(
        _attn_kernel,
        grid=(bsz, N_HEADS),
        in_specs=[
            pl.BlockSpec((1, seq, LANES), lambda b, h: (b, 0, qoff + h)),
            pl.BlockSpec((1, seq, LANES), lambda b, h: (b, 0, koff + h)),
            pl.BlockSpec((1, seq, LANES), lambda b, h: (b, 0, voff + h)),
            const((1, LANES)), const((1, LANES)),
            const((1, HEAD_DIM)), const((1, HEAD_DIM)), const((1, HEAD_DIM)), const((1, HEAD_DIM)),
            const((1, LANES)), const((LANES, LANES)),
        ],
        out_specs=pl.BlockSpec((1, seq, LANES), lambda b, h: (b, 0, h)),
        out_shape=jax.ShapeDtypeStruct((bsz, seq, D_ATTN), F32),
        compiler_params=_cparams(("arbitrary", "arbitrary")),
        name="attn",
    )(z3, z3, z3, qg2, kg2, lq1, lk1, lq2, lk2, sg, ones_bd)


def _out_proj_kernel(yr_ref, ya_ref, x_ref, wo_ref, g_ref, rw_ref, rb_ref, tri_ref,
                     x1_ref, h2_ref, route_ref, gates_ref, counts_ref):
    acc = jnp.dot(yr_ref[...].astype(BF16), wo_ref[0], preferred_element_type=F32)
    acc = acc + jnp.dot(ya_ref[...].astype(BF16), wo_ref[1], preferred_element_type=F32)
    x1 = x_ref[...] + acc
    x1_ref[...] = x1
    h2 = x1 * lax.rsqrt(jnp.mean(x1 * x1, axis=-1, keepdims=True) + EPS) * g_ref[...]
    hh = h2.astype(BF16)
    h2_ref[...] = hh
    hl = (h2 - hh.astype(F32)).astype(BF16)
    ph = jnp.dot(hh, rw_ref[...], preferred_element_type=F32)
    pl_ = jnp.dot(hl, rw_ref[...], preferred_element_type=F32)
    logits = (ph[:, :LANES] + ph[:, LANES:]) + (pl_[:, :LANES] + pl_[:, LANES:]) + rb_ref[...]
    tm = logits.shape[0]
    l = logits.T[:N_EXPERTS, :]
    eid = lax.broadcasted_iota(jnp.int32, (N_EXPERTS, tm), 0)
    vals, idxs = [], []
    for _ in range(TOP_K):
        m = jnp.max(l, axis=0, keepdims=True)
        idx = jnp.min(jnp.where(l == m, eid, N_EXPERTS), axis=0, keepdims=True)
        vals.append(m)
        idxs.append(idx)
        l = jnp.where(eid == idx, -jnp.inf, l)
    es = [jnp.exp(v - vals[0]) for v in vals]
    inv = 1.0 / (es[0] + es[1] + es[2] + es[3])
    chosen = jnp.zeros((N_EXPERTS, tm), F32)
    for k in range(TOP_K):
        chosen = chosen + (eid == idxs[k]).astype(F32)
    before = jnp.dot(chosen.astype(BF16), tri_ref[...], preferred_element_type=F32)
    sub = lax.broadcasted_iota(jnp.int32, (SUBLANES, tm), 0)
    route = jnp.zeros((SUBLANES, tm), jnp.int32)
    gates = jnp.zeros((SUBLANES, tm), F32)
    for k in range(TOP_K):
        rank = jnp.sum(jnp.where(eid == idxs[k], before, 0.0), axis=0, keepdims=True).astype(jnp.int32)
        route = jnp.where(sub == k, idxs[k], route)
        route = jnp.where(sub == TOP_K + k, rank, route)
        gates = jnp.where(sub == k, es[k] * inv, gates)
    route_ref[...] = route
    gates_ref[...] = gates
    counts_ref[...] = jnp.broadcast_to(jnp.sum(chosen, axis=1, keepdims=True), (N_EXPERTS, LANES)).astype(jnp.int32)


def _out_proj(y_rnn, y_attn, x2, wo_bf, g2, rw_parts, rb, tri):
    n = x2.shape[0]
    n_tiles = n // T_TOK
    row = lambda w: pl.BlockSpec((T_TOK, w), lambda i: (i, 0))
    col = pl.BlockSpec((SUBLANES, T_TOK), lambda i: (0, i))
    const = lambda shape: pl.BlockSpec(shape, lambda i: (0,) * len(shape))
    return pl.pallas_call(
        _out_proj_kernel,
        grid=(n_tiles,),
        in_specs=[row(D_RNN), row(D_ATTN), row(D_MODEL),
                  const((2, D_RNN, D_MODEL)), const((1, D_MODEL)),
                  const((D_MODEL, 2 * LANES)), const((1, LANES)),
                  const((T_TOK, T_TOK))],
        out_specs=[row(D_MODEL), row(D_MODEL), col, col, pl.BlockSpec((N_EXPERTS, LANES), lambda i: (i, 0))],
        out_shape=[jax.ShapeDtypeStruct((n, D_MODEL), F32),
                   jax.ShapeDtypeStruct((n, D_MODEL), BF16),
                   jax.ShapeDtypeStruct((SUBLANES, n), jnp.int32),
                   jax.ShapeDtypeStruct((SUBLANES, n), F32),
                   jax.ShapeDtypeStruct((n_tiles * N_EXPERTS, LANES), jnp.int32)],
        compiler_params=_cparams(("arbitrary",)),
        name="out_proj",
    )(y_rnn, y_attn, x2, wo_bf, g2, rw_parts, rb, tri)


def _plan(tile_counts, n_blocks):
    n_tiles = tile_counts.shape[0]
    counts = jnp.sum(tile_counts, axis=0)
    padded = (counts + BM - 1) // BM * BM
    pad_end = jnp.cumsum(padded).astype(jnp.int32)
    pad_start = pad_end - padded
    earlier_tiles = jnp.cumsum(tile_counts, axis=0) - tile_counts
    run_off = (jnp.cumsum(tile_counts, axis=1) - tile_counts).astype(jnp.int32)
    run_dst = (pad_start[None, :] + earlier_tiles).astype(jnp.int32)
    off_lanes = jnp.broadcast_to(run_off.reshape(n_tiles * N_EXPERTS, 1), (n_tiles * N_EXPERTS, LANES))
    blk_start = jnp.arange(n_blocks, dtype=jnp.int32) * BM
    blk_e = jnp.minimum(jnp.sum((pad_end[None, :] <= blk_start[:, None]).astype(jnp.int32), axis=1), N_EXPERTS - 1)
    n_valid = (pad_end[-1] // BM).reshape(1)
    return (run_dst.reshape(-1), tile_counts.reshape(-1).astype(jnp.int32), run_off.reshape(-1), off_lanes,
            blk_e, n_valid, pad_end, padded)


def _tile_positions(route, off_col):
    eid = lax.broadcasted_iota(jnp.int32, (N_EXPERTS, route.shape[1]), 0)
    pos = []
    for k in range(TOP_K):
        start = jnp.sum(jnp.where(eid == route[k:k + 1, :], off_col, 0), axis=0, keepdims=True)
        pos.append(start + route[TOP_K + k:TOP_K + k + 1, :])
    return pos


def _run_copies(run_len_ref, tile, run_bases, make_copy, act):
    def per_expert(e, carry):
        run = tile * N_EXPERTS + e
        length = run_len_ref[run]
        bases = run_bases(run)
        for b in range(RUN_BITS - 1, -1, -1):
            piece_start = length & ~((2 << b) - 1)

            @pl.when((length & (1 << b)) != 0)
            def _():
                act(make_copy(bases, piece_start, 1 << b))
        return carry

    lax.fori_loop(0, N_EXPERTS, per_expert, 0)


def _dispatch_kernel(run_dst_ref, run_len_ref, run_off_ref, pad_end_ref, padded_ref, n_valid_ref,
                     h_ref, route_ref, off_ref, xs_ref, sorted_ref, zeros_ref, sem, zsem):
    i = pl.program_id(0)
    n_tiles = pl.num_programs(0)
    slot = i % 2
    blk_rows = BM * ROW_TILES
    n_blocks = xs_ref.shape[0] // blk_rows

    def run_bases(run):
        return run_off_ref[run], run_dst_ref[run]

    def run_copy(s, bases, piece_start, size):
        src_row = pl.multiple_of((bases[0] + piece_start) * ROW_TILES, ROW_TILES)
        dst_row = pl.multiple_of((bases[1] + piece_start) * ROW_TILES, ROW_TILES)
        return pltpu.make_async_copy(sorted_ref.at[s, pl.ds(src_row, size * ROW_TILES), :],
                                     xs_ref.at[pl.ds(dst_row, size * ROW_TILES), :], sem.at[s])

    def start_runs(tile, s):
        _run_copies(run_len_ref, tile, run_bases, functools.partial(run_copy, s), lambda cp: cp.start())

    def wait_runs(s):
        pltpu.make_async_copy(sorted_ref.at[s], xs_ref.at[pl.ds(0, sorted_ref.shape[1]), :], sem.at[s]).wait()

    @pl.when(i == 0)
    def _():
        zeros_ref[...] = jnp.zeros_like(zeros_ref)

        def zero_copy(blk):
            start = pl.multiple_of(blk * blk_rows, blk_rows)
            return pltpu.make_async_copy(zeros_ref, xs_ref.at[pl.ds(start, blk_rows), :], zsem)

        def zero_blocks(act):
            def tail(e, carry):
                @pl.when(padded_ref[e] > 0)
                def _():
                    act(zero_copy(pad_end_ref[e] // BM - 1))
                return carry

            def dead(b, carry):
                act(zero_copy(b))
                return carry

            lax.fori_loop(0, N_EXPERTS, tail, 0)
            lax.fori_loop(n_valid_ref[0], n_blocks, dead, 0)

        zero_blocks(lambda cp: cp.start())
        zero_blocks(lambda cp: cp.wait())

    @pl.when(i >= 2)
    def _():
        wait_runs(slot)

    pos = _tile_positions(route_ref[...], off_ref[:, 0:1])
    hb = h_ref[...]

    for c in range(TOP_K * T_TOK // P_ROWS):
        row = c * P_ROWS + lax.broadcasted_iota(jnp.int32, (P_ROWS, T_TOK), 0)
        hit = row == pos[0]
        for k in range(1, TOP_K):
            hit = hit | (row == pos[k])
        perm = jnp.where(hit, 1.0, 0.0).astype(BF16)
        rows = jnp.dot(perm, hb, preferred_element_type=F32)
        _to_tile_rows(sorted_ref.at[slot, pl.ds(c * P_ROWS * ROW_TILES, P_ROWS * ROW_TILES), :], rows)
    start_runs(i, slot)

    @pl.when(i == n_tiles - 1)
    def _():
        @pl.when(i >= 1)
        def _():
            wait_runs(1 - slot)
        wait_runs(slot)


def _dispatch(h2, route, off_lanes, run_dst, run_len, run_off, pad_end, padded, n_valid, n_rows):
    n = h2.shape[0]
    grid_spec = pltpu.PrefetchScalarGridSpec(
        num_scalar_prefetch=6,
        grid=(n // T_TOK,),
        in_specs=[
            pl.BlockSpec((T_TOK, D_MODEL), lambda i, *_: (i, 0)),
            pl.BlockSpec((SUBLANES, T_TOK), lambda i, *_: (0, i)),
            pl.BlockSpec((N_EXPERTS, LANES), lambda i, *_: (i, 0)),
        ],
        out_specs=pl.BlockSpec(memory_space=pl.ANY),
        scratch_shapes=[pltpu.VMEM((2, TOP_K * T_TOK * ROW_TILES, LANES), F32),
                        pltpu.VMEM((BM * ROW_TILES, LANES), F32),
                        pltpu.SemaphoreType.DMA((2,)), pltpu.SemaphoreType.DMA(())],
    )
    return pl.pallas_call(
        _dispatch_kernel,
        grid_spec=grid_spec,
        out_shape=jax.ShapeDtypeStruct((n_rows * ROW_TILES, LANES), F32),
        compiler_params=_cparams(("arbitrary",)),
        name="dispatch",
    )(run_dst, run_len, run_off, pad_end, padded, n_valid, h2, route, off_lanes)


def _experts_kernel(blk_e_ref, n_valid_ref, xs_ref, w1_ref, b1_ref, w2_ref, b2_ref, y_ref, w1b_ref, w2b_ref):
    i = pl.program_id(0)

    @pl.when(i < n_valid_ref[0])
    def _():
        prev_e = blk_e_ref[jnp.maximum(i - 1, 0)]

        @pl.when((i == 0) | (blk_e_ref[i] != prev_e))
        def _():
            w1b_ref[...] = w1_ref[0].astype(BF16)
            w2b_ref[...] = w2_ref[0].astype(BF16)

        x = _from_tile_rows(xs_ref, BM).astype(BF16)
        hcat = jnp.dot(x, w1b_ref[...], preferred_element_type=F32) + b1_ref[0]
        gate = jnp.minimum(hcat[:, :D_FF], SWIGLU_LIMIT)
        up = jnp.clip(hcat[:, D_FF:], -SWIGLU_LIMIT, SWIGLU_LIMIT)
        act = gate * jax.nn.sigmoid(SWIGLU_ALPHA * gate) * (up + 1.0)
        y = jnp.dot(act.astype(BF16), w2b_ref[...], preferred_element_type=F32) + b2_ref[0]
        _to_tile_rows(y_ref, y)

    @pl.when(i >= n_valid_ref[0])
    def _():
        y_ref[...] = jnp.zeros_like(y_ref)


def _experts(xs, blk_e, n_valid, w1, b1, w2, b2):
    blk_rows = BM * ROW_TILES
    n_blocks = xs.shape[0] // blk_rows
    exp3 = lambda i, be, nv: (be[i], 0, 0)
    grid_spec = pltpu.PrefetchScalarGridSpec(
        num_scalar_prefetch=2,
        grid=(n_blocks,),
        in_specs=[
            pl.BlockSpec((blk_rows, LANES), lambda i, be, nv: (jnp.minimum(i, nv[0] - 1), 0)),
            pl.BlockSpec((1, D_MODEL, 2 * D_FF), exp3),
            pl.BlockSpec((1, 1, 2 * D_FF), exp3),
            pl.BlockSpec((1, D_FF, D_MODEL), exp3),
            pl.BlockSpec((1, 1, D_MODEL), exp3),
        ],
        out_specs=pl.BlockSpec((blk_rows, LANES), lambda i, be, nv: (i, 0)),
        scratch_shapes=[pltpu.VMEM((D_MODEL, 2 * D_FF), BF16), pltpu.VMEM((D_FF, D_MODEL), BF16)],
    )
    return pl.pallas_call(
        _experts_kernel,
        grid_spec=grid_spec,
        out_shape=jax.ShapeDtypeStruct(xs.shape, F32),
        compiler_params=_cparams(("arbitrary",)),
        name="experts",
    )(blk_e, n_valid, xs, w1, b1, w2, b2)


def _combine_kernel(run_src_ref, run_len_ref, run_off_ref, x1_ref, route_ref, gates_ref, off_ref, ys_ref, o_ref,
                    buf_ref, sem):
    i = pl.program_id(0)
    n_tiles = pl.num_programs(0)
    slot = i % 2

    def run_bases(run):
        return run_src_ref[run], run_off_ref[run]

    def run_copy(s, bases, piece_start, size):
        src_row = pl.multiple_of((bases[0] + piece_start) * ROW_TILES, ROW_TILES)
        dst_row = pl.multiple_of((bases[1] + piece_start) * ROW_TILES, ROW_TILES)
        return pltpu.make_async_copy(ys_ref.at[pl.ds(src_row, size * ROW_TILES), :],
                                     buf_ref.at[s, pl.ds(dst_row, size * ROW_TILES), :], sem.at[s])

    def start_runs(tile, s):
        _run_copies(run_len_ref, tile, run_bases, functools.partial(run_copy, s), lambda cp: cp.start())

    @pl.when(i == 0)
    def _():
        start_runs(0, 0)

    @pl.when(i + 1 < n_tiles)
    def _():
        start_runs(i + 1, 1 - slot)

    pltpu.make_async_copy(ys_ref.at[pl.ds(0, buf_ref.shape[1]), :], buf_ref.at[slot], sem.at[slot]).wait()

    pos = _tile_positions(route_ref[...], off_ref[:, 0:1])
    sub = lax.broadcasted_iota(jnp.int32, (SUBLANES, T_TOK), 0)
    gates = gates_ref[...]
    packed = jnp.zeros((SUBLANES, T_TOK), F32)
    for k in range(TOP_K):
        packed = jnp.where(sub == k, pos[k].astype(F32), packed)
        packed = jnp.where(sub == TOP_K + k, gates[k:k + 1, :], packed)
    cols = packed.T
    acc = x1_ref[...]
    for c in range(TOP_K * T_TOK // P_ROWS):
        row = (c * P_ROWS + lax.broadcasted_iota(jnp.int32, (T_TOK, P_ROWS), 1)).astype(F32)
        g = jnp.zeros((T_TOK, P_ROWS), F32)
        for k in range(TOP_K):
            g = jnp.where(row == cols[:, k:k + 1], cols[:, TOP_K + k:TOP_K + k + 1], g)
        y = _from_tile_rows(buf_ref.at[slot, pl.ds(c * P_ROWS * ROW_TILES, P_ROWS * ROW_TILES), :], P_ROWS)
        acc = acc + jnp.dot(g.astype(BF16), y.astype(BF16), preferred_element_type=F32)
    o_ref[...] = acc


def _combine(x1, route, gates, off_lanes, run_dst, run_len, run_off, ys):
    n = x1.shape[0]
    grid_spec = pltpu.PrefetchScalarGridSpec(
        num_scalar_prefetch=3,
        grid=(n // T_TOK,),
        in_specs=[
            pl.BlockSpec((T_TOK, D_MODEL), lambda i, *_: (i, 0)),
            pl.BlockSpec((SUBLANES, T_TOK), lambda i, *_: (0, i)),
            pl.BlockSpec((SUBLANES, T_TOK), lambda i, *_: (0, i)),
            pl.BlockSpec((N_EXPERTS, LANES), lambda i, *_: (i, 0)),
            pl.BlockSpec(memory_space=pl.ANY),
        ],
        out_specs=pl.BlockSpec((T_TOK, D_MODEL), lambda i, *_: (i, 0)),
        scratch_shapes=[pltpu.VMEM((2, TOP_K * T_TOK * ROW_TILES, LANES), F32), pltpu.SemaphoreType.DMA((2,))],
    )
    return pl.pallas_call(
        _combine_kernel,
        grid_spec=grid_spec,
        out_shape=jax.ShapeDtypeStruct((n, D_MODEL), F32),
        compiler_params=_cparams(("arbitrary",)),
        name="combine",
    )(run_dst, run_len, run_off, x1, route, gates, off_lanes, ys)


def _block_diag(w):
    n, r, _ = w.shape
    eye = jnp.eye(n, dtype=w.dtype)
    return (eye[:, None, :, None] * w[:, :, None, :]).reshape(n * r, n * r)


def kernel(x, norm1_g, w_in, conv_w, conv_b, lru_wa, lru_ba, lru_wx, lru_bx, lru_lambda, q_norm_g, k_norm_g,
           lambda_q1, lambda_k1, lambda_q2, lambda_k2, subln_g, w_out, norm2_g, router_w, router_b, w1, b1, w2, b2):
    bsz, seq, d = x.shape
    n_tok = bsz * seq
    assert d == D_MODEL and n_tok % TM_PROJ == 0 and seq % T_SCAN == 0 and seq % TQ == 0 and n_tok % T_TOK == 0
    assert (n_tok * TOP_K) % BM == 0
    assert norm1_g.shape[0] == 1, "single-layer stack"
    x2 = x.reshape(n_tok, d)

    z = _in_proj(x2, norm1_g[0][None, :], w_in[0].astype(BF16))
    z3 = z.reshape(bsz, seq, D_IN)

    n_slabs = D_RNN // LANES
    per_slab = LANES // RNN_BLOCK
    wa = lru_wa[0].reshape(n_slabs, per_slab, RNN_BLOCK, RNN_BLOCK)
    wx = lru_wx[0].reshape(n_slabs, per_slab, RNN_BLOCK, RNN_BLOCK)
    wg = jnp.concatenate([jax.vmap(_block_diag)(wa), jax.vmap(_block_diag)(wx)], axis=2).astype(BF16)
    bg = jnp.concatenate([lru_ba[0].reshape(n_slabs, 1, LANES), lru_bx[0].reshape(n_slabs, 1, LANES)], axis=2)
    y_rnn = _rnn(z3, conv_w[0], conv_b[0][None, :], wg, bg, lru_lambda[0].reshape(n_slabs, 1, LANES))

    half = jnp.arange(LANES) // HEAD_DIM
    ones_bd = (half[:, None] == half[None, :]).astype(BF16)
    y_attn = _attn(z3, jnp.tile(q_norm_g[0], 2)[None, :], jnp.tile(k_norm_g[0], 2)[None, :],
                   lambda_q1[0][None, :], lambda_k1[0][None, :], lambda_q2[0][None, :], lambda_k2[0][None, :],
                   subln_g[0][None, :], ones_bd)

    rw = jnp.pad(router_w[0], ((0, 0), (0, LANES - N_EXPERTS)))
    rw_hi = rw.astype(BF16)
    rw_lo = (rw - rw_hi.astype(F32)).astype(BF16)
    rw_parts = jnp.concatenate([rw_hi, rw_lo], axis=1)
    rb = jnp.pad(router_b[0], (0, LANES - N_EXPERTS))[None, :]
    tok = jnp.arange(T_TOK)
    tri = (tok[:, None] < tok[None, :]).astype(BF16)
    x1, h2, route, gates, counts = _out_proj(
        y_rnn.reshape(n_tok, D_RNN), y_attn.reshape(n_tok, D_ATTN), x2,
        w_out[0].astype(BF16).reshape(2, D_RNN, D_MODEL), norm2_g[0][None, :], rw_parts, rb, tri)

    n_blocks = (n_tok * TOP_K) // BM + N_EXPERTS
    tile_counts = counts[:, 0].reshape(n_tok // T_TOK, N_EXPERTS)
    run_dst, run_len, run_off, off_lanes, blk_e, n_valid, pad_end, padded = _plan(tile_counts, n_blocks)
    xs = _dispatch(h2, route, off_lanes, run_dst, run_len, run_off, pad_end, padded, n_valid, n_blocks * BM)
    ys = _experts(xs, blk_e, n_valid, w1[0], b1[0][:, None, :], w2[0], b2[0][:, None, :])
    out = _combine(x1, route, gates, off_lanes, run_dst, run_len, run_off, ys)
    return out.reshape(bsz, seq, d)
```

```python
import functools
import math

import jax
import jax.numpy as jnp
from jax import lax
from jax.experimental import pallas as pl
from jax.experimental.pallas import tpu as pltpu

F32 = jnp.float32
BF16 = jnp.bfloat16

D_MODEL = 1024
D_RNN = 512
RNN_BLOCK = 64
CONV_WIDTH = 4
LRU_C = 8.0
HEAD_DIM = 64
N_HEADS = 4
D_ATTN = 512
D_IN = 2 * D_RNN + 3 * D_ATTN
N_EXPERTS = 32
TOP_K = 4
D_FF = 1024
SWIGLU_LIMIT = 7.0
SWIGLU_ALPHA = 1.702
EPS = 1e-5
LAM_INIT = 0.8 - 0.6 * math.exp(0.0)
LOG2_E = math.log2(math.e)

LANES = 128
SUBLANES = 8
ROW_TILES = D_MODEL // LANES
VMEM_LIMIT = 52 * 1024 * 1024

TM_PROJ = 1024
T_SCAN = 1024
TQ = 128
BM = 512
T_TOK = 512
P_ROWS = 256
RUN_BITS = T_TOK.bit_length()


def _cparams(sem):
    return pltpu.CompilerParams(dimension_semantics=sem, vmem_limit_bytes=VMEM_LIMIT)


def _to_tile_rows(ref, x):
    rows = x.shape[0]
    for s in range(ROW_TILES):
        ref[pl.ds(s, rows, stride=ROW_TILES), :] = x[:, s * LANES:(s + 1) * LANES]


def _from_tile_rows(ref, rows):
    return jnp.concatenate([ref[pl.ds(s, rows, stride=ROW_TILES), :] for s in range(ROW_TILES)], axis=1)


def _in_proj_kernel(x_ref, g_ref, w_ref, z_ref):
    x = x_ref[...]
    ms = jnp.mean(x * x, axis=-1, keepdims=True)
    h = x * lax.rsqrt(ms + EPS) * g_ref[...]
    z_ref[...] = jnp.dot(h.astype(BF16), w_ref[...], preferred_element_type=F32)


def _in_proj(x2, g, w_bf):
    n = x2.shape[0]
    return pl.pallas_call(
        _in_proj_kernel,
        grid=(n // TM_PROJ,),
        in_specs=[
            pl.BlockSpec((TM_PROJ, D_MODEL), lambda i: (i, 0)),
            pl.BlockSpec((1, D_MODEL), lambda i: (0, 0)),
            pl.BlockSpec((D_MODEL, D_IN), lambda i: (0, 0)),
        ],
        out_specs=pl.BlockSpec((TM_PROJ, D_IN), lambda i: (i, 0)),
        out_shape=jax.ShapeDtypeStruct((n, D_IN), F32),
        compiler_params=_cparams(("arbitrary",)),
        name="in_proj",
    )(x2, g, w_bf)


def _rnn_kernel(xr_ref, gr_ref, cw_ref, cb_ref, wg_ref, bg_ref, lam_ref, y_ref):
    seq = xr_ref.shape[1]
    n_chunks = seq // T_SCAN
    n_groups = T_SCAN // SUBLANES
    cw = cw_ref[...]
    cb = cb_ref[...]
    nl = -lam_ref[0]
    softplus_neg_lam = jnp.maximum(nl, 0.0) + jnp.log(1.0 + jnp.exp(-jnp.abs(nl)))
    group = lax.broadcasted_iota(jnp.int32, (n_groups, LANES), 0)

    def previous_group(v, first):
        return jnp.where(group >= 1, pltpu.roll(v, 1, axis=0), first)

    def phase_rows(t0, r):
        return pl.ds(t0 + r, n_groups, stride=SUBLANES)

    def chunk(c, carry):
        h_prev, x_tail = carry[0], carry[1:]
        t0 = pl.multiple_of(c * T_SCAN, T_SCAN)
        x = [xr_ref[0, phase_rows(t0, r), :] for r in range(SUBLANES)]
        wrapped = [previous_group(x[SUBLANES - j], x_tail[CONV_WIDTH - 1 - j]) for j in range(1, CONV_WIDTH)]

        def delayed(r, j):
            return x[r - j] if r >= j else wrapped[j - r - 1]

        conv = []
        for r in range(SUBLANES):
            acc = cb + cw[CONV_WIDTH - 1:CONV_WIDTH, :] * x[r]
            for j in range(1, CONV_WIDTH):
                acc = acc + cw[CONV_WIDTH - 1 - j:CONV_WIDTH - j, :] * delayed(r, j)
            conv.append(acc)
        conv = jnp.concatenate(conv, axis=0)
        gates = jnp.dot(conv.astype(BF16), wg_ref[0], preferred_element_type=F32) + bg_ref[0]
        rg = jax.nn.sigmoid(gates[:, :LANES])
        ig = jax.nn.sigmoid(gates[:, LANES:])
        a = jnp.exp(-LRU_C * rg * softplus_neg_lam)
        var = 1.0 - a * a
        u = jnp.where(var > 0.0, var * lax.rsqrt(var), 0.0) * (ig * conv)
        ph = lambda v, r: v[r * n_groups:(r + 1) * n_groups, :]
        a_in, u_in = [ph(a, 0)], [ph(u, 0)]
        for r in range(1, SUBLANES):
            a_in.append(ph(a, r) * a_in[-1])
            u_in.append(ph(a, r) * u_in[-1] + ph(u, r))
        ga, gu = a_in[-1], u_in[-1]
        d = 1
        while d < n_groups:
            keep = group >= d
            gu = jnp.where(keep, ga * pltpu.roll(gu, d, axis=0) + gu, gu)
            ga = jnp.where(keep, ga * pltpu.roll(ga, d, axis=0), ga)
            d *= 2
        h_after = ga * h_prev + gu
        h_before = previous_group(h_after, h_prev)
        for r in range(SUBLANES):
            h = a_in[r] * h_before + u_in[r]
            rows = phase_rows(t0, r)
            y_ref[0, rows, :] = h * jax.nn.gelu(gr_ref[0, rows, :], approximate=True)
        last = slice(n_groups - 1, n_groups)
        return (h_after[last, :],) + tuple(x[SUBLANES - CONV_WIDTH + 1 + j][last, :] for j in range(CONV_WIDTH - 1))

    zero = jnp.zeros((1, LANES), F32)
    lax.fori_loop(0, n_chunks, chunk, (zero,) * CONV_WIDTH)


def _rnn(z3, conv_w, conv_b, wg_slabs, bg_slabs, lam_slabs):
    bsz, seq, _ = z3.shape
    n_slabs = D_RNN // LANES
    slab = lambda rows: pl.BlockSpec((rows, LANES), lambda b, c: (0, c))
    slab3 = lambda shape: pl.BlockSpec((1,) + shape, lambda b, c: (c, 0, 0))
    return pl.pallas_call(
        _rnn_kernel,
        grid=(bsz, n_slabs),
        in_specs=[
            pl.BlockSpec((1, seq, LANES), lambda b, c: (b, 0, c)),
            pl.BlockSpec((1, seq, LANES), lambda b, c: (b, 0, n_slabs + c)),
            slab(CONV_WIDTH),
            slab(1),
            slab3((LANES, 2 * LANES)),
            slab3((1, 2 * LANES)),
            slab3((1, LANES)),
        ],
        out_specs=pl.BlockSpec((1, seq, LANES), lambda b, c: (b, 0, c)),
        out_shape=jax.ShapeDtypeStruct((bsz, seq, D_RNN), F32),
        compiler_params=_cparams(("arbitrary", "arbitrary")),
        name="rnn",
    )(z3, z3, conv_w, conv_b, wg_slabs, bg_slabs, lam_slabs)


def _group_rms(x, ones_bd):
    x2 = x * x
    hi = x2.astype(BF16)
    lo = (x2 - hi.astype(F32)).astype(BF16)
    ssq = jnp.dot(hi, ones_bd, preferred_element_type=F32) + jnp.dot(lo, ones_bd, preferred_element_type=F32)
    return x * lax.rsqrt(ssq * (1.0 / HEAD_DIM) + EPS)


def _attn_kernel(q_ref, k_ref, v_ref, qg_ref, kg_ref, lq1_ref, lk1_ref, lq2_ref, lk2_ref, sg_ref, ones_ref, o_ref):
    seq = q_ref.shape[1]
    ones_bd = ones_ref[...]
    lam = (jnp.exp(jnp.sum(lq1_ref[...] * lk1_ref[...], axis=-1, keepdims=True))
           - jnp.exp(jnp.sum(lq2_ref[...] * lk2_ref[...], axis=-1, keepdims=True)) + LAM_INIT)
    qn = _group_rms(q_ref[0], ones_bd) * qg_ref[...] * (HEAD_DIM ** -0.5 * LOG2_E)
    kn = _group_rms(k_ref[0], ones_bd) * kg_ref[...]
    lane = lax.broadcasted_iota(jnp.int32, (seq, LANES), 1)
    q1 = jnp.where(lane < HEAD_DIM, qn, 0.0).astype(BF16)
    q2 = jnp.where(lane >= HEAD_DIM, qn, 0.0).astype(BF16)
    kb = kn.astype(BF16)
    vb = v_ref[0].astype(BF16)
    sg = sg_ref[...]
    dn = (((1,), (1,)), ((), ()))
    n_blk = seq // TQ

    def scores(qi, qm):
        kv = (qi + 1) * TQ
        return lax.dot_general(qm[qi * TQ:kv], kb[:kv], dn, preferred_element_type=F32)

    causal = (lax.broadcasted_iota(jnp.int32, (TQ, TQ), 1) <= lax.broadcasted_iota(jnp.int32, (TQ, TQ), 0))

    def weights(qi, s1, s2):
        c0 = qi * TQ

        def probs(s):
            diag = jnp.where(causal, s[:, c0:], -jnp.inf)
            m = jnp.max(diag, axis=-1, keepdims=True)
            if qi > 0:
                m = jnp.maximum(m, jnp.max(s[:, :c0], axis=-1, keepdims=True))
            e = jnp.exp2(diag - m)
            l = jnp.sum(e, axis=-1, keepdims=True)
            if qi > 0:
                e_prev = jnp.exp2(s[:, :c0] - m)
                l = l + jnp.sum(e_prev, axis=-1, keepdims=True)
                e = jnp.concatenate([e_prev, e], axis=1)
            return e, l

        e1, l1 = probs(s1)
        e2, l2 = probs(s2)
        return (e1 - e2 * (lam * l1 / l2)).astype(BF16), 1.0 / l1

    def values(qi, w_and_scale):
        w, scale = w_and_scale
        kv = (qi + 1) * TQ
        o = jnp.dot(w, vb[:kv], preferred_element_type=F32) * scale
        o = o * lax.rsqrt(jnp.mean(o * o, axis=-1, keepdims=True) + EPS) * sg * (1.0 - LAM_INIT)
        o_ref[0, qi * TQ:kv, :] = o.astype(o_ref.dtype)

    s = {0: (scores(0, q1), scores(0, q2))}
    w = {}
    for t in range(n_blk + 1):
        if t + 1 < n_blk:
            s[t + 1] = (scores(t + 1, q1), scores(t + 1, q2))
        if t < n_blk:
            w[t] = weights(t, *s.pop(t))
        if t >= 1:
            values(t - 1, w.pop(t - 1))


def _attn(z3, qg2, kg2, lq1, lk1, lq2, lk2, sg, ones_bd):
    bsz, seq, _ = z3.shape
    qoff = 2 * D_RNN // LANES
    koff = qoff + D_ATTN // LANES
    voff = koff + D_ATTN // LANES
    const = lambda shape: pl.BlockSpec(shape, lambda b, h: (0,) * len(shape))
    return pl.pallas_call(
        _attn_kernel,
        grid=(bsz, N_HEADS),
        in_specs=[
            pl.BlockSpec((1, seq, LANES), lambda b, h: (b, 0, qoff + h)),
            pl.BlockSpec((1, seq, LANES), lambda b, h: (b, 0, koff + h)),
            pl.BlockSpec((1, seq, LANES), lambda b, h: (b, 0, voff + h)),
            const((1, LANES)), const((1, LANES)),
            const((1, HEAD_DIM)), const((1, HEAD_DIM)), const((1, HEAD_DIM)), const((1, HEAD_DIM)),
            const((1, LANES)), const((LANES, LANES)),
        ],
        out_specs=pl.BlockSpec((1, seq, LANES), lambda b, h: (b, 0, h)),
        out_shape=jax.ShapeDtypeStruct((bsz, seq, D_ATTN), BF16),
        compiler_params=_cparams(("arbitrary", "arbitrary")),
        name="attn",
    )(z3, z3, z3, qg2, kg2, lq1, lk1, lq2, lk2, sg, ones_bd)


def _out_proj_kernel(yr_ref, ya_ref, x_ref, wo_ref, g_ref, rw_ref, rb_ref, tri_ref,
                     x1_ref, h2_ref, route_ref, gates_ref, counts_ref):
    acc = jnp.dot(yr_ref[...].astype(BF16), wo_ref[0], preferred_element_type=F32)
    acc = acc + jnp.dot(ya_ref[...], wo_ref[1], preferred_element_type=F32)
    x1 = x_ref[...] + acc
    x1_ref[...] = x1
    h2 = x1 * lax.rsqrt(jnp.mean(x1 * x1, axis=-1, keepdims=True) + EPS) * g_ref[...]
    hh = h2.astype(BF16)
    h2_ref[...] = hh
    hl = (h2 - hh.astype(F32)).astype(BF16)
    ph = jnp.dot(hh, rw_ref[...], preferred_element_type=F32)
    pl_ = jnp.dot(hl, rw_ref[...], preferred_element_type=F32)
    logits = (ph[:, :LANES] + ph[:, LANES:]) + (pl_[:, :LANES] + pl_[:, LANES:]) + rb_ref[...]
    tm = logits.shape[0]
    l = logits.T[:N_EXPERTS, :]
    eid = lax.broadcasted_iota(jnp.int32, (N_EXPERTS, tm), 0)
    vals, idxs = [], []
    for _ in range(TOP_K):
        m = jnp.max(l, axis=0, keepdims=True)
        idx = jnp.min(jnp.where(l == m, eid, N_EXPERTS), axis=0, keepdims=True)
        vals.append(m)
        idxs.append(idx)
        l = jnp.where(eid == idx, -jnp.inf, l)
    es = [jnp.exp(v - vals[0]) for v in vals]
    inv = 1.0 / (es[0] + es[1] + es[2] + es[3])
    chosen = jnp.zeros((N_EXPERTS, tm), F32)
    for k in range(TOP_K):
        chosen = chosen + (eid == idxs[k]).astype(F32)
    before = jnp.dot(chosen.astype(BF16), tri_ref[...], preferred_element_type=F32)
    sub = lax.broadcasted_iota(jnp.int32, (SUBLANES, tm), 0)
    route = jnp.zeros((SUBLANES, tm), jnp.int32)
    gates = jnp.zeros((SUBLANES, tm), F32)
    for k in range(TOP_K):
        rank = jnp.sum(jnp.where(eid == idxs[k], before, 0.0), axis=0, keepdims=True).astype(jnp.int32)
        route = jnp.where(sub == k, idxs[k], route)
        route = jnp.where(sub == TOP_K + k, rank, route)
        gates = jnp.where(sub == k, es[k] * inv, gates)
    route_ref[...] = route
    gates_ref[...] = gates
    counts_ref[...] = jnp.broadcast_to(jnp.sum(chosen, axis=1, keepdims=True), (N_EXPERTS, LANES)).astype(jnp.int32)


def _out_proj(y_rnn, y_attn, x2, wo_bf, g2, rw_parts, rb, tri):
    n = x2.shape[0]
    n_tiles = n // T_TOK
    row = lambda w: pl.BlockSpec((T_TOK, w), lambda i: (i, 0))
    col = pl.BlockSpec((SUBLANES, T_TOK), lambda i: (0, i))
    const = lambda shape: pl.BlockSpec(shape, lambda i: (0,) * len(shape))
    return pl.pallas_call(
        _out_proj_kernel,
        grid=(n_tiles,),
        in_specs=[row(D_RNN), row(D_ATTN), row(D_MODEL),
                  const((2, D_RNN, D_MODEL)), const((1, D_MODEL)),
                  const((D_MODEL, 2 * LANES)), const((1, LANES)),
                  const((T_TOK, T_TOK))],
        out_specs=[row(D_MODEL), row(D_MODEL), col, col, pl.BlockSpec((N_EXPERTS, LANES), lambda i: (i, 0))],
        out_shape=[jax.ShapeDtypeStruct((n, D_MODEL), F32),
                   jax.ShapeDtypeStruct((n, D_MODEL), BF16),
                   jax.ShapeDtypeStruct((SUBLANES, n), jnp.int32),
                   jax.ShapeDtypeStruct((SUBLANES, n), F32),
                   jax.ShapeDtypeStruct((n_tiles * N_EXPERTS, LANES), jnp.int32)],
        compiler_params=_cparams(("arbitrary",)),
        name="out_proj",
    )(y_rnn, y_attn, x2, wo_bf, g2, rw_parts, rb, tri)


def _plan(tile_counts, n_blocks):
    n_tiles = tile_counts.shape[0]
    counts = jnp.sum(tile_counts, axis=0)
    padded = (counts + BM - 1) // BM * BM
    pad_end = jnp.cumsum(padded).astype(jnp.int32)
    pad_start = pad_end - padded
    earlier_tiles = jnp.cumsum(tile_counts, axis=0) - tile_counts
    run_off = (jnp.cumsum(tile_counts, axis=1) - tile_counts).astype(jnp.int32)
    run_dst = (pad_start[None, :] + earlier_tiles).astype(jnp.int32)
    off_lanes = jnp.broadcast_to(run_off.reshape(n_tiles * N_EXPERTS, 1), (n_tiles * N_EXPERTS, LANES))
    blk_start = jnp.arange(n_blocks, dtype=jnp.int32) * BM
    blk_e = jnp.minimum(jnp.sum((pad_end[None, :] <= blk_start[:, None]).astype(jnp.int32), axis=1), N_EXPERTS - 1)
    n_valid = (pad_end[-1] // BM).reshape(1)
    return (run_dst.reshape(-1), tile_counts.reshape(-1).astype(jnp.int32), run_off.reshape(-1), off_lanes,
            blk_e, n_valid, pad_end, padded)


def _tile_positions(route, off_col):
    eid = lax.broadcasted_iota(jnp.int32, (N_EXPERTS, route.shape[1]), 0)
    pos = []
    for k in range(TOP_K):
        start = jnp.sum(jnp.where(eid == route[k:k + 1, :], off_col, 0), axis=0, keepdims=True)
        pos.append(start + route[TOP_K + k:TOP_K + k + 1, :])
    return pos


def _run_copies(run_len_ref, tile, run_bases, make_copy, act):
    def per_expert(e, carry):
        run = tile * N_EXPERTS + e
        length = run_len_ref[run]
        bases = run_bases(run)
        for b in range(RUN_BITS - 1, -1, -1):
            piece_start = length & ~((2 << b) - 1)

            @pl.when((length & (1 << b)) != 0)
            def _():
                act(make_copy(bases, piece_start, 1 << b))
        return carry

    lax.fori_loop(0, N_EXPERTS, per_expert, 0)


def _dispatch_kernel(run_dst_ref, run_len_ref, run_off_ref, pad_end_ref, padded_ref, n_valid_ref,
                     h_ref, route_ref, off_ref, xs_ref, sorted_ref, zeros_ref, sem, zsem):
    i = pl.program_id(0)
    n_tiles = pl.num_programs(0)
    slot = i % 2
    blk_rows = BM * ROW_TILES
    n_blocks = xs_ref.shape[0] // blk_rows

    def run_bases(run):
        return run_off_ref[run], run_dst_ref[run]

    def run_copy(s, bases, piece_start, size):
        src_row = pl.multiple_of((bases[0] + piece_start) * ROW_TILES, ROW_TILES)
        dst_row = pl.multiple_of((bases[1] + piece_start) * ROW_TILES, ROW_TILES)
        return pltpu.make_async_copy(sorted_ref.at[s, pl.ds(src_row, size * ROW_TILES), :],
                                     xs_ref.at[pl.ds(dst_row, size * ROW_TILES), :], sem.at[s])

    def start_runs(tile, s):
        _run_copies(run_len_ref, tile, run_bases, functools.partial(run_copy, s), lambda cp: cp.start())

    def wait_runs(s):
        pltpu.make_async_copy(sorted_ref.at[s], xs_ref.at[pl.ds(0, sorted_ref.shape[1]), :], sem.at[s]).wait()

    @pl.when(i == 0)
    def _():
        zeros_ref[...] = jnp.zeros_like(zeros_ref)

        def zero_copy(blk):
            start = pl.multiple_of(blk * blk_rows, blk_rows)
            return pltpu.make_async_copy(zeros_ref, xs_ref.at[pl.ds(start, blk_rows), :], zsem)

        def zero_blocks(act):
            def tail(e, carry):
                @pl.when(padded_ref[e] > 0)
                def _():
                    act(zero_copy(pad_end_ref[e] // BM - 1))
                return carry

            def dead(b, carry):
                act(zero_copy(b))
                return carry

            lax.fori_loop(0, N_EXPERTS, tail, 0)
            lax.fori_loop(n_valid_ref[0], n_blocks, dead, 0)

        zero_blocks(lambda cp: cp.start())
        zero_blocks(lambda cp: cp.wait())

    @pl.when(i >= 2)
    def _():
        wait_runs(slot)

    pos = _tile_positions(route_ref[...], off_ref[:, 0:1])
    hb = h_ref[...]

    for c in range(TOP_K * T_TOK // P_ROWS):
        row = c * P_ROWS + lax.broadcasted_iota(jnp.int32, (P_ROWS, T_TOK), 0)
        hit = row == pos[0]
        for k in range(1, TOP_K):
            hit = hit | (row == pos[k])
        perm = jnp.where(hit, 1.0, 0.0).astype(BF16)
        rows = jnp.dot(perm, hb, preferred_element_type=F32)
        _to_tile_rows(sorted_ref.at[slot, pl.ds(c * P_ROWS * ROW_TILES, P_ROWS * ROW_TILES), :], rows)
    start_runs(i, slot)

    @pl.when(i == n_tiles - 1)
    def _():
        @pl.when(i >= 1)
        def _():
            wait_runs(1 - slot)
        wait_runs(slot)


def _dispatch(h2, route, off_lanes, run_dst, run_len, run_off, pad_end, padded, n_valid, n_rows):
    n = h2.shape[0]
    grid_spec = pltpu.PrefetchScalarGridSpec(
        num_scalar_prefetch=6,
        grid=(n // T_TOK,),
        in_specs=[
            pl.BlockSpec((T_TOK, D_MODEL), lambda i, *_: (i, 0)),
            pl.BlockSpec((SUBLANES, T_TOK), lambda i, *_: (0, i)),
            pl.BlockSpec((N_EXPERTS, LANES), lambda i, *_: (i, 0)),
        ],
        out_specs=pl.BlockSpec(memory_space=pl.ANY),
        scratch_shapes=[pltpu.VMEM((2, TOP_K * T_TOK * ROW_TILES, LANES), F32),
                        pltpu.VMEM((BM * ROW_TILES, LANES), F32),
                        pltpu.SemaphoreType.DMA((2,)), pltpu.SemaphoreType.DMA(())],
    )
    return pl.pallas_call(
        _dispatch_kernel,
        grid_spec=grid_spec,
        out_shape=jax.ShapeDtypeStruct((n_rows * ROW_TILES, LANES), F32),
        compiler_params=_cparams(("arbitrary",)),
        name="dispatch",
    )(run_dst, run_len, run_off, pad_end, padded, n_valid, h2, route, off_lanes)


def _experts_kernel(blk_e_ref, n_valid_ref, xs_ref, w1_ref, b1_ref, w2_ref, b2_ref, y_ref, w1b_ref, w2b_ref):
    i = pl.program_id(0)

    @pl.when(i < n_valid_ref[0])
    def _():
        prev_e = blk_e_ref[jnp.maximum(i - 1, 0)]

        @pl.when((i == 0) | (blk_e_ref[i] != prev_e))
        def _():
            w1b_ref[...] = w1_ref[0].astype(BF16)
            w2b_ref[...] = w2_ref[0].astype(BF16)

        x = _from_tile_rows(xs_ref, BM).astype(BF16)
        hcat = jnp.dot(x, w1b_ref[...], preferred_element_type=F32) + b1_ref[0]
        gate = jnp.minimum(hcat[:, :D_FF], SWIGLU_LIMIT)
        up = jnp.clip(hcat[:, D_FF:], -SWIGLU_LIMIT, SWIGLU_LIMIT)
        act = gate * jax.nn.sigmoid(SWIGLU_ALPHA * gate) * (up + 1.0)
        y = jnp.dot(act.astype(BF16), w2b_ref[...], preferred_element_type=F32) + b2_ref[0]
        _to_tile_rows(y_ref, y)

    @pl.when(i >= n_valid_ref[0])
    def _():
        y_ref[...] = jnp.zeros_like(y_ref)


def _experts(xs, blk_e, n_valid, w1, b1, w2, b2):
    blk_rows = BM * ROW_TILES
    n_blocks = xs.shape[0] // blk_rows
    exp3 = lambda i, be, nv: (be[i], 0, 0)
    grid_spec = pltpu.PrefetchScalarGridSpec(
        num_scalar_prefetch=2,
        grid=(n_blocks,),
        in_specs=[
            pl.BlockSpec((blk_rows, LANES), lambda i, be, nv: (jnp.minimum(i, nv[0] - 1), 0)),
            pl.BlockSpec((1, D_MODEL, 2 * D_FF), exp3),
            pl.BlockSpec((1, 1, 2 * D_FF), exp3),
            pl.BlockSpec((1, D_FF, D_MODEL), exp3),
            pl.BlockSpec((1, 1, D_MODEL), exp3),
        ],
        out_specs=pl.BlockSpec((blk_rows, LANES), lambda i, be, nv: (i, 0)),
        scratch_shapes=[pltpu.VMEM((D_MODEL, 2 * D_FF), BF16), pltpu.VMEM((D_FF, D_MODEL), BF16)],
    )
    return pl.pallas_call(
        _experts_kernel,
        grid_spec=grid_spec,
        out_shape=jax.ShapeDtypeStruct(xs.shape, F32),
        compiler_params=_cparams(("arbitrary",)),
        name="experts",
    )(blk_e, n_valid, xs, w1, b1, w2, b2)


def _combine_kernel(run_src_ref, run_len_ref, run_off_ref, x1_ref, route_ref, gates_ref, off_ref, ys_ref, o_ref,
                    buf_ref, sem):
    i = pl.program_id(0)
    n_tiles = pl.num_programs(0)
    slot = i % 2

    def run_bases(run):
        return run_src_ref[run], run_off_ref[run]

    def run_copy(s, bases, piece_start, size):
        src_row = pl.multiple_of((bases[0] + piece_start) * ROW_TILES, ROW_TILES)
        dst_row = pl.multiple_of((bases[1] + piece_start) * ROW_TILES, ROW_TILES)
        return pltpu.make_async_copy(ys_ref.at[pl.ds(src_row, size * ROW_TILES), :],
                                     buf_ref.at[s, pl.ds(dst_row, size * ROW_TILES), :], sem.at[s])

    def start_runs(tile, s):
        _run_copies(run_len_ref, tile, run_bases, functools.partial(run_copy, s), lambda cp: cp.start())

    @pl.when(i == 0)
    def _():
        start_runs(0, 0)

    @pl.when(i + 1 < n_tiles)
    def _():
        start_runs(i + 1, 1 - slot)

    pltpu.make_async_copy(ys_ref.at[pl.ds(0, buf_ref.shape[1]), :], buf_ref.at[slot], sem.at[slot]).wait()

    pos = _tile_positions(route_ref[...], off_ref[:, 0:1])
    sub = lax.broadcasted_iota(jnp.int32, (SUBLANES, T_TOK), 0)
    gates = gates_ref[...]
    packed = jnp.zeros((SUBLANES, T_TOK), F32)
    for k in range(TOP_K):
        packed = jnp.where(sub == k, pos[k].astype(F32), packed)
        packed = jnp.where(sub == TOP_K + k, gates[k:k + 1, :], packed)
    cols = packed.T
    acc = x1_ref[...]
    for c in range(TOP_K * T_TOK // P_ROWS):
        row = (c * P_ROWS + lax.broadcasted_iota(jnp.int32, (T_TOK, P_ROWS), 1)).astype(F32)
        g = jnp.zeros((T_TOK, P_ROWS), F32)
        for k in range(TOP_K):
            g = jnp.where(row == cols[:, k:k + 1], cols[:, TOP_K + k:TOP_K + k + 1], g)
        y = _from_tile_rows(buf_ref.at[slot, pl.ds(c * P_ROWS * ROW_TILES, P_ROWS * ROW_TILES), :], P_ROWS)
        acc = acc + jnp.dot(g.astype(BF16), y.astype(BF16), preferred_element_type=F32)
    o_ref[...] = acc


def _combine(x1, route, gates, off_lanes, run_dst, run_len, run_off, ys):
    n = x1.shape[0]
    grid_spec = pltpu.PrefetchScalarGridSpec(
        num_scalar_prefetch=3,
        grid=(n // T_TOK,),
        in_specs=[
            pl.BlockSpec((T_TOK, D_MODEL), lambda i, *_: (i, 0)),
            pl.BlockSpec((SUBLANES, T_TOK), lambda i, *_: (0, i)),
            pl.BlockSpec((SUBLANES, T_TOK), lambda i, *_: (0, i)),
            pl.BlockSpec((N_EXPERTS, LANES), lambda i, *_: (i, 0)),
            pl.BlockSpec(memory_space=pl.ANY),
        ],
        out_specs=pl.BlockSpec((T_TOK, D_MODEL), lambda i, *_: (i, 0)),
        scratch_shapes=[pltpu.VMEM((2, TOP_K * T_TOK * ROW_TILES, LANES), F32), pltpu.SemaphoreType.DMA((2,))],
    )
    return pl.pallas_call(
        _combine_kernel,
        grid_spec=grid_spec,
        out_shape=jax.ShapeDtypeStruct((n, D_MODEL), F32),
        compiler_params=_cparams(("arbitrary",)),
        name="combine",
    )(run_dst, run_len, run_off, x1, route, gates, off_lanes, ys)


def _block_diag(w):
    n, r, _ = w.shape
    eye = jnp.eye(n, dtype=w.dtype)
    return (eye[:, None, :, None] * w[:, :, None, :]).reshape(n * r, n * r)


def kernel(x, norm1_g, w_in, conv_w, conv_b, lru_wa, lru_ba, lru_wx, lru_bx, lru_lambda, q_norm_g, k_norm_g,
           lambda_q1, lambda_k1, lambda_q2, lambda_k2, subln_g, w_out, norm2_g, router_w, router_b, w1, b1, w2, b2):
    bsz, seq, d = x.shape
    n_tok = bsz * seq
    assert d == D_MODEL and n_tok % TM_PROJ == 0 and seq % T_SCAN == 0 and seq % TQ == 0 and n_tok % T_TOK == 0
    assert (n_tok * TOP_K) % BM == 0
    assert norm1_g.shape[0] == 1, "single-layer stack"
    x2 = x.reshape(n_tok, d)

    z = _in_proj(x2, norm1_g[0][None, :], w_in[0].astype(BF16))
    z3 = z.reshape(bsz, seq, D_IN)

    n_slabs = D_RNN // LANES
    per_slab = LANES // RNN_BLOCK
    wa = lru_wa[0].reshape(n_slabs, per_slab, RNN_BLOCK, RNN_BLOCK)
    wx = lru_wx[0].reshape(n_slabs, per_slab, RNN_BLOCK, RNN_BLOCK)
    wg = jnp.concatenate([jax.vmap(_block_diag)(wa), jax.vmap(_block_diag)(wx)], axis=2).astype(BF16)
    bg = jnp.concatenate([lru_ba[0].reshape(n_slabs, 1, LANES), lru_bx[0].reshape(n_slabs, 1, LANES)], axis=2)
    y_rnn = _rnn(z3, conv_w[0], conv_b[0][None, :], wg, bg, lru_lambda[0].reshape(n_slabs, 1, LANES))

    half = jnp.arange(LANES) // HEAD_DIM
    ones_bd = (half[:, None] == half[None, :]).astype(BF16)
    y_attn = _attn(z3, jnp.tile(q_norm_g[0], 2)[None, :], jnp.tile(k_norm_g[0], 2)[None, :],
                   lambda_q1[0][None, :], lambda_k1[0][None, :], lambda_q2[0][None, :], lambda_k2[0][None, :],
                   subln_g[0][None, :], ones_bd)

    rw = jnp.pad(router_w[0], ((0, 0), (0, LANES - N_EXPERTS)))
    rw_hi = rw.astype(BF16)
    rw_lo = (rw - rw_hi.astype(F32)).astype(BF16)
    rw_parts = jnp.concatenate([rw_hi, rw_lo], axis=1)
    rb = jnp.pad(router_b[0], (0, LANES - N_EXPERTS))[None, :]
    tok = jnp.arange(T_TOK)
    tri = (tok[:, None] < tok[None, :]).astype(BF16)
    x1, h2, route, gates, counts = _out_proj(
        y_rnn.reshape(n_tok, D_RNN), y_attn.reshape(n_tok, D_ATTN), x2,
        w_out[0].astype(BF16).reshape(2, D_RNN, D_MODEL), norm2_g[0][None, :], rw_parts, rb, tri)

    n_blocks = (n_tok * TOP_K) // BM + N_EXPERTS
    tile_counts = counts[:, 0].reshape(n_tok // T_TOK, N_EXPERTS)
    run_dst, run_len, run_off, off_lanes, blk_e, n_valid, pad_end, padded = _plan(tile_counts, n_blocks)
    xs = _dispatch(h2, route, off_lanes, run_dst, run_len, run_off, pad_end, padded, n_valid, n_blocks * BM)
    ys = _experts(xs, blk_e, n_valid, w1[0], b1[0][:, None, :], w2[0], b2[0][:, None, :])
    out = _combine(x1, route, gates, off_lanes, run_dst, run_len, run_off, ys)
    return out.reshape(bsz, seq, d)
```
